```python
import math
import jax, jax.numpy as jnp
from jax import lax
import numpy as np

D_MODEL = 1024
BATCH = 8
SEQ = 8192
DEPTH = 2
DEC_BATCH = 4
DEC_SEQ = 4096
PAST_LEN = 128

RW_HEADS = 4
RW_HEAD = 64
RW_W = RW_HEADS * RW_HEAD
RW_LORA_W = 32
RW_LORA_A = 32
RW_LORA_G = 64
RW_GN_EPS = 64e-5
MLA_HEADS = 8
MLA_NOPE = 64
MLA_ROPE = 32
MLA_V = 64
MLA_Q_LORA = 256
MLA_KV_LORA = 128
MLA_W = MLA_HEADS * MLA_V
MLA_SCALE = (MLA_NOPE + MLA_ROPE) ** -0.5
ROPE_THETA = 10000.0
Q_BLOCK = 128
RMS_EPS = 1e-6
S5_W = 256
S5_GROUP = 16
S5_GROUPS = S5_W // S5_GROUP
S5_STATE = 64
D_FF = 4 * D_MODEL
LN_EPS = 1e-5
N_BRANCH = 3
DN_ALPHA = (2 * DEPTH) ** 0.25
DN_BETA = (8 * DEPTH) ** -0.25
RW_SHIFT_W = 3 * RW_W + RW_LORA_W + RW_LORA_A + RW_LORA_G
MLA_IN_W = MLA_Q_LORA + MLA_KV_LORA + MLA_ROPE
OFF_MLA = RW_SHIFT_W
OFF_S5 = OFF_MLA + MLA_IN_W
OFF_GATE = OFF_S5 + S5_W
D_IN = OFF_GATE + N_BRANCH * D_MODEL

kernel_name = 'hybrid_rwkv7_mla_s5_encoder'


def _layer_norm(x, g, b):
    xf = x.astype(jnp.float32)
    mu = jnp.mean(xf, -1, keepdims=True)
    var = jnp.mean(jnp.square(xf - mu), -1, keepdims=True)
    return ((xf - mu) * lax.rsqrt(var + LN_EPS) * g.astype(jnp.float32) + b.astype(jnp.float32)).astype(x.dtype)


def _rms_norm(x, g):
    xf = x.astype(jnp.float32)
    return (xf * lax.rsqrt(jnp.mean(xf * xf, -1, keepdims=True) + RMS_EPS) * g.astype(jnp.float32)).astype(x.dtype)


def _centred_shift(z):
    zp = jnp.pad(z, ((0, 0), (1, 1), (0, 0)))
    return 0.5 * (zp[:, :-2] + zp[:, 2:])


def _rwkv_scan(r, w, kk, a, k, v, reverse):
    bsz, h, n = r.shape[1:]

    def step(S, inp):
        r_t, w_t, kk_t, a_t, k_t, v_t = inp
        sa = jnp.einsum('bhij,bhj->bhi', S, kk_t)
        S = (S * w_t[:, :, None, :]
             - sa[..., None] * (kk_t * a_t)[:, :, None, :]
             + v_t[..., None] * k_t[:, :, None, :])
        return S, jnp.einsum('bhij,bhj->bhi', S, r_t)

    S0 = jnp.zeros((bsz, h, n, n), jnp.float32)
    _, o = lax.scan(step, S0, (r, w, kk, a, k, v), reverse=reverse)
    return o


def _rwkv_mixer(z, mu, w0, w2, a0, a2, g2, k_k, k_a, r_k, gn_g, gn_b):
    f32 = jnp.float32
    bsz, seq, _ = z.shape
    z = z + mu * (_centred_shift(z) - z)
    r = z[..., 0:RW_W]
    k = z[..., RW_W:2 * RW_W]
    v = z[..., 2 * RW_W:3 * RW_W]
    xw = z[..., 3 * RW_W:3 * RW_W + RW_LORA_W]
    xa = z[..., 3 * RW_W + RW_LORA_W:3 * RW_W + RW_LORA_W + RW_LORA_A]
    xg = z[..., 3 * RW_W + RW_LORA_W + RW_LORA_A:]

    def heads(t):
        return t.astype(f32).reshape(bsz, seq, RW_HEADS, RW_HEAD)

    def time_major(t):
        return heads(t).transpose(1, 0, 2, 3)

    kk = heads(k * k_k)
    kk = kk * lax.rsqrt(jnp.maximum(jnp.sum(kk * kk, -1, keepdims=True), 1e-12))
    kk_t = kk.transpose(1, 0, 2, 3)
    r_t = time_major(r)
    v_t = time_major(v)
    o = jnp.zeros((seq, bsz, RW_HEADS, RW_HEAD), f32)
    for d in range(2):
        w_log = -jax.nn.softplus(-(w0[d] + jnp.tanh(xw) @ w2[d]).astype(f32)) - 0.5
        decay = jnp.exp(-jnp.exp(w_log))
        a = jax.nn.sigmoid((a0[d] + xa @ a2[d]).astype(f32))
        k_d = k.astype(f32) * (1.0 + (a - 1.0) * k_a.astype(f32))
        o = o + _rwkv_scan(r_t, time_major(decay), kk_t, time_major(a), time_major(k_d), v_t, d == 1)
    o = o.transpose(1, 0, 2, 3)
    mean = jnp.mean(o, -1, keepdims=True)
    var = jnp.mean(jnp.square(o - mean), -1, keepdims=True)
    o = ((o - mean) * lax.rsqrt(var + RW_GN_EPS)).reshape(bsz, seq, RW_W) * gn_g.astype(f32) + gn_b.astype(f32)
    bonus = jnp.sum(heads(r) * heads(k) * r_k.astype(f32).reshape(RW_HEADS, RW_HEAD), -1, keepdims=True) * heads(v)
    g = (jax.nn.sigmoid(xg) @ g2).astype(f32)
    return ((o + bonus.reshape(bsz, seq, RW_W)) * g).astype(z.dtype)


def _rope(t, seq):
    f32 = jnp.float32
    half = MLA_ROPE // 2
    inv = ROPE_THETA ** (-jnp.arange(half, dtype=f32) / half)
    ang = jnp.arange(seq, dtype=f32)[:, None] * inv[None, :]
    shape = (1, seq) + (1,) * (t.ndim - 3) + (half,)
    cos = jnp.cos(ang).reshape(shape)
    sin = jnp.sin(ang).reshape(shape)
    tf = t.astype(f32)
    t1, t2 = tf[..., :half], tf[..., half:]
    return jnp.concatenate([t1 * cos - t2 * sin, t1 * sin + t2 * cos], -1).astype(t.dtype)


def _mla_mixer(z, q_norm, w_uq, kv_norm, w_ukv):
    f32 = jnp.float32
    bsz, seq, _ = z.shape
    c_q = z[..., :MLA_Q_LORA]
    c_kv = z[..., MLA_Q_LORA:MLA_Q_LORA + MLA_KV_LORA]
    k_rope = _rope(z[..., MLA_Q_LORA + MLA_KV_LORA:], seq)
    q = (_rms_norm(c_q, q_norm) @ w_uq).reshape(bsz, seq, MLA_HEADS, MLA_NOPE + MLA_ROPE)
    kv = (_rms_norm(c_kv, kv_norm) @ w_ukv).reshape(bsz, seq, MLA_HEADS, MLA_NOPE + MLA_V)
    q_nope = q[..., :MLA_NOPE]
    q_rope = _rope(q[..., MLA_NOPE:], seq)
    k_nope = kv[..., :MLA_NOPE]
    v = kv[..., MLA_NOPE:]
    n_blk = seq // Q_BLOCK

    def blocks(t):
        return t.reshape((bsz, n_blk, Q_BLOCK) + t.shape[2:]).swapaxes(0, 1)

    def attend(qs):
        qn, qr = qs
        s = (jnp.einsum('bqhd,bkhd->bhqk', qn, k_nope)
             + jnp.einsum('bqhr,bkr->bhqk', qr, k_rope)).astype(f32) * MLA_SCALE
        p = jax.nn.softmax(s, axis=-1).astype(v.dtype)
        return jnp.einsum('bhqk,bkhd->bqhd', p, v)

    o = lax.map(attend, (blocks(q_nope), blocks(q_rope)))
    return o.swapaxes(0, 1).reshape(bsz, seq, MLA_W)


def _s5_direction(u, lam_re, lam_im, log_dt, b_re, b_im, c_re, c_im, reverse):
    dt = jnp.exp(log_dt)[:, None]
    mag = jnp.exp(lam_re * dt)
    ab_re = mag * jnp.cos(lam_im * dt)
    ab_im = mag * jnp.sin(lam_im * dt)
    den = lam_re * lam_re + lam_im * lam_im
    nr = ab_re - 1.0
    ni = ab_im
    coef_re = (nr * lam_re + ni * lam_im) / den
    coef_im = (ni * lam_re - nr * lam_im) / den
    bb_re = coef_re[..., None] * b_re - coef_im[..., None] * b_im
    bb_im = coef_re[..., None] * b_im + coef_im[..., None] * b_re
    bu_re = jnp.einsum('blgc,gpc->blgp', u, bb_re)
    bu_im = jnp.einsum('blgc,gpc->blgp', u, bb_im)
    a_re = jnp.broadcast_to(ab_re, bu_re.shape)
    a_im = jnp.broadcast_to(ab_im, bu_re.shape)

    def combine(e1, e2):
        a1r, a1i, b1r, b1i = e1
        a2r, a2i, b2r, b2i = e2
        return (a2r * a1r - a2i * a1i,
                a2r * a1i + a2i * a1r,
                a2r * b1r - a2i * b1i + b2r,
                a2r * b1i + a2i * b1r + b2i)

    _, _, x_re, x_im = lax.associative_scan(combine, (a_re, a_im, bu_re, bu_im), reverse=reverse, axis=1)
    return jnp.einsum('blgp,gcp->blgc', x_re, c_re) - jnp.einsum('blgp,gcp->blgc', x_im, c_im)


def _s5_mixer(u, lam_re, lam_im, log_dt, b_re, b_im, c_re, c_im, d_skip, glu_w, glu_b):
    f32 = jnp.float32
    bsz, seq, _ = u.shape
    uf = u.astype(f32)
    ug = uf.reshape(bsz, seq, S5_GROUPS, S5_GROUP)
    y = jnp.zeros_like(ug)
    for d in range(2):
        y = y + _s5_direction(ug, lam_re[d].astype(f32), lam_im[d].astype(f32), log_dt[d].astype(f32),
                              b_re[d].astype(f32), b_im[d].astype(f32),
                              c_re[d].astype(f32), c_im[d].astype(f32), d == 1)
    y = y.reshape(bsz, seq, S5_W) + d_skip.astype(f32) * uf
    y = jax.nn.gelu(y).astype(u.dtype)
    return y * jax.nn.sigmoid(y @ glu_w + glu_b)


def _layer(x, w_in, rw_mu, rw_w0, rw_w2, rw_a0, rw_a2, rw_g2, rw_k_k, rw_k_a, rw_r_k, rw_gn_g, rw_gn_b, rw_proj,
           mla_q_norm, mla_w_uq, mla_kv_norm, mla_w_ukv, mla_proj,
           s5_lam_re, s5_lam_im, s5_log_dt, s5_b_re, s5_b_im, s5_c_re, s5_c_im, s5_d, s5_glu_w, s5_glu_b, s5_proj,
           w_out, ln1_g, ln1_b, mlp_w1, mlp_w2, ln2_g, ln2_b):
    bsz, seq, _ = x.shape
    z = x @ w_in
    y_rw = _rwkv_mixer(z[..., :OFF_MLA], rw_mu, rw_w0, rw_w2, rw_a0, rw_a2, rw_g2,
                       rw_k_k, rw_k_a, rw_r_k, rw_gn_g, rw_gn_b) @ rw_proj
    y_mla = _mla_mixer(z[..., OFF_MLA:OFF_S5], mla_q_norm, mla_w_uq, mla_kv_norm, mla_w_ukv) @ mla_proj
    y_s5 = _s5_mixer(z[..., OFF_S5:OFF_GATE], s5_lam_re, s5_lam_im, s5_log_dt, s5_b_re, s5_b_im,
                     s5_c_re, s5_c_im, s5_d, s5_glu_w, s5_glu_b) @ s5_proj
    gates = jax.nn.sigmoid(z[..., OFF_GATE:].astype(jnp.float32)).astype(x.dtype)
    gates = gates.reshape(bsz, seq, N_BRANCH, D_MODEL)
    merged = gates[:, :, 0] * y_rw + gates[:, :, 1] * y_mla + gates[:, :, 2] * y_s5
    x = _layer_norm(DN_ALPHA * x + merged @ w_out, ln1_g, ln1_b)
    h = jnp.square(jax.nn.relu(x @ mlp_w1))
    return _layer_norm(DN_ALPHA * x + h @ mlp_w2, ln2_g, ln2_b)


def _trunk(x, weights):
    for l in range(DEPTH):
        x = _layer(x, *[w[l] for w in weights])
    return x


def setup_inputs(seed: int = 0) -> dict:
    key = jax.random.key(seed)
    ks = iter(jax.random.split(key, 48))
    f32 = jnp.float32

    def nrm(shape, scale):
        return scale * jax.random.normal(next(ks), shape, f32)

    decay_base = jnp.broadcast_to(jnp.linspace(-6.0, -1.0, RW_HEAD, dtype=f32), (RW_HEADS, RW_HEAD)).reshape(RW_W)
    lam_im_base = jnp.pi * jnp.arange(S5_STATE, dtype=f32)
    return {
        'x_prompt': nrm((BATCH, SEQ, D_MODEL), 1.0),
        'x_sample': nrm((DEC_BATCH, DEC_SEQ, D_MODEL), 1.0),
        'w_in': nrm((DEPTH, D_MODEL, D_IN), D_MODEL ** -0.5),
        'rw_mu': jax.random.uniform(next(ks), (DEPTH, RW_SHIFT_W), f32),
        'rw_w0': decay_base + nrm((DEPTH, 2, RW_W), 0.1),
        'rw_w2': nrm((DEPTH, 2, RW_LORA_W, RW_W), 0.1),
        'rw_a0': nrm((DEPTH, 2, RW_W), 0.1),
        'rw_a2': nrm((DEPTH, 2, RW_LORA_A, RW_W), 0.5 * RW_LORA_A ** -0.5),
        'rw_g2': nrm((DEPTH, RW_LORA_G, RW_W), RW_LORA_G ** -0.5),
        'rw_k_k': 0.85 + nrm((DEPTH, RW_W), 0.05),
        'rw_k_a': 1.0 + nrm((DEPTH, RW_W), 0.05),
        'rw_r_k': nrm((DEPTH, RW_W), 0.1),
        'rw_gn_g': 1.0 + nrm((DEPTH, RW_W), 0.05),
        'rw_gn_b': nrm((DEPTH, RW_W), 0.02),
        'rw_proj': nrm((DEPTH, RW_W, D_MODEL), RW_W ** -0.5),
        'mla_q_norm': 1.0 + nrm((DEPTH, MLA_Q_LORA), 0.05),
        'mla_w_uq': nrm((DEPTH, MLA_Q_LORA, MLA_HEADS * (MLA_NOPE + MLA_ROPE)), MLA_Q_LORA ** -0.5),
        'mla_kv_norm': 1.0 + nrm((DEPTH, MLA_KV_LORA), 0.05),
        'mla_w_ukv': nrm((DEPTH, MLA_KV_LORA, MLA_HEADS * (MLA_NOPE + MLA_V)), MLA_KV_LORA ** -0.5),
        'mla_proj': nrm((DEPTH, MLA_W, D_MODEL), MLA_W ** -0.5),
        's5_lam_re': -0.5 + nrm((DEPTH, 2, S5_GROUPS, S5_STATE), 0.01),
        's5_lam_im': lam_im_base + nrm((DEPTH, 2, S5_GROUPS, S5_STATE), 0.01),
        's5_log_dt': jax.random.uniform(next(ks), (DEPTH, 2, S5_GROUPS), f32,
                                        minval=math.log(1e-3), maxval=math.log(1e-1)),
        's5_b_re': nrm((DEPTH, 2, S5_GROUPS, S5_STATE, S5_GROUP), (2 * S5_GROUP) ** -0.5),
        's5_b_im': nrm((DEPTH, 2, S5_GROUPS, S5_STATE, S5_GROUP), (2 * S5_GROUP) ** -0.5),
        's5_c_re': nrm((DEPTH, 2, S5_GROUPS, S5_GROUP, S5_STATE), S5_STATE ** -0.5),
        's5_c_im': nrm((DEPTH, 2, S5_GROUPS, S5_GROUP, S5_STATE), S5_STATE ** -0.5),
        's5_d': nrm((DEPTH, S5_W), 1.0),
        's5_glu_w': nrm((DEPTH, S5_W, S5_W), S5_W ** -0.5),
        's5_glu_b': nrm((DEPTH, S5_W), 0.02),
        's5_proj': nrm((DEPTH, S5_W, D_MODEL), S5_W ** -0.5),
        'w_out': nrm((DEPTH, D_MODEL, D_MODEL), DN_BETA * D_MODEL ** -0.5),
        'ln1_g': 1.0 + nrm((DEPTH, D_MODEL), 0.05),
        'ln1_b': nrm((DEPTH, D_MODEL), 0.02),
        'mlp_w1': nrm((DEPTH, D_MODEL, D_FF), D_MODEL ** -0.5),
        'mlp_w2': nrm((DEPTH, D_FF, D_MODEL), DN_BETA * D_FF ** -0.5),
        'ln2_g': 1.0 + nrm((DEPTH, D_MODEL), 0.05),
        'ln2_b': nrm((DEPTH, D_MODEL), 0.02),
    }


def reference(x_prompt, x_sample, w_in, rw_mu, rw_w0, rw_w2, rw_a0, rw_a2, rw_g2, rw_k_k, rw_k_a, rw_r_k,
              rw_gn_g, rw_gn_b, rw_proj, mla_q_norm, mla_w_uq, mla_kv_norm, mla_w_ukv, mla_proj,
              s5_lam_re, s5_lam_im, s5_log_dt, s5_b_re, s5_b_im, s5_c_re, s5_c_im, s5_d, s5_glu_w, s5_glu_b,
              s5_proj, w_out, ln1_g, ln1_b, mlp_w1, mlp_w2, ln2_g, ln2_b):
    weights = (w_in, rw_mu, rw_w0, rw_w2, rw_a0, rw_a2, rw_g2, rw_k_k, rw_k_a, rw_r_k, rw_gn_g, rw_gn_b, rw_proj,
               mla_q_norm, mla_w_uq, mla_kv_norm, mla_w_ukv, mla_proj,
               s5_lam_re, s5_lam_im, s5_log_dt, s5_b_re, s5_b_im, s5_c_re, s5_c_im, s5_d, s5_glu_w, s5_glu_b,
               s5_proj, w_out, ln1_g, ln1_b, mlp_w1, mlp_w2, ln2_g, ln2_b)
    y_prompt = _trunk(x_prompt, weights)
    y_sample = _trunk(x_sample, weights)
    return (y_prompt, y_sample)
```

```python
import functools
import math

import numpy as np
import jax
import jax.numpy as jnp
from jax import lax
from jax.experimental import pallas as pl
from jax.experimental.pallas import tpu as pltpu

F32 = jnp.float32
BF16 = jnp.bfloat16
HIGHEST = lax.Precision.HIGHEST

D_MODEL = 1024
DEPTH = 2
RW_HEADS = 4
RW_HEAD = 64
RW_W = RW_HEADS * RW_HEAD
RW_LORA_W = 32
RW_LORA_A = 32
RW_LORA_G = 64
RW_GN_EPS = 64e-5
RW_IN = 3 * RW_W + RW_LORA_W + RW_LORA_A + RW_LORA_G
MLA_HEADS = 8
MLA_NOPE = 64
MLA_ROPE = 32
MLA_V = 64
MLA_Q_LORA = 256
MLA_KV_LORA = 128
MLA_W = MLA_HEADS * MLA_V
MLA_QK_PAD = 128
MLA_IN_PAD = MLA_Q_LORA + MLA_KV_LORA + MLA_QK_PAD
MLA_SCALE = (MLA_NOPE + MLA_ROPE) ** -0.5
ROPE_THETA = 10000.0
RMS_EPS = 1e-6
S5_W = 256
S5_GROUP = 16
S5_GROUPS = S5_W // S5_GROUP
S5_STATE = 64
S5_N = S5_GROUPS * S5_STATE
S5_BATCH = 8
D_FF = 4 * D_MODEL
LN_EPS = 1e-5
N_BRANCH = 3
DN_ALPHA = (2 * DEPTH) ** 0.25
OFF_MLA = RW_IN
OFF_S5 = OFF_MLA + MLA_Q_LORA + MLA_KV_LORA + MLA_ROPE
OFF_GATE = OFF_S5 + S5_W
RW_CHUNK = 64
VMEM_LIMIT = 56 * 1024 * 1024


def _params(*sem):
    return pltpu.CompilerParams(dimension_semantics=sem, vmem_limit_bytes=VMEM_LIMIT)


def _mm(a, b):
    return jnp.dot(a.astype(BF16), b.astype(BF16), preferred_element_type=F32)


def _mm_nt(a, b):
    return lax.dot_general(a.astype(BF16), b.astype(BF16), (((1,), (1,)), ((), ())),
                           preferred_element_type=F32)


def _mm_tn(a, b):
    return lax.dot_general(a.astype(BF16), b.astype(BF16), (((0,), (0,)), ((), ())),
                           preferred_element_type=F32)


def _mm_f32(a, b):
    return jnp.dot(a, b, preferred_element_type=F32, precision=HIGHEST)


def _sigmoid(x):
    return 1.0 / (1.0 + jnp.exp(-x))


def _full(a):
    nd = a.ndim
    return pl.BlockSpec(a.shape, lambda *_: (0,) * nd)


def _layer_norm(x, g, b):
    mu = jnp.mean(x, -1, keepdims=True)
    xc = x - mu
    var = jnp.mean(xc * xc, -1, keepdims=True)
    return xc * lax.rsqrt(var + LN_EPS) * g + b


def _inproj_kernel(x_ref, wrw_ref, wmla_ref, ws5_ref, wg_ref, zrw_ref, zmla_ref, zs5_ref, g_ref):
    xb = x_ref[...].astype(BF16)
    zrw_ref[...] = jnp.dot(xb, wrw_ref[...], preferred_element_type=F32)
    zmla_ref[...] = jnp.dot(xb, wmla_ref[...], preferred_element_type=F32)
    zs5_ref[...] = jnp.dot(xb, ws5_ref[...], preferred_element_type=F32)
    g_ref[...] = _sigmoid(jnp.dot(xb, wg_ref[...], preferred_element_type=F32))


def _inproj(x, w_rw, w_mla, w_s5, w_g, tm):
    t = x.shape[0]
    row = lambda n: pl.BlockSpec((tm, n), lambda i: (i, 0))
    return pl.pallas_call(
        _inproj_kernel,
        grid=(t // tm,),
        in_specs=[row(D_MODEL), _full(w_rw), _full(w_mla), _full(w_s5), _full(w_g)],
        out_specs=[row(RW_IN), row(MLA_IN_PAD), row(S5_W), row(N_BRANCH * D_MODEL)],
        out_shape=[jax.ShapeDtypeStruct((t, RW_IN), F32), jax.ShapeDtypeStruct((t, MLA_IN_PAD), F32),
                   jax.ShapeDtypeStruct((t, S5_W), F32), jax.ShapeDtypeStruct((t, N_BRANCH * D_MODEL), F32)],
        compiler_params=_params("parallel"),
        name="inproj",
    )(x, w_rw, w_mla, w_s5, w_g)


def _rwkv_kernel(reverse, nblk, tb,
                 z_ref, zp_ref, zn_ref, mu_ref, w0_ref, w2_ref, a0_ref, a2_ref, g2_ref,
                 kk_ref, ka_ref, rk_ref, *rest):
    if reverse:
        o_ref, r_s, v_s, kn_s, lw_s, b_s, kd_s, state = rest
    else:
        o_ref, bonus_ref, gate_ref, r_s, v_s, kn_s, lw_s, b_s, kd_s, state = rest
    step = pl.program_id(1)
    blk = (nblk - 1 - step) if reverse else step
    c = RW_CHUNK
    w = RW_W

    z = z_ref[...]
    prev_row = jnp.where(blk == 0, 0.0, zp_ref[7:8, :])
    next_row = jnp.where(blk == nblk - 1, 0.0, zn_ref[0:1, :])
    rows = lax.broadcasted_iota(jnp.int32, z.shape, 0)
    z_prev = jnp.where(rows == 0, prev_row, pltpu.roll(z, 1, axis=0))
    z_next = jnp.where(rows == tb - 1, next_row, pltpu.roll(z, tb - 1, axis=0))
    z = z + mu_ref[...] * (0.5 * (z_prev + z_next) - z)
    r = z[:, 0:w]
    k = z[:, w:2 * w]
    v = z[:, 2 * w:3 * w]
    xw = z[:, 3 * w:3 * w + RW_LORA_W]
    xa = z[:, 3 * w + RW_LORA_W:3 * w + RW_LORA_W + RW_LORA_A]
    xg = z[:, 3 * w + RW_LORA_W + RW_LORA_A:]

    hr = lax.broadcasted_iota(jnp.int32, (w, w), 0) // RW_HEAD
    hc = lax.broadcasted_iota(jnp.int32, (w, w), 1) // RW_HEAD
    same_head = hr == hc
    head_ones = same_head.astype(F32)

    kk = k * kk_ref[...]
    kk_ss = _mm_f32(kk * kk, head_ones)
    kk = kk * lax.rsqrt(jnp.maximum(kk_ss, 1e-12))
    y = w0_ref[...] + _mm_f32(jnp.tanh(xw), w2_ref[...])
    lw = -math.exp(-0.5) * _sigmoid(y)
    a = _sigmoid(a0_ref[...] + _mm_f32(xa, a2_ref[...]))
    r_s[...] = r
    v_s[...] = v
    kn_s[...] = kk
    lw_s[...] = lw
    b_s[...] = kk * a
    kd_s[...] = k * (1.0 + (a - 1.0) * ka_ref[...])
    if not reverse:
        rk = _mm_f32(r * k * rk_ref[...], head_ones)
        bonus_ref[...] = rk * v
        gate_ref[...] = _mm_f32(_sigmoid(xg), g2_ref[...])

    @pl.when(step == 0)
    def _():
        state[...] = jnp.zeros_like(state)

    ti = lax.broadcasted_iota(jnp.int32, (c, c), 0)
    si = lax.broadcasted_iota(jnp.int32, (c, c), 1)
    cum_mat = ((si >= ti) if reverse else (si <= ti)).astype(F32)
    rr = lax.broadcasted_iota(jnp.int32, (w, w), 0)
    cc = lax.broadcasted_iota(jnp.int32, (w, w), 1)
    strict = (rr < cc) if reverse else (rr > cc)
    incl = (rr <= cc) if reverse else (rr >= cc)
    eye = (rr == cc).astype(F32)

    def stack(x):
        return jnp.where(same_head, jnp.concatenate([x] * RW_HEADS, axis=0), 0.0)

    nchunk = tb // c

    def chunk(ci, carry):
        cpos = (nchunk - 1 - ci) if reverse else ci
        sl = pl.ds(pl.multiple_of(cpos * c, c), c)
        lwc = lw_s[sl, :]
        l_in = _mm_f32(cum_mat, lwc)
        l_tot = jnp.sum(lwc, axis=0, keepdims=True)
        e_in = jnp.exp(l_in)
        e_neg = jnp.exp(-l_in)
        e_tot = jnp.exp(l_tot)
        kap = stack(kn_s[sl, :] * jnp.exp(l_in - lwc)).astype(BF16)
        bt = b_s[sl, :] * e_neg
        kt = kd_s[sl, :] * e_neg
        bts = stack(bt).astype(BF16)
        kts = stack(kt).astype(BF16)
        rts = stack(r_s[sl, :] * e_in).astype(BF16)
        vs = stack(v_s[sl, :]).astype(BF16)
        a_b = jnp.where(strict, _mm_nt(kap, bts), 0.0)
        a_k = jnp.where(strict, _mm_nt(kap, kts), 0.0)
        m_b = jnp.where(incl, _mm_nt(rts, bts), 0.0)
        m_k = jnp.where(incl, _mm_nt(rts, kts), 0.0)
        n = -a_b
        tinv = eye + n
        p = n
        for _ in range(int(math.log2(c)) - 1):
            p = _mm(p, p)
            tinv = tinv + _mm(tinv, p)
        pm = _mm(tinv, kap)
        qm = _mm(tinv, _mm(a_k, vs))
        s_bd = state[...]
        u = -(_mm_nt(pm, s_bd) + qm)
        o_st = _mm_nt(rts, s_bd) + _mm(m_b, u) + _mm(m_k, vs)
        o = o_st[0:c] + o_st[c:2 * c] + o_st[2 * c:3 * c] + o_st[3 * c:4 * c]
        o_ref[sl, :] = o
        upd = _mm_tn(u, stack(bt * e_tot)) + _mm_tn(vs, stack(kt * e_tot))
        state[...] = s_bd * e_tot + upd
        return carry

    lax.fori_loop(0, nchunk, chunk, 0)


def _rwkv_scan(z, mu, w0, w2, a0, a2, g2, k_k, k_a, r_k, reverse, tb):
    bsz, seq, _ = z.shape
    nblk = seq // tb
    z8 = z.reshape(bsz, seq // 8, 8, RW_IN)
    t8 = tb // 8

    def pos(i):
        return (nblk - 1 - i) if reverse else i

    z_spec = pl.BlockSpec((None, tb, RW_IN), lambda b, i: (b, pos(i), 0))
    zp_spec = pl.BlockSpec((None, None, 8, RW_IN), lambda b, i: (b, jnp.maximum(pos(i) * t8 - 1, 0), 0, 0))
    zn_spec = pl.BlockSpec((None, None, 8, RW_IN),
                           lambda b, i: (b, jnp.minimum((pos(i) + 1) * t8, seq // 8 - 1), 0, 0))
    o_spec = pl.BlockSpec((None, tb, RW_W), lambda b, i: (b, pos(i), 0))
    o_shape = jax.ShapeDtypeStruct((bsz, seq, RW_W), F32)
    weights = [mu, w0, w2, a0, a2, g2, k_k, k_a, r_k]
    n_out = 1 if reverse else 3
    return pl.pallas_call(
        functools.partial(_rwkv_kernel, reverse, nblk, tb),
        grid=(bsz, nblk),
        in_specs=[z_spec, zp_spec, zn_spec] + [_full(wt) for wt in weights],
        out_specs=[o_spec] * n_out,
        out_shape=[o_shape] * n_out,
        scratch_shapes=[pltpu.VMEM((tb, RW_W), F32)] * 6 + [pltpu.VMEM((RW_W, RW_W), F32)],
        compiler_params=_params("parallel", "arbitrary"),
        name="rwkv_bwd" if reverse else "rwkv_fwd",
    )(z, z8, z8, *weights)


def _rope_tables(seq):
    half = MLA_ROPE // 2
    inv = (ROPE_THETA ** (-np.arange(half, dtype=np.float32) / half)).astype(np.float32)
    ang = np.arange(seq, dtype=np.float32)[:, None] * inv[None, :]
    cos = np.cos(ang).astype(np.float32)
    sin = np.sin(ang).astype(np.float32)
    ct = np.zeros((seq, MLA_QK_PAD), np.float32)
    st = np.zeros((seq, MLA_QK_PAD), np.float32)
    ct[:, :MLA_NOPE] = 1.0
    ct[:, MLA_NOPE:MLA_NOPE + half] = cos
    ct[:, MLA_NOPE + half:MLA_NOPE + 2 * half] = cos
    st[:, MLA_NOPE:MLA_NOPE + half] = -sin
    st[:, MLA_NOPE + half:MLA_NOPE + 2 * half] = sin
    return jnp.asarray(ct), jnp.asarray(st)


def _rope(x, cos_t, sin_t):
    half = MLA_ROPE // 2
    lane = lax.broadcasted_iota(jnp.int32, x.shape, 1)
    swapped = jnp.where(lane < MLA_NOPE + half,
                        pltpu.roll(x, MLA_QK_PAD - half, axis=1), pltpu.roll(x, half, axis=1))
    return x * cos_t + swapped * sin_t


def _mla_qkv_kernel(z_ref, cos_ref, sin_ref, qn_ref, wuq_ref, kvn_ref, wuk_ref, wuv_ref, q_ref, k_ref, v_ref):
    z = z_ref[...]
    cos_t = cos_ref[...]
    sin_t = sin_ref[...]
    c_q = z[:, :MLA_Q_LORA]
    c_kv = z[:, MLA_Q_LORA:MLA_Q_LORA + MLA_KV_LORA]
    k_rope = _rope(z[:, MLA_Q_LORA + MLA_KV_LORA:], cos_t, sin_t)
    c_q = (c_q * lax.rsqrt(jnp.mean(c_q * c_q, -1, keepdims=True) + RMS_EPS) * qn_ref[...]).astype(BF16)
    c_kv = (c_kv * lax.rsqrt(jnp.mean(c_kv * c_kv, -1, keepdims=True) + RMS_EPS) * kvn_ref[...]).astype(BF16)
    for h in range(MLA_HEADS):
        q = jnp.dot(c_q, wuq_ref[h], preferred_element_type=F32)
        q_ref[h] = (_rope(q, cos_t, sin_t) * MLA_SCALE).astype(BF16)
        kh = jnp.dot(c_kv, wuk_ref[h], preferred_element_type=F32)
        k_ref[h] = (kh + k_rope).astype(BF16)
        v_ref[h] = jnp.dot(c_kv, wuv_ref[h], preferred_element_type=F32).astype(BF16)


def _mla_qkv(z, cos_t, sin_t, q_norm, w_uq, kv_norm, w_uk, w_uv, seq, tm):
    t = z.shape[0]
    nseq = seq // tm
    tab = pl.BlockSpec((tm, MLA_QK_PAD), lambda i: (i % nseq, 0))
    head = lambda n: pl.BlockSpec((MLA_HEADS, tm, n), lambda i: (0, i, 0))
    return pl.pallas_call(
        _mla_qkv_kernel,
        grid=(t // tm,),
        in_specs=[pl.BlockSpec((tm, MLA_IN_PAD), lambda i: (i, 0)), tab, tab,
                  _full(q_norm), _full(w_uq), _full(kv_norm), _full(w_uk), _full(w_uv)],
        out_specs=[head(MLA_QK_PAD), head(MLA_QK_PAD), head(MLA_V)],
        out_shape=[jax.ShapeDtypeStruct((MLA_HEADS, t, MLA_QK_PAD), BF16),
                   jax.ShapeDtypeStruct((MLA_HEADS, t, MLA_QK_PAD), BF16),
                   jax.ShapeDtypeStruct((MLA_HEADS, t, MLA_V), BF16)],
        compiler_params=_params("parallel"),
        name="mla_qkv",
    )(z, cos_t, sin_t, q_norm, w_uq, kv_norm, w_uk, w_uv)


def _mla_attn_kernel(nk, q_ref, k_ref, v_ref, o_ref, m_s, l_s, acc_s):
    j = pl.program_id(2)

    @pl.when(j == 0)
    def _():
        m_s[...] = jnp.full(m_s.shape, -jnp.inf, F32)
        l_s[...] = jnp.zeros_like(l_s)
        acc_s[...] = jnp.zeros_like(acc_s)

    for h in range(MLA_HEADS):
        s = lax.dot_general(q_ref[h], k_ref[h], (((1,), (1,)), ((), ())), preferred_element_type=F32)
        m_prev = m_s[h][:, :1]
        m_new = jnp.maximum(m_prev, jnp.max(s, axis=-1, keepdims=True))
        alpha = jnp.exp(m_prev - m_new)
        p = jnp.exp(s - m_new)
        l_s[h] = alpha * l_s[h] + jnp.sum(p, axis=-1, keepdims=True)
        acc_s[h] = alpha * acc_s[h] + jnp.dot(p.astype(BF16), v_ref[h], preferred_element_type=F32)
        m_s[h] = jnp.broadcast_to(m_new, m_s.shape[1:])

    @pl.when(j == nk - 1)
    def _():
        for h in range(MLA_HEADS):
            o_ref[:, h * MLA_V:(h + 1) * MLA_V] = acc_s[h] / l_s[h][:, :1]


def _mla_attn(q, k, v, bsz, seq, tq, tk):
    nq = seq // tq
    nk = seq // tk
    t = bsz * seq
    return pl.pallas_call(
        functools.partial(_mla_attn_kernel, nk),
        grid=(bsz, nq, nk),
        in_specs=[pl.BlockSpec((MLA_HEADS, tq, MLA_QK_PAD), lambda b, i, j: (0, b * nq + i, 0)),
                  pl.BlockSpec((MLA_HEADS, tk, MLA_QK_PAD), lambda b, i, j: (0, b * nk + j, 0)),
                  pl.BlockSpec((MLA_HEADS, tk, MLA_V), lambda b, i, j: (0, b * nk + j, 0))],
        out_specs=pl.BlockSpec((tq, MLA_W), lambda b, i, j: (b * nq + i, 0)),
        out_shape=jax.ShapeDtypeStruct((t, MLA_W), F32),
        scratch_shapes=[pltpu.VMEM((MLA_HEADS, tq, 128), F32), pltpu.VMEM((MLA_HEADS, tq, 128), F32),
                        pltpu.VMEM((MLA_HEADS, tq, MLA_V), F32)],
        compiler_params=_params("parallel", "parallel", "arbitrary"),
        name="mla_attn",
    )(q, k, v)


def _s5_disc_kernel(lre_ref, lim_ref, ldt_ref, bre_ref, bim_ref, are_ref, aim_ref, bbre_ref, bbim_ref):
    lam_re = lre_ref[...]
    lam_im = lim_ref[...]
    dt = jnp.exp(ldt_ref[...])
    mag = jnp.exp(lam_re * dt)
    ab_re = mag * jnp.cos(lam_im * dt)
    ab_im = mag * jnp.sin(lam_im * dt)
    den = lam_re * lam_re + lam_im * lam_im
    nr = ab_re - 1.0
    ni = ab_im
    coef_re = (nr * lam_re + ni * lam_im) / den
    coef_im = (ni * lam_re - nr * lam_im) / den
    are_ref[...] = ab_re
    aim_ref[...] = ab_im
    b_re = bre_ref[...]
    b_im = bim_ref[...]
    bbre_ref[...] = coef_re * b_re - coef_im * b_im
    bbim_ref[...] = coef_re * b_im + coef_im * b_re


def _s5_discretise(lam_re, lam_im, log_dt, b_re, b_im):
    shape = (2, S5_GROUPS, S5_STATE, S5_GROUP)
    flat = lambda a: jnp.broadcast_to(a, shape).reshape(-1, 128)
    out = jax.ShapeDtypeStruct((math.prod(shape) // 128, 128), F32)
    a_re, a_im, bb_re, bb_im = pl.pallas_call(_s5_disc_kernel, out_shape=[out] * 4, name="s5_disc")(
        flat(lam_re[..., None]), flat(lam_im[..., None]), flat(log_dt[..., None, None]), flat(b_re), flat(b_im))
    g2 = 2 * S5_GROUPS
    mat = lambda a: a.reshape(g2, S5_STATE, S5_GROUP)
    return mat(a_re)[..., 0], mat(a_im)[..., 0], mat(bb_re), mat(bb_im)


def _block_diag(blocks):
    g, m, n = blocks.shape
    eye = jnp.eye(g, dtype=blocks.dtype)
    return (eye[:, None, :, None] * blocks[:, :, None, :]).reshape(g * m, g * n)


def _s5_scan_kernel(tb, uf_ref, ub_ref, a_ref, bin_ref, cout_ref, yf_ref, yb_ref, xf_s, xb_s, carry_s):
    i = pl.program_id(0)
    n = S5_N
    rows = S5_BATCH

    @pl.when(i == 0)
    def _():
        carry_s[...] = jnp.zeros_like(carry_s)

    xf_s[...] = _mm_f32(uf_ref[...], bin_ref[0])
    xb_s[...] = _mm_f32(ub_ref[...], bin_ref[1])
    af_re = jnp.broadcast_to(a_ref[0:1, :], (rows, n))
    af_im = jnp.broadcast_to(a_ref[1:2, :], (rows, n))
    ab_re = jnp.broadcast_to(a_ref[2:3, :], (rows, n))
    ab_im = jnp.broadcast_to(a_ref[3:4, :], (rows, n))

    def step(t, carry):
        f_re, f_im, b_re, b_im = carry
        sf = pl.ds(pl.multiple_of(t * rows, rows), rows)
        sb = pl.ds(pl.multiple_of((tb - 1 - t) * rows, rows), rows)
        nf_re = af_re * f_re - af_im * f_im + xf_s[sf, 0:n]
        nf_im = af_re * f_im + af_im * f_re + xf_s[sf, n:2 * n]
        nb_re = ab_re * b_re - ab_im * b_im + xb_s[sb, 0:n]
        nb_im = ab_re * b_im + ab_im * b_re + xb_s[sb, n:2 * n]
        xf_s[sf, 0:n] = nf_re
        xf_s[sf, n:2 * n] = nf_im
        xb_s[sb, 0:n] = nb_re
        xb_s[sb, n:2 * n] = nb_im
        return nf_re, nf_im, nb_re, nb_im

    init = (carry_s[0], carry_s[1], carry_s[2], carry_s[3])
    f_re, f_im, b_re, b_im = lax.fori_loop(0, tb, step, init)
    carry_s[0] = f_re
    carry_s[1] = f_im
    carry_s[2] = b_re
    carry_s[3] = b_im
    yf_ref[...] = _mm_f32(xf_s[...], cout_ref[0])
    yb_ref[...] = _mm_f32(xb_s[...], cout_ref[1])


def _s5_scan(u_tm, a_vec, b_in, c_out, seq, tb):
    nblk = seq // tb
    rows = tb * S5_BATCH
    fwd = pl.BlockSpec((rows, S5_W), lambda i: (i, 0))
    bwd = pl.BlockSpec((rows, S5_W), lambda i: (nblk - 1 - i, 0))
    shape = jax.ShapeDtypeStruct((seq * S5_BATCH, S5_W), F32)
    return pl.pallas_call(
        functools.partial(_s5_scan_kernel, tb),
        grid=(nblk,),
        in_specs=[fwd, bwd, _full(a_vec), _full(b_in), _full(c_out)],
        out_specs=[fwd, bwd],
        out_shape=[shape, shape],
        scratch_shapes=[pltpu.VMEM((rows, 2 * S5_N), F32), pltpu.VMEM((rows, 2 * S5_N), F32),
                        pltpu.VMEM((4, S5_BATCH, S5_N), F32)],
        compiler_params=_params("arbitrary"),
        name="s5_scan",
    )(u_tm, u_tm, a_vec, b_in, c_out)


def _s5_post_kernel(yf_ref, yb_ref, u_ref, d_ref, w_ref, b_ref, o_ref):
    y = yf_ref[...] + yb_ref[...] + d_ref[...] * u_ref[...]
    y = jax.nn.gelu(y, approximate=True)
    o_ref[...] = y * _sigmoid(_mm(y, w_ref[...]) + b_ref[...])


def _s5_post(yf, yb, u_tm, d_skip, glu_w, glu_b, tm):
    t = u_tm.shape[0]
    row = pl.BlockSpec((tm, S5_W), lambda i: (i, 0))
    return pl.pallas_call(
        _s5_post_kernel,
        grid=(t // tm,),
        in_specs=[row, row, row, _full(d_skip), _full(glu_w), _full(glu_b)],
        out_specs=row,
        out_shape=jax.ShapeDtypeStruct((t, S5_W), F32),
        compiler_params=_params("parallel"),
        name="s5_post",
    )(yf, yb, u_tm, d_skip, glu_w, glu_b)


def _merge_kernel(x_ref, of_ref, ob_ref, bonus_ref, rgate_ref, mla_ref, s5_ref, g_ref,
                  gng_ref, gnb_ref, wrw_ref, wmla_ref, ws5_ref, wout_ref, lng_ref, lnb_ref, o_ref):
    w = RW_W
    hr = lax.broadcasted_iota(jnp.int32, (w, w), 0) // RW_HEAD
    hc = lax.broadcasted_iota(jnp.int32, (w, w), 1) // RW_HEAD
    head_mean = (hr == hc).astype(F32) * (1.0 / RW_HEAD)
    o = of_ref[...] + ob_ref[...]
    mean = _mm_f32(o, head_mean)
    oc = o - mean
    var = _mm_f32(oc * oc, head_mean)
    o = oc * lax.rsqrt(var + RW_GN_EPS) * gng_ref[...] + gnb_ref[...]
    y_rw = _mm((o + bonus_ref[...]) * rgate_ref[...], wrw_ref[...])
    y_mla = _mm(mla_ref[...], wmla_ref[...])
    y_s5 = _mm(s5_ref[...], ws5_ref[...])
    d = D_MODEL
    merged = g_ref[:, 0:d] * y_rw + g_ref[:, d:2 * d] * y_mla + g_ref[:, 2 * d:3 * d] * y_s5
    o_ref[...] = _layer_norm(DN_ALPHA * x_ref[...] + _mm(merged, wout_ref[...]), lng_ref[...], lnb_ref[...])


def _merge(x, o_f, o_b, bonus, rgate, o_mla, y_s5, gates, gn_g, gn_b, w_rw, w_mla, w_s5, w_out, ln_g, ln_b, tm):
    t = x.shape[0]
    row = lambda n: pl.BlockSpec((tm, n), lambda i: (i, 0))
    weights = [gn_g, gn_b, w_rw, w_mla, w_s5, w_out, ln_g, ln_b]
    return pl.pallas_call(
        _merge_kernel,
        grid=(t // tm,),
        in_specs=[row(D_MODEL), row(RW_W), row(RW_W), row(RW_W), row(RW_W), row(MLA_W), row(S5_W),
                  row(N_BRANCH * D_MODEL)] + [_full(wt) for wt in weights],
        out_specs=row(D_MODEL),
        out_shape=jax.ShapeDtypeStruct((t, D_MODEL), F32),
        compiler_params=_params("parallel"),
        name="merge_ln1",
    )(x, o_f, o_b, bonus, rgate, o_mla, y_s5, gates, *weights)


def _mlp_kernel(nf, x_ref, w1_ref, w2_ref, lng_ref, lnb_ref, o_ref, acc_s):
    j = pl.program_id(1)

    @pl.when(j == 0)
    def _():
        acc_s[...] = jnp.zeros_like(acc_s)

    h = jnp.maximum(_mm(x_ref[...], w1_ref[...]), 0.0)
    acc_s[...] += _mm(h * h, w2_ref[...])

    @pl.when(j == nf - 1)
    def _():
        o_ref[...] = _layer_norm(DN_ALPHA * x_ref[...] + acc_s[...], lng_ref[...], lnb_ref[...])


def _mlp(x, w1, w2, ln_g, ln_b, tm, tf):
    t = x.shape[0]
    nf = D_FF // tf
    return pl.pallas_call(
        functools.partial(_mlp_kernel, nf),
        grid=(t // tm, nf),
        in_specs=[pl.BlockSpec((tm, D_MODEL), lambda i, j: (i, 0)),
                  pl.BlockSpec((D_MODEL, tf), lambda i, j: (0, j)),
                  pl.BlockSpec((tf, D_MODEL), lambda i, j: (j, 0)),
                  _full(ln_g), _full(ln_b)],
        out_specs=pl.BlockSpec((tm, D_MODEL), lambda i, j: (i, 0)),
        out_shape=jax.ShapeDtypeStruct((t, D_MODEL), F32),
        scratch_shapes=[pltpu.VMEM((tm, D_MODEL), F32)],
        compiler_params=_params("parallel", "arbitrary"),
        name="mlp_ln2",
    )(x, w1, w2, ln_g, ln_b)


def _tile(n, pref):
    t = min(n, pref)
    assert n % t == 0, (n, pref)
    return t


def _prep_layer(w_in, rw_mu, rw_w0, rw_w2, rw_a0, rw_a2, rw_g2, rw_k_k, rw_k_a, rw_r_k, rw_gn_g, rw_gn_b, rw_proj,
                mla_q_norm, mla_w_uq, mla_kv_norm, mla_w_ukv, mla_proj,
                s5_lam_re, s5_lam_im, s5_log_dt, s5_b_re, s5_b_im, s5_c_re, s5_c_im, s5_d, s5_glu_w, s5_glu_b,
                s5_proj, w_out, ln1_g, ln1_b, mlp_w1, mlp_w2, ln2_g, ln2_b):
    row = lambda a: a.reshape(1, -1)
    p = {}
    p["w_rw"] = w_in[:, :OFF_MLA].astype(BF16)
    w_mla = w_in[:, OFF_MLA:OFF_S5]
    zeros = lambda n: jnp.zeros((D_MODEL, n), F32)
    p["w_mla"] = jnp.concatenate([w_mla[:, :MLA_Q_LORA + MLA_KV_LORA], zeros(MLA_NOPE),
                                  w_mla[:, MLA_Q_LORA + MLA_KV_LORA:],
                                  zeros(MLA_QK_PAD - MLA_NOPE - MLA_ROPE)], axis=1).astype(BF16)
    p["w_s5"] = w_in[:, OFF_S5:OFF_GATE].astype(BF16)
    p["w_gate"] = w_in[:, OFF_GATE:].astype(BF16)
    p["rw"] = [row(rw_mu)]
    p["rw_dir"] = [(row(rw_w0[d]), rw_w2[d], row(rw_a0[d]), rw_a2[d]) for d in range(2)]
    p["rw_shared"] = [rw_g2, row(rw_k_k), row(rw_k_a), row(rw_r_k)]
    p["rw_gn"] = [row(rw_gn_g), row(rw_gn_b)]
    p["rw_proj"] = rw_proj.astype(BF16)
    uq = mla_w_uq.reshape(MLA_Q_LORA, MLA_HEADS, MLA_NOPE + MLA_ROPE)
    uq = jnp.pad(uq, ((0, 0), (0, 0), (0, MLA_QK_PAD - MLA_NOPE - MLA_ROPE)))
    p["w_uq"] = uq.transpose(1, 0, 2).astype(BF16)
    ukv = mla_w_ukv.reshape(MLA_KV_LORA, MLA_HEADS, MLA_NOPE + MLA_V)
    uk = jnp.pad(ukv[:, :, :MLA_NOPE], ((0, 0), (0, 0), (0, MLA_QK_PAD - MLA_NOPE)))
    p["w_uk"] = uk.transpose(1, 0, 2).astype(BF16)
    p["w_uv"] = ukv[:, :, MLA_NOPE:].transpose(1, 0, 2).astype(BF16)
    p["q_norm"] = row(mla_q_norm)
    p["kv_norm"] = row(mla_kv_norm)
    p["mla_proj"] = mla_proj.astype(BF16)
    a_re, a_im, bb_re, bb_im = _s5_discretise(s5_lam_re, s5_lam_im, s5_log_dt, s5_b_re, s5_b_im)
    p["s5_a"] = jnp.stack([a_re[:S5_GROUPS].reshape(-1), a_im[:S5_GROUPS].reshape(-1),
                           a_re[S5_GROUPS:].reshape(-1), a_im[S5_GROUPS:].reshape(-1)])
    b_in, c_out = [], []
    for d in range(2):
        sl = slice(d * S5_GROUPS, (d + 1) * S5_GROUPS)
        b_in.append(jnp.concatenate([_block_diag(bb_re[sl].transpose(0, 2, 1)),
                                     _block_diag(bb_im[sl].transpose(0, 2, 1))], axis=1))
        c_out.append(jnp.concatenate([_block_diag(s5_c_re[d].transpose(0, 2, 1)),
                                      -_block_diag(s5_c_im[d].transpose(0, 2, 1))], axis=0))
    p["s5_b_in"] = jnp.stack(b_in)
    p["s5_c_out"] = jnp.stack(c_out)
    p["s5_post"] = [row(s5_d), s5_glu_w.astype(BF16), row(s5_glu_b)]
    p["s5_proj"] = s5_proj.astype(BF16)
    p["w_out"] = w_out.astype(BF16)
    p["ln1"] = [row(ln1_g), row(ln1_b)]
    p["w1"] = mlp_w1.astype(BF16)
    p["w2"] = mlp_w2.astype(BF16)
    p["ln2"] = [row(ln2_g), row(ln2_b)]
    return p


def _layer(x, p, bsz, seq, rope):
    t = bsz * seq
    tm = _tile(t, 256)
    z_rw, z_mla, z_s5, gates = _inproj(x, p["w_rw"], p["w_mla"], p["w_s5"], p["w_gate"], tm)

    tb = _tile(seq, 512)
    z3 = z_rw.reshape(bsz, seq, RW_IN)
    o_f, bonus, rgate = _rwkv_scan(z3, *p["rw"], *p["rw_dir"][0], *p["rw_shared"], reverse=False, tb=tb)
    (o_b,) = _rwkv_scan(z3, *p["rw"], *p["rw_dir"][1], *p["rw_shared"], reverse=True, tb=tb)
    flat = lambda a: a.reshape(t, RW_W)

    q, k, v = _mla_qkv(z_mla, rope[0], rope[1], p["q_norm"], p["w_uq"], p["kv_norm"], p["w_uk"], p["w_uv"],
                       seq, _tile(seq, 512))
    o_mla = _mla_attn(q, k, v, bsz, seq, _tile(seq, 512), _tile(seq, 512))

    u_tm = jnp.pad(z_s5.reshape(bsz, seq, S5_W).transpose(1, 0, 2), ((0, 0), (0, S5_BATCH - bsz), (0, 0)))
    u_tm = u_tm.reshape(seq * S5_BATCH, S5_W)
    y_f, y_b = _s5_scan(u_tm, p["s5_a"], p["s5_b_in"], p["s5_c_out"], seq, _tile(seq, 64))
    y_s5 = _s5_post(y_f, y_b, u_tm, *p["s5_post"], _tile(seq * S5_BATCH, 1024))
    y_s5 = y_s5.reshape(seq, S5_BATCH, S5_W)[:, :bsz].transpose(1, 0, 2).reshape(t, S5_W)

    x1 = _merge(x, flat(o_f), flat(o_b), flat(bonus), flat(rgate), o_mla, y_s5, gates,
                *p["rw_gn"], p["rw_proj"], p["mla_proj"], p["s5_proj"], p["w_out"], *p["ln1"], tm)
    return _mlp(x1, p["w1"], p["w2"], *p["ln2"], _tile(t, 1024), 512)


def _trunk(x, layers):
    bsz, seq, _ = x.shape
    assert bsz <= S5_BATCH and seq % RW_CHUNK == 0
    rope = _rope_tables(seq)
    h = x.reshape(bsz * seq, D_MODEL)
    for p in layers:
        h = _layer(h, p, bsz, seq, rope)
    return h.reshape(bsz, seq, D_MODEL)


def kernel(x_prompt, x_sample, w_in, rw_mu, rw_w0, rw_w2, rw_a0, rw_a2, rw_g2, rw_k_k, rw_k_a, rw_r_k, rw_gn_g, rw_gn_b, rw_proj, mla_q_norm, mla_w_uq, mla_kv_norm, mla_w_ukv, mla_proj, s5_lam_re, s5_lam_im, s5_log_dt, s5_b_re, s5_b_im, s5_c_re, s5_c_im, s5_d, s5_glu_w, s5_glu_b, s5_proj, w_out, ln1_g, ln1_b, mlp_w1, mlp_w2, ln2_g, ln2_b):
    weights = (w_in, rw_mu, rw_w0, rw_w2, rw_a0, rw_a2, rw_g2, rw_k_k, rw_k_a, rw_r_k, rw_gn_g, rw_gn_b, rw_proj,
               mla_q_norm, mla_w_uq, mla_kv_norm, mla_w_ukv, mla_proj,
               s5_lam_re, s5_lam_im, s5_log_dt, s5_b_re, s5_b_im, s5_c_re, s5_c_im, s5_d, s5_glu_w, s5_glu_b,
               s5_proj, w_out, ln1_g, ln1_b, mlp_w1, mlp_w2, ln2_g, ln2_b)
    layers = [_prep_layer(*[wt[l] for wt in weights]) for l in range(w_in.shape[0])]
    return _trunk(x_prompt, layers), _trunk(x_sample, layers)
```

```python
import functools
import math

import numpy as np
import jax
import jax.numpy as jnp
from jax import lax
from jax.experimental import pallas as pl
from jax.experimental.pallas import tpu as pltpu

F32 = jnp.float32
BF16 = jnp.bfloat16
HIGHEST = lax.Precision.HIGHEST

D_MODEL = 1024
DEPTH = 2
RW_HEADS = 4
RW_HEAD = 64
RW_W = RW_HEADS * RW_HEAD
RW_LORA_W = 32
RW_LORA_A = 32
RW_LORA_G = 64
RW_GN_EPS = 64e-5
RW_IN = 3 * RW_W + RW_LORA_W + RW_LORA_A + RW_LORA_G
MLA_HEADS = 8
MLA_NOPE = 64
MLA_ROPE = 32
MLA_V = 64
MLA_Q_LORA = 256
MLA_KV_LORA = 128
MLA_W = MLA_HEADS * MLA_V
MLA_QK_PAD = 128
MLA_IN_PAD = MLA_Q_LORA + MLA_KV_LORA + MLA_QK_PAD
MLA_SCALE = (MLA_NOPE + MLA_ROPE) ** -0.5
MLA_VT_ROWS = MLA_V + 16
LOG2_E = math.log2(math.e)
MLA_SHIFT_SLACK = 64.0
ROPE_THETA = 10000.0
RMS_EPS = 1e-6
S5_W = 256
S5_GROUP = 16
S5_GROUPS = S5_W // S5_GROUP
S5_STATE = 64
S5_N = S5_GROUPS * S5_STATE
S5_BATCH = 8
D_FF = 4 * D_MODEL
LN_EPS = 1e-5
N_BRANCH = 3
DN_ALPHA = (2 * DEPTH) ** 0.25
OFF_MLA = RW_IN
OFF_S5 = OFF_MLA + MLA_Q_LORA + MLA_KV_LORA + MLA_ROPE
OFF_GATE = OFF_S5 + S5_W
RW_CHUNK = 64
VMEM_LIMIT = 56 * 1024 * 1024


def _params(*sem):
    return pltpu.CompilerParams(dimension_semantics=sem, vmem_limit_bytes=VMEM_LIMIT)


def _mm(a, b):
    return jnp.dot(a.astype(BF16), b.astype(BF16), preferred_element_type=F32)


def _mm_nt(a, b):
    return lax.dot_general(a.astype(BF16), b.astype(BF16), (((1,), (1,)), ((), ())),
                           preferred_element_type=F32)


def _mm_tn(a, b):
    return lax.dot_general(a.astype(BF16), b.astype(BF16), (((0,), (0,)), ((), ())),
                           preferred_element_type=F32)


def _mm_f32(a, b):
    return jnp.dot(a, b, preferred_element_type=F32, precision=HIGHEST)


def _bf16_terms(x, n):
    terms = []
    for _ in range(n):
        t = x.astype(BF16)
        terms.append(t)
        x = x - t.astype(F32)
    return terms


def _mm_split(a, b, a_terms, b_terms):
    at = _bf16_terms(a, a_terms)
    bt = _bf16_terms(b, b_terms)
    out = None
    for i, x in enumerate(at):
        for j, y in enumerate(bt):
            if i + j < max(a_terms, b_terms):
                d = jnp.dot(x, y, preferred_element_type=F32)
                out = d if out is None else out + d
    return out


def _sigmoid(x):
    return 1.0 / (1.0 + jnp.exp(-x))


def _full(a):
    nd = a.ndim
    return pl.BlockSpec(a.shape, lambda *_: (0,) * nd)


def _layer_norm(x, g, b):
    mu = jnp.mean(x, -1, keepdims=True)
    xc = x - mu
    var = jnp.mean(xc * xc, -1, keepdims=True)
    return xc * lax.rsqrt(var + LN_EPS) * g + b


def _inproj_kernel(x_ref, wrw_ref, wmla_ref, ws5_ref, wg_ref, zrw_ref, zmla_ref, zs5_ref, g_ref):
    xb = x_ref[...].astype(BF16)
    zrw_ref[...] = jnp.dot(xb, wrw_ref[...], preferred_element_type=F32)
    zmla_ref[...] = jnp.dot(xb, wmla_ref[...], preferred_element_type=F32)
    zs5_ref[...] = jnp.dot(xb, ws5_ref[...], preferred_element_type=F32)
    g_ref[...] = _sigmoid(jnp.dot(xb, wg_ref[...], preferred_element_type=F32))


def _inproj(x, w_rw, w_mla, w_s5, w_g, tm):
    t = x.shape[0]
    row = lambda n: pl.BlockSpec((tm, n), lambda i: (i, 0))
    return pl.pallas_call(
        _inproj_kernel,
        grid=(t // tm,),
        in_specs=[row(D_MODEL), _full(w_rw), _full(w_mla), _full(w_s5), _full(w_g)],
        out_specs=[row(RW_IN), row(MLA_IN_PAD), row(S5_W), row(N_BRANCH * D_MODEL)],
        out_shape=[jax.ShapeDtypeStruct((t, RW_IN), F32), jax.ShapeDtypeStruct((t, MLA_IN_PAD), F32),
                   jax.ShapeDtypeStruct((t, S5_W), F32), jax.ShapeDtypeStruct((t, N_BRANCH * D_MODEL), F32)],
        compiler_params=_params("parallel"),
        name="inproj",
    )(x, w_rw, w_mla, w_s5, w_g)


def _rwkv_kernel(nblk, tb,
                 zf_ref, zfp_ref, zfn_ref, zb_ref, zbp_ref, zbn_ref,
                 mu_ref, w0_ref, w2_ref, a0_ref, a2_ref, g2_ref, kk_ref, ka_ref, rk_ref,
                 of_ref, bonus_ref, gate_ref, ob_ref, *scratch):
    step = pl.program_id(1)
    c = RW_CHUNK
    w = RW_W
    hr = lax.broadcasted_iota(jnp.int32, (w, w), 0) // RW_HEAD
    hc = lax.broadcasted_iota(jnp.int32, (w, w), 1) // RW_HEAD
    same_head = hr == hc
    head_ones = same_head.astype(F32)

    def prepare(d, z_ref, zp_ref, zn_ref, blk, r_s, v_s, kn_s, lw_s, b_s, kd_s):
        z = z_ref[...]
        prev_row = jnp.where(blk == 0, 0.0, zp_ref[7:8, :])
        next_row = jnp.where(blk == nblk - 1, 0.0, zn_ref[0:1, :])
        rows = lax.broadcasted_iota(jnp.int32, z.shape, 0)
        z_prev = jnp.where(rows == 0, prev_row, pltpu.roll(z, 1, axis=0))
        z_next = jnp.where(rows == tb - 1, next_row, pltpu.roll(z, tb - 1, axis=0))
        z = z + mu_ref[...] * (0.5 * (z_prev + z_next) - z)
        r = z[:, 0:w]
        k = z[:, w:2 * w]
        v = z[:, 2 * w:3 * w]
        xw = z[:, 3 * w:3 * w + RW_LORA_W]
        xa = z[:, 3 * w + RW_LORA_W:3 * w + RW_LORA_W + RW_LORA_A]
        xg = z[:, 3 * w + RW_LORA_W + RW_LORA_A:]
        kk = k * kk_ref[...]
        kk_ss = _mm_split(kk * kk, head_ones, 2, 1)
        kk = kk * lax.rsqrt(jnp.maximum(kk_ss, 1e-12))
        y = w0_ref[d] + _mm_split(jnp.tanh(xw), w2_ref[d], 2, 2)
        lw = -math.exp(-0.5) * _sigmoid(y)
        a = _sigmoid(a0_ref[d] + _mm(xa, a2_ref[d]))
        r_s[...] = r
        v_s[...] = v
        kn_s[...] = kk
        lw_s[...] = lw
        b_s[...] = kk * a
        kd_s[...] = k * (1.0 + (a - 1.0) * ka_ref[...])
        if d == 0:
            rk = _mm_split(r * k * rk_ref[...], head_ones, 2, 1)
            bonus_ref[...] = rk * v
            gate_ref[...] = _mm(_sigmoid(xg), g2_ref[...])

    scr_f, scr_b = scratch[0:6], scratch[6:12]
    state_f, state_b = scratch[12], scratch[13]
    prepare(0, zf_ref, zfp_ref, zfn_ref, step, *scr_f)
    prepare(1, zb_ref, zbp_ref, zbn_ref, nblk - 1 - step, *scr_b)

    @pl.when(step == 0)
    def _():
        state_f[...] = jnp.zeros_like(state_f)
        state_b[...] = jnp.zeros_like(state_b)

    ti = lax.broadcasted_iota(jnp.int32, (c, c), 0)
    si = lax.broadcasted_iota(jnp.int32, (c, c), 1)
    rr = lax.broadcasted_iota(jnp.int32, (w, w), 0)
    cc = lax.broadcasted_iota(jnp.int32, (w, w), 1)
    eye = (rr == cc).astype(F32)

    def stack(x):
        return jnp.where(same_head, jnp.concatenate([x] * RW_HEADS, axis=0), 0.0)

    nchunk = tb // c

    def chunk(reverse, cpos, r_s, v_s, kn_s, lw_s, b_s, kd_s, state, o_ref):
        cum_mat = ((si >= ti) if reverse else (si <= ti)).astype(F32)
        strict = (rr < cc) if reverse else (rr > cc)
        incl = (rr <= cc) if reverse else (rr >= cc)
        sl = pl.ds(pl.multiple_of(cpos * c, c), c)
        lwc = lw_s[sl, :]
        l_in = _mm_split(cum_mat, lwc, 1, 3)
        l_tot = jnp.sum(lwc, axis=0, keepdims=True)
        e_in = jnp.exp(l_in)
        e_neg = jnp.exp(-l_in)
        e_tot = jnp.exp(l_tot)
        kap = stack(kn_s[sl, :] * jnp.exp(l_in - lwc)).astype(BF16)
        bt = b_s[sl, :] * e_neg
        kt = kd_s[sl, :] * e_neg
        bts = stack(bt).astype(BF16)
        kts = stack(kt).astype(BF16)
        rts = stack(r_s[sl, :] * e_in).astype(BF16)
        vs = stack(v_s[sl, :]).astype(BF16)
        bhs = stack(bt * e_tot).astype(BF16)
        khs = stack(kt * e_tot).astype(BF16)
        s_bd = state[...]
        yield
        a_b = jnp.where(strict, _mm_nt(kap, bts), 0.0)
        a_k = jnp.where(strict, _mm_nt(kap, kts), 0.0)
        yield
        m_b = jnp.where(incl, _mm_nt(rts, bts), 0.0)
        m_k = jnp.where(incl, _mm_nt(rts, kts), 0.0)
        yield
        n = -a_b
        tinv = eye + n
        p = n
        for _ in range(int(math.log2(c)) - 1):
            p = _mm(p, p)
            yield
            tinv = tinv + _mm(tinv, p)
            yield
        pm = _mm(tinv, kap)
        akv = _mm(a_k, vs)
        yield
        qm = _mm(tinv, akv)
        yield
        u = -(_mm_nt(pm, s_bd) + qm)
        yield
        o_st = _mm_nt(rts, s_bd) + _mm(m_b, u) + _mm(m_k, vs)
        upd = _mm_tn(u, bhs) + _mm_tn(vs, khs)
        yield
        o_ref[sl, :] = o_st[0:c] + o_st[c:2 * c] + o_st[2 * c:3 * c] + o_st[3 * c:4 * c]
        state[...] = s_bd * e_tot + upd

    def both(ci, carry):
        fwd = chunk(False, ci, *scr_f, state_f, of_ref)
        bwd = chunk(True, nchunk - 1 - ci, *scr_b, state_b, ob_ref)
        for _ in zip(fwd, bwd):
            pass
        for _ in fwd:
            pass
        for _ in bwd:
            pass
        return carry

    lax.fori_loop(0, nchunk, both, 0)


def _rwkv_scan(z, mu, w0, w2, a0, a2, g2, k_k, k_a, r_k, tb):
    bsz, seq, _ = z.shape
    nblk = seq // tb
    z8 = z.reshape(bsz, seq // 8, 8, RW_IN)
    t8 = tb // 8

    def specs(pos):
        z_spec = pl.BlockSpec((None, tb, RW_IN), lambda b, i: (b, pos(i), 0))
        zp_spec = pl.BlockSpec((None, None, 8, RW_IN), lambda b, i: (b, jnp.maximum(pos(i) * t8 - 1, 0), 0, 0))
        zn_spec = pl.BlockSpec((None, None, 8, RW_IN),
                               lambda b, i: (b, jnp.minimum((pos(i) + 1) * t8, seq // 8 - 1), 0, 0))
        o_spec = pl.BlockSpec((None, tb, RW_W), lambda b, i: (b, pos(i), 0))
        return [z_spec, zp_spec, zn_spec], o_spec

    in_f, o_f = specs(lambda i: i)
    in_b, o_b = specs(lambda i: nblk - 1 - i)
    o_shape = jax.ShapeDtypeStruct((bsz, seq, RW_W), F32)
    weights = [mu, w0, w2, a0, a2, g2, k_k, k_a, r_k]
    return pl.pallas_call(
        functools.partial(_rwkv_kernel, nblk, tb),
        grid=(bsz, nblk),
        in_specs=in_f + in_b + [_full(wt) for wt in weights],
        out_specs=[o_f, o_f, o_f, o_b],
        out_shape=[o_shape] * 4,
        scratch_shapes=[pltpu.VMEM((tb, RW_W), F32)] * 12 + [pltpu.VMEM((RW_W, RW_W), F32)] * 2,
        compiler_params=_params("parallel", "arbitrary"),
        name="rwkv_scan",
    )(z, z8, z8, z, z8, z8, *weights)


def _rope_tables(seq):
    half = MLA_ROPE // 2
    inv = (ROPE_THETA ** (-np.arange(half, dtype=np.float32) / half)).astype(np.float32)
    ang = np.arange(seq, dtype=np.float32)[:, None] * inv[None, :]
    cos = np.cos(ang).astype(np.float32)
    sin = np.sin(ang).astype(np.float32)
    ct = np.zeros((seq, MLA_QK_PAD), np.float32)
    st = np.zeros((seq, MLA_QK_PAD), np.float32)
    ct[:, :MLA_NOPE] = 1.0
    ct[:, MLA_NOPE:MLA_NOPE + half] = cos
    ct[:, MLA_NOPE + half:MLA_NOPE + 2 * half] = cos
    st[:, MLA_NOPE:MLA_NOPE + half] = -sin
    st[:, MLA_NOPE + half:MLA_NOPE + 2 * half] = sin
    return jnp.asarray(ct), jnp.asarray(st)


def _rope(x, cos_t, sin_t):
    half = MLA_ROPE // 2
    lane = lax.broadcasted_iota(jnp.int32, x.shape, 1)
    swapped = jnp.where(lane < MLA_NOPE + half,
                        pltpu.roll(x, MLA_QK_PAD - half, axis=1), pltpu.roll(x, half, axis=1))
    return x * cos_t + swapped * sin_t


def _mla_qkv_kernel(z_ref, cos_ref, sin_ref, qn_ref, wuq_ref, kvn_ref, wuk_ref, wuv_ref, q_ref, k_ref, v_ref):
    z = z_ref[...]
    cos_t = cos_ref[...]
    sin_t = sin_ref[...]
    c_q = z[:, :MLA_Q_LORA]
    c_kv = z[:, MLA_Q_LORA:MLA_Q_LORA + MLA_KV_LORA]
    k_rope = _rope(z[:, MLA_Q_LORA + MLA_KV_LORA:], cos_t, sin_t)
    c_q = (c_q * lax.rsqrt(jnp.mean(c_q * c_q, -1, keepdims=True) + RMS_EPS) * qn_ref[...]).astype(BF16)
    c_kv = (c_kv * lax.rsqrt(jnp.mean(c_kv * c_kv, -1, keepdims=True) + RMS_EPS) * kvn_ref[...]).astype(BF16)
    ones_row = (lax.broadcasted_iota(jnp.int32, (MLA_VT_ROWS - MLA_V, z.shape[0]), 0) == 0).astype(BF16)
    for h in range(MLA_HEADS):
        q = jnp.dot(c_q, wuq_ref[h], preferred_element_type=F32)
        q_ref[h] = (_rope(q, cos_t, sin_t) * (MLA_SCALE * LOG2_E)).astype(BF16)
        kh = jnp.dot(c_kv, wuk_ref[h], preferred_element_type=F32)
        k_ref[h] = (kh + k_rope).astype(BF16)
        v_t = lax.dot_general(wuv_ref[h], c_kv, (((1,), (1,)), ((), ())), preferred_element_type=F32)
        v_ref[h, 0:MLA_V, :] = v_t.astype(BF16)
        v_ref[h, MLA_V:MLA_VT_ROWS, :] = ones_row


def _mla_qkv(z, cos_t, sin_t, q_norm, w_uq, kv_norm, w_uk, w_uv, seq, tm):
    t = z.shape[0]
    nseq = seq // tm
    tab = pl.BlockSpec((tm, MLA_QK_PAD), lambda i: (i % nseq, 0))
    head = lambda n: pl.BlockSpec((MLA_HEADS, tm, n), lambda i: (0, i, 0))
    return pl.pallas_call(
        _mla_qkv_kernel,
        grid=(t // tm,),
        in_specs=[pl.BlockSpec((tm, MLA_IN_PAD), lambda i: (i, 0)), tab, tab,
                  _full(q_norm), _full(w_uq), _full(kv_norm), _full(w_uk), _full(w_uv)],
        out_specs=[head(MLA_QK_PAD), head(MLA_QK_PAD),
                   pl.BlockSpec((MLA_HEADS, MLA_VT_ROWS, tm), lambda i: (0, 0, i))],
        out_shape=[jax.ShapeDtypeStruct((MLA_HEADS, t, MLA_QK_PAD), BF16),
                   jax.ShapeDtypeStruct((MLA_HEADS, t, MLA_QK_PAD), BF16),
                   jax.ShapeDtypeStruct((MLA_HEADS, MLA_VT_ROWS, t), BF16)],
        compiler_params=_params("parallel"),
        name="mla_qkv",
    )(z, cos_t, sin_t, q_norm, w_uq, kv_norm, w_uk, w_uv)


def _mla_attn_kernel(nk, q_ref, k_ref, v_ref, o_ref, m_s, acc_s, bm_s, pv_s):
    j = pl.program_id(2)

    @pl.when(j == 0)
    def _():
        m_s[...] = jnp.full(m_s.shape, -jnp.inf, F32)
        acc_s[...] = jnp.zeros_like(acc_s)

    def scores(h):
        return lax.dot_general(k_ref[h], q_ref[h], (((1,), (1,)), ((), ())), preferred_element_type=F32)

    s_next = scores(0)
    for h in range(MLA_HEADS):
        s = s_next
        if h + 1 < MLA_HEADS:
            s_next = scores(h + 1)
        bm_s[h:h + 1, :] = jnp.max(s, axis=0, keepdims=True)
        p = jnp.exp2(s - m_s[h:h + 1, :]).astype(BF16)
        pv_s[h] = jnp.dot(v_ref[h], p, preferred_element_type=F32)

    stale_shift_ok = jnp.max(bm_s[...] - m_s[...]) < MLA_SHIFT_SLACK

    @pl.when(stale_shift_ok)
    def _():
        for h in range(MLA_HEADS):
            m_prev = m_s[h:h + 1, :]
            m_new = jnp.maximum(m_prev, bm_s[h:h + 1, :])
            acc_s[h] = (acc_s[h] + pv_s[h]) * jnp.exp2(m_prev - m_new)
            m_s[h:h + 1, :] = m_new

    @pl.when(jnp.logical_not(stale_shift_ok))
    def _():
        for h in range(MLA_HEADS):
            s = scores(h)
            m_prev = m_s[h:h + 1, :]
            m_new = jnp.maximum(m_prev, bm_s[h:h + 1, :])
            p = jnp.exp2(s - m_new).astype(BF16)
            acc_s[h] = jnp.exp2(m_prev - m_new) * acc_s[h] + jnp.dot(v_ref[h], p, preferred_element_type=F32)
            m_s[h:h + 1, :] = m_new

    @pl.when(j == nk - 1)
    def _():
        heads = []
        for h in range(MLA_HEADS):
            acc = acc_s[h]
            heads.append(acc[0:MLA_V] / acc[MLA_V:MLA_V + 1])
        o_ref[...] = jnp.concatenate(heads, axis=0).T


def _mla_attn(q, k, v, bsz, seq, tq, tk):
    nq = seq // tq
    nk = seq // tk
    t = bsz * seq
    return pl.pallas_call(
        functools.partial(_mla_attn_kernel, nk),
        grid=(bsz, nq, nk),
        in_specs=[pl.BlockSpec((MLA_HEADS, tq, MLA_QK_PAD), lambda b, i, j: (0, b * nq + i, 0)),
                  pl.BlockSpec((MLA_HEADS, tk, MLA_QK_PAD), lambda b, i, j: (0, b * nk + j, 0)),
                  pl.BlockSpec((MLA_HEADS, MLA_VT_ROWS, tk), lambda b, i, j: (0, 0, b * nk + j))],
        out_specs=pl.BlockSpec((tq, MLA_W), lambda b, i, j: (b * nq + i, 0)),
        out_shape=jax.ShapeDtypeStruct((t, MLA_W), F32),
        scratch_shapes=[pltpu.VMEM((MLA_HEADS, tq), F32), pltpu.VMEM((MLA_HEADS, MLA_VT_ROWS, tq), F32),
                        pltpu.VMEM((MLA_HEADS, tq), F32), pltpu.VMEM((MLA_HEADS, MLA_VT_ROWS, tq), F32)],
        compiler_params=_params("parallel", "parallel", "arbitrary"),
        name="mla_attn",
    )(q, k, v)


def _s5_disc_kernel(lre_ref, lim_ref, ldt_ref, bre_ref, bim_ref, are_ref, aim_ref, bbre_ref, bbim_ref):
    lam_re = lre_ref[...]
    lam_im = lim_ref[...]
    dt = jnp.exp(ldt_ref[...])
    mag = jnp.exp(lam_re * dt)
    ab_re = mag * jnp.cos(lam_im * dt)
    ab_im = mag * jnp.sin(lam_im * dt)
    den = lam_re * lam_re + lam_im * lam_im
    nr = ab_re - 1.0
    ni = ab_im
    coef_re = (nr * lam_re + ni * lam_im) / den
    coef_im = (ni * lam_re - nr * lam_im) / den
    are_ref[...] = ab_re
    aim_ref[...] = ab_im
    b_re = bre_ref[...]
    b_im = bim_ref[...]
    bbre_ref[...] = coef_re * b_re - coef_im * b_im
    bbim_ref[...] = coef_re * b_im + coef_im * b_re


def _s5_discretise(lam_re, lam_im, log_dt, b_re, b_im):
    shape = (2, S5_GROUPS, S5_STATE, S5_GROUP)
    flat = lambda a: jnp.broadcast_to(a, shape).reshape(-1, 128)
    out = jax.ShapeDtypeStruct((math.prod(shape) // 128, 128), F32)
    a_re, a_im, bb_re, bb_im = pl.pallas_call(_s5_disc_kernel, out_shape=[out] * 4, name="s5_disc")(
        flat(lam_re[..., None]), flat(lam_im[..., None]), flat(log_dt[..., None, None]), flat(b_re), flat(b_im))
    g2 = 2 * S5_GROUPS
    mat = lambda a: a.reshape(g2, S5_STATE, S5_GROUP)
    return mat(a_re)[..., 0], mat(a_im)[..., 0], mat(bb_re), mat(bb_im)


def _block_diag(blocks):
    g, m, n = blocks.shape
    eye = jnp.eye(g, dtype=blocks.dtype)
    return (eye[:, None, :, None] * blocks[:, :, None, :]).reshape(g * m, g * n)


def _s5_scan_kernel(tb, uf_ref, ub_ref, a_ref, bin_ref, cout_ref, yf_ref, yb_ref, xf_s, xb_s, carry_s):
    i = pl.program_id(0)
    n = S5_N
    rows = S5_BATCH

    @pl.when(i == 0)
    def _():
        carry_s[...] = jnp.zeros_like(carry_s)

    xf_s[...] = _mm(uf_ref[...], bin_ref[0])
    xb_s[...] = _mm(ub_ref[...], bin_ref[1])
    af_re = jnp.broadcast_to(a_ref[0:1, :], (rows, n))
    af_im = jnp.broadcast_to(a_ref[1:2, :], (rows, n))
    ab_re = jnp.broadcast_to(a_ref[2:3, :], (rows, n))
    ab_im = jnp.broadcast_to(a_ref[3:4, :], (rows, n))

    def step(t, carry):
        f_re, f_im, b_re, b_im = carry
        sf = pl.ds(pl.multiple_of(t * rows, rows), rows)
        sb = pl.ds(pl.multiple_of((tb - 1 - t) * rows, rows), rows)
        nf_re = af_re * f_re - af_im * f_im + xf_s[sf, 0:n]
        nf_im = af_re * f_im + af_im * f_re + xf_s[sf, n:2 * n]
        nb_re = ab_re * b_re - ab_im * b_im + xb_s[sb, 0:n]
        nb_im = ab_re * b_im + ab_im * b_re + xb_s[sb, n:2 * n]
        xf_s[sf, 0:n] = nf_re
        xf_s[sf, n:2 * n] = nf_im
        xb_s[sb, 0:n] = nb_re
        xb_s[sb, n:2 * n] = nb_im
        return nf_re, nf_im, nb_re, nb_im

    init = (carry_s[0], carry_s[1], carry_s[2], carry_s[3])
    f_re, f_im, b_re, b_im = lax.fori_loop(0, tb, step, init)
    carry_s[0] = f_re
    carry_s[1] = f_im
    carry_s[2] = b_re
    carry_s[3] = b_im
    yf_ref[...] = _mm(xf_s[...], cout_ref[0])
    yb_ref[...] = _mm(xb_s[...], cout_ref[1])


def _s5_scan(u_tm, a_vec, b_in, c_out, seq, tb):
    nblk = seq // tb
    rows = tb * S5_BATCH
    fwd = pl.BlockSpec((rows, S5_W), lambda i: (i, 0))
    bwd = pl.BlockSpec((rows, S5_W), lambda i: (nblk - 1 - i, 0))
    shape = jax.ShapeDtypeStruct((seq * S5_BATCH, S5_W), F32)
    return pl.pallas_call(
        functools.partial(_s5_scan_kernel, tb),
        grid=(nblk,),
        in_specs=[fwd, bwd, _full(a_vec), _full(b_in), _full(c_out)],
        out_specs=[fwd, bwd],
        out_shape=[shape, shape],
        scratch_shapes=[pltpu.VMEM((rows, 2 * S5_N), F32), pltpu.VMEM((rows, 2 * S5_N), F32),
                        pltpu.VMEM((4, S5_BATCH, S5_N), F32)],
        compiler_params=_params("arbitrary"),
        name="s5_scan",
    )(u_tm, u_tm, a_vec, b_in, c_out)


def _s5_post_kernel(yf_ref, yb_ref, u_ref, d_ref, w_ref, b_ref, o_ref):
    y = yf_ref[...] + yb_ref[...] + d_ref[...] * u_ref[...]
    y = jax.nn.gelu(y, approximate=True)
    o_ref[...] = y * _sigmoid(_mm(y, w_ref[...]) + b_ref[...])


def _s5_post(yf, yb, u_tm, d_skip, glu_w, glu_b, tm):
    t = u_tm.shape[0]
    row = pl.BlockSpec((tm, S5_W), lambda i: (i, 0))
    return pl.pallas_call(
        _s5_post_kernel,
        grid=(t // tm,),
        in_specs=[row, row, row, _full(d_skip), _full(glu_w), _full(glu_b)],
        out_specs=row,
        out_shape=jax.ShapeDtypeStruct((t, S5_W), F32),
        compiler_params=_params("parallel"),
        name="s5_post",
    )(yf, yb, u_tm, d_skip, glu_w, glu_b)


def _merge_kernel(x_ref, of_ref, ob_ref, bonus_ref, rgate_ref, mla_ref, s5_ref, g_ref,
                  gng_ref, gnb_ref, wrw_ref, wmla_ref, ws5_ref, wout_ref, lng_ref, lnb_ref, o_ref):
    w = RW_W
    hr = lax.broadcasted_iota(jnp.int32, (w, w), 0) // RW_HEAD
    hc = lax.broadcasted_iota(jnp.int32, (w, w), 1) // RW_HEAD
    head_mean = (hr == hc).astype(F32) * (1.0 / RW_HEAD)
    o = of_ref[...] + ob_ref[...]
    mean = _mm_split(o, head_mean, 2, 1)
    oc = o - mean
    var = _mm_split(oc * oc, head_mean, 2, 1)
    o = oc * lax.rsqrt(var + RW_GN_EPS) * gng_ref[...] + gnb_ref[...]
    y_rw = _mm((o + bonus_ref[...]) * rgate_ref[...], wrw_ref[...])
    y_mla = _mm(mla_ref[...], wmla_ref[...])
    y_s5 = _mm(s5_ref[...], ws5_ref[...])
    d = D_MODEL
    merged = g_ref[:, 0:d] * y_rw + g_ref[:, d:2 * d] * y_mla + g_ref[:, 2 * d:3 * d] * y_s5
    o_ref[...] = _layer_norm(DN_ALPHA * x_ref[...] + _mm(merged, wout_ref[...]), lng_ref[...], lnb_ref[...])


def _merge(x, o_f, o_b, bonus, rgate, o_mla, y_s5, gates, gn_g, gn_b, w_rw, w_mla, w_s5, w_out, ln_g, ln_b, tm):
    t = x.shape[0]
    row = lambda n: pl.BlockSpec((tm, n), lambda i: (i, 0))
    weights = [gn_g, gn_b, w_rw, w_mla, w_s5, w_out, ln_g, ln_b]
    return pl.pallas_call(
        _merge_kernel,
        grid=(t // tm,),
        in_specs=[row(D_MODEL), row(RW_W), row(RW_W), row(RW_W), row(RW_W), row(MLA_W), row(S5_W),
                  row(N_BRANCH * D_MODEL)] + [_full(wt) for wt in weights],
        out_specs=row(D_MODEL),
        out_shape=jax.ShapeDtypeStruct((t, D_MODEL), F32),
        compiler_params=_params("parallel"),
        name="merge_ln1",
    )(x, o_f, o_b, bonus, rgate, o_mla, y_s5, gates, *weights)


def _mlp_kernel(nf, x_ref, w1_ref, w2_ref, lng_ref, lnb_ref, o_ref, acc_s):
    j = pl.program_id(1)

    @pl.when(j == 0)
    def _():
        acc_s[...] = jnp.zeros_like(acc_s)

    h = jnp.maximum(_mm(x_ref[...], w1_ref[...]), 0.0)
    acc_s[...] += _mm(h * h, w2_ref[...])

    @pl.when(j == nf - 1)
    def _():
        o_ref[...] = _layer_norm(DN_ALPHA * x_ref[...] + acc_s[...], lng_ref[...], lnb_ref[...])


def _mlp(x, w1, w2, ln_g, ln_b, tm, tf):
    t = x.shape[0]
    nf = D_FF // tf
    return pl.pallas_call(
        functools.partial(_mlp_kernel, nf),
        grid=(t // tm, nf),
        in_specs=[pl.BlockSpec((tm, D_MODEL), lambda i, j: (i, 0)),
                  pl.BlockSpec((D_MODEL, tf), lambda i, j: (0, j)),
                  pl.BlockSpec((tf, D_MODEL), lambda i, j: (j, 0)),
                  _full(ln_g), _full(ln_b)],
        out_specs=pl.BlockSpec((tm, D_MODEL), lambda i, j: (i, 0)),
        out_shape=jax.ShapeDtypeStruct((t, D_MODEL), F32),
        scratch_shapes=[pltpu.VMEM((tm, D_MODEL), F32)],
        compiler_params=_params("parallel", "arbitrary"),
        name="mlp_ln2",
    )(x, w1, w2, ln_g, ln_b)


def _tile(n, pref):
    t = min(n, pref)
    assert n % t == 0, (n, pref)
    return t


def _prep_layer(w_in, rw_mu, rw_w0, rw_w2, rw_a0, rw_a2, rw_g2, rw_k_k, rw_k_a, rw_r_k, rw_gn_g, rw_gn_b, rw_proj,
                mla_q_norm, mla_w_uq, mla_kv_norm, mla_w_ukv, mla_proj,
                s5_lam_re, s5_lam_im, s5_log_dt, s5_b_re, s5_b_im, s5_c_re, s5_c_im, s5_d, s5_glu_w, s5_glu_b,
                s5_proj, w_out, ln1_g, ln1_b, mlp_w1, mlp_w2, ln2_g, ln2_b):
    row = lambda a: a.reshape(1, -1)
    p = {}
    p["w_rw"] = w_in[:, :OFF_MLA].astype(BF16)
    w_mla = w_in[:, OFF_MLA:OFF_S5]
    zeros = lambda n: jnp.zeros((D_MODEL, n), F32)
    p["w_mla"] = jnp.concatenate([w_mla[:, :MLA_Q_LORA + MLA_KV_LORA], zeros(MLA_NOPE),
                                  w_mla[:, MLA_Q_LORA + MLA_KV_LORA:],
                                  zeros(MLA_QK_PAD - MLA_NOPE - MLA_ROPE)], axis=1).astype(BF16)
    p["w_s5"] = w_in[:, OFF_S5:OFF_GATE].astype(BF16)
    p["w_gate"] = w_in[:, OFF_GATE:].astype(BF16)
    p["rw"] = [row(rw_mu)]
    p["rw_dir"] = [rw_w0[:, None, :], rw_w2, rw_a0[:, None, :], rw_a2]
    p["rw_shared"] = [rw_g2, row(rw_k_k), row(rw_k_a), row(rw_r_k)]
    p["rw_gn"] = [row(rw_gn_g), row(rw_gn_b)]
    p["rw_proj"] = rw_proj.astype(BF16)
    uq = mla_w_uq.reshape(MLA_Q_LORA, MLA_HEADS, MLA_NOPE + MLA_ROPE)
    uq = jnp.pad(uq, ((0, 0), (0, 0), (0, MLA_QK_PAD - MLA_NOPE - MLA_ROPE)))
    p["w_uq"] = uq.transpose(1, 0, 2).astype(BF16)
    ukv = mla_w_ukv.reshape(MLA_KV_LORA, MLA_HEADS, MLA_NOPE + MLA_V)
    uk = jnp.pad(ukv[:, :, :MLA_NOPE], ((0, 0), (0, 0), (0, MLA_QK_PAD - MLA_NOPE)))
    p["w_uk"] = uk.transpose(1, 0, 2).astype(BF16)
    p["w_uv"] = ukv[:, :, MLA_NOPE:].transpose(1, 2, 0).astype(BF16)
    p["q_norm"] = row(mla_q_norm)
    p["kv_norm"] = row(mla_kv_norm)
    p["mla_proj"] = mla_proj.astype(BF16)
    a_re, a_im, bb_re, bb_im = _s5_discretise(s5_lam_re, s5_lam_im, s5_log_dt, s5_b_re, s5_b_im)
    p["s5_a"] = jnp.stack([a_re[:S5_GROUPS].reshape(-1), a_im[:S5_GROUPS].reshape(-1),
                           a_re[S5_GROUPS:].reshape(-1), a_im[S5_GROUPS:].reshape(-1)])
    b_in, c_out = [], []
    for d in range(2):
        sl = slice(d * S5_GROUPS, (d + 1) * S5_GROUPS)
        b_in.append(jnp.concatenate([_block_diag(bb_re[sl].transpose(0, 2, 1)),
                                     _block_diag(bb_im[sl].transpose(0, 2, 1))], axis=1))
        c_out.append(jnp.concatenate([_block_diag(s5_c_re[d].transpose(0, 2, 1)),
                                      -_block_diag(s5_c_im[d].transpose(0, 2, 1))], axis=0))
    p["s5_b_in"] = jnp.stack(b_in).astype(BF16)
    p["s5_c_out"] = jnp.stack(c_out).astype(BF16)
    p["s5_post"] = [row(s5_d), s5_glu_w.astype(BF16), row(s5_glu_b)]
    p["s5_proj"] = s5_proj.astype(BF16)
    p["w_out"] = w_out.astype(BF16)
    p["ln1"] = [row(ln1_g), row(ln1_b)]
    p["w1"] = mlp_w1.astype(BF16)
    p["w2"] = mlp_w2.astype(BF16)
    p["ln2"] = [row(ln2_g), row(ln2_b)]
    return p


def _layer(x, p, bsz, seq, rope):
    t = bsz * seq
    tm = _tile(t, 256)
    z_rw, z_mla, z_s5, gates = _inproj(x, p["w_rw"], p["w_mla"], p["w_s5"], p["w_gate"], tm)

    tb = _tile(seq, 512)
    z3 = z_rw.reshape(bsz, seq, RW_IN)
    o_f, bonus, rgate, o_b = _rwkv_scan(z3, *p["rw"], *p["rw_dir"], *p["rw_shared"], tb=tb)
    flat = lambda a: a.reshape(t, RW_W)

    q, k, v = _mla_qkv(z_mla, rope[0], rope[1], p["q_norm"], p["w_uq"], p["kv_norm"], p["w_uk"], p["w_uv"],
                       seq, _tile(seq, 512))
    o_mla = _mla_attn(q, k, v, bsz, seq, _tile(seq, 512), _tile(seq, 512))

    u_tm = jnp.pad(z_s5.reshape(bsz, seq, S5_W).transpose(1, 0, 2), ((0, 0), (0, S5_BATCH - bsz), (0, 0)))
    u_tm = u_tm.reshape(seq * S5_BATCH, S5_W)
    y_f, y_b = _s5_scan(u_tm, p["s5_a"], p["s5_b_in"], p["s5_c_out"], seq, _tile(seq, 64))
    y_s5 = _s5_post(y_f, y_b, u_tm, *p["s5_post"], _tile(seq * S5_BATCH, 1024))
    y_s5 = y_s5.reshape(seq, S5_BATCH, S5_W)[:, :bsz].transpose(1, 0, 2).reshape(t, S5_W)

    x1 = _merge(x, flat(o_f), flat(o_b), flat(bonus), flat(rgate), o_mla, y_s5, gates,
                *p["rw_gn"], p["rw_proj"], p["mla_proj"], p["s5_proj"], p["w_out"], *p["ln1"], tm)
    return _mlp(x1, p["w1"], p["w2"], *p["ln2"], _tile(t, 1024), 512)


def _trunk(x, layers):
    bsz, seq, _ = x.shape
    assert bsz <= S5_BATCH and seq % RW_CHUNK == 0
    rope = _rope_tables(seq)
    h = x.reshape(bsz * seq, D_MODEL)
    for p in layers:
        h = _layer(h, p, bsz, seq, rope)
    return h.reshape(bsz, seq, D_MODEL)


def kernel(x_prompt, x_sample, w_in, rw_mu, rw_w0, rw_w2, rw_a0, rw_a2, rw_g2, rw_k_k, rw_k_a, rw_r_k, rw_gn_g, rw_gn_b, rw_proj, mla_q_norm, mla_w_uq, mla_kv_norm, mla_w_ukv, mla_proj, s5_lam_re, s5_lam_im, s5_log_dt, s5_b_re, s5_b_im, s5_c_re, s5_c_im, s5_d, s5_glu_w, s5_glu_b, s5_proj, w_out, ln1_g, ln1_b, mlp_w1, mlp_w2, ln2_g, ln2_b):
    weights = (w_in, rw_mu, rw_w0, rw_w2, rw_a0, rw_a2, rw_g2, rw_k_k, rw_k_a, rw_r_k, rw_gn_g, rw_gn_b, rw_proj,
               mla_q_norm, mla_w_uq, mla_kv_norm, mla_w_ukv, mla_proj,
               s5_lam_re, s5_lam_im, s5_log_dt, s5_b_re, s5_b_im, s5_c_re, s5_c_im, s5_d, s5_glu_w, s5_glu_b,
               s5_proj, w_out, ln1_g, ln1_b, mlp_w1, mlp_w2, ln2_g, ln2_b)
    layers = [_prep_layer(*[wt[l] for wt in weights]) for l in range(w_in.shape[0])]
    return _trunk(x_prompt, layers), _trunk(x_sample, layers)
```

```python
import functools
import math

import numpy as np
import jax
import jax.numpy as jnp
from jax import lax
from jax.experimental import pallas as pl
from jax.experimental.pallas import tpu as pltpu

F32 = jnp.float32
BF16 = jnp.bfloat16
HIGHEST = lax.Precision.HIGHEST

D_MODEL = 1024
DEPTH = 2
RW_HEADS = 4
RW_HEAD = 64
RW_W = RW_HEADS * RW_HEAD
RW_LORA_W = 32
RW_LORA_A = 32
RW_LORA_G = 64
RW_GN_EPS = 64e-5
RW_IN = 3 * RW_W + RW_LORA_W + RW_LORA_A + RW_LORA_G
MLA_HEADS = 8
MLA_NOPE = 64
MLA_ROPE = 32
MLA_V = 64
MLA_Q_LORA = 256
MLA_KV_LORA = 128
MLA_W = MLA_HEADS * MLA_V
MLA_QK_PAD = 128
MLA_IN_PAD = MLA_Q_LORA + MLA_KV_LORA + MLA_QK_PAD
MLA_SCALE = (MLA_NOPE + MLA_ROPE) ** -0.5
MLA_VT_ROWS = MLA_V + 16
LOG2_E = math.log2(math.e)
MLA_SHIFT_SLACK = 64.0
ROPE_THETA = 10000.0
RMS_EPS = 1e-6
S5_W = 256
S5_GROUP = 16
S5_GROUPS = S5_W // S5_GROUP
S5_STATE = 64
S5_N = S5_GROUPS * S5_STATE
S5_BATCH = 8
D_FF = 4 * D_MODEL
LN_EPS = 1e-5
N_BRANCH = 3
DN_ALPHA = (2 * DEPTH) ** 0.25
OFF_MLA = RW_IN
OFF_S5 = OFF_MLA + MLA_Q_LORA + MLA_KV_LORA + MLA_ROPE
OFF_GATE = OFF_S5 + S5_W
RW_CHUNK = 64
VMEM_LIMIT = 56 * 1024 * 1024


def _params(*sem):
    return pltpu.CompilerParams(dimension_semantics=sem, vmem_limit_bytes=VMEM_LIMIT)


def _mm(a, b):
    return jnp.dot(a.astype(BF16), b.astype(BF16), preferred_element_type=F32)


def _mm_nt(a, b):
    return lax.dot_general(a.astype(BF16), b.astype(BF16), (((1,), (1,)), ((), ())),
                           preferred_element_type=F32)


def _mm_tn(a, b):
    return lax.dot_general(a.astype(BF16), b.astype(BF16), (((0,), (0,)), ((), ())),
                           preferred_element_type=F32)


def _mm_f32(a, b):
    return jnp.dot(a, b, preferred_element_type=F32, precision=HIGHEST)


def _bf16_terms(x, n):
    terms = []
    for _ in range(n):
        t = x.astype(BF16)
        terms.append(t)
        x = x - t.astype(F32)
    return terms


def _mm_split(a, b, a_terms, b_terms):
    at = _bf16_terms(a, a_terms)
    bt = _bf16_terms(b, b_terms)
    out = None
    for i, x in enumerate(at):
        for j, y in enumerate(bt):
            if i + j < max(a_terms, b_terms):
                d = jnp.dot(x, y, preferred_element_type=F32)
                out = d if out is None else out + d
    return out


def _sigmoid(x):
    return 1.0 / (1.0 + jnp.exp(-x))


def _full(a):
    nd = a.ndim
    return pl.BlockSpec(a.shape, lambda *_: (0,) * nd)


def _layer_norm(x, g, b):
    mu = jnp.mean(x, -1, keepdims=True)
    xc = x - mu
    var = jnp.mean(xc * xc, -1, keepdims=True)
    return xc * lax.rsqrt(var + LN_EPS) * g + b


def _inproj_kernel(x_ref, wrw_ref, wmla_ref, ws5_ref, wg_ref, zrw_ref, zmla_ref, zs5_ref, g_ref):
    xb = x_ref[...].astype(BF16)
    zrw_ref[...] = jnp.dot(xb, wrw_ref[...], preferred_element_type=F32)
    zmla_ref[...] = jnp.dot(xb, wmla_ref[...], preferred_element_type=F32)
    zs5_ref[...] = jnp.dot(xb, ws5_ref[...], preferred_element_type=F32)
    g_ref[...] = _sigmoid(jnp.dot(xb, wg_ref[...], preferred_element_type=F32))


def _inproj(x, w_rw, w_mla, w_s5, w_g, tm):
    t = x.shape[0]
    row = lambda n: pl.BlockSpec((tm, n), lambda i: (i, 0))
    return pl.pallas_call(
        _inproj_kernel,
        grid=(t // tm,),
        in_specs=[row(D_MODEL), _full(w_rw), _full(w_mla), _full(w_s5), _full(w_g)],
        out_specs=[row(RW_IN), row(MLA_IN_PAD), row(S5_W), row(N_BRANCH * D_MODEL)],
        out_shape=[jax.ShapeDtypeStruct((t, RW_IN), F32), jax.ShapeDtypeStruct((t, MLA_IN_PAD), F32),
                   jax.ShapeDtypeStruct((t, S5_W), F32), jax.ShapeDtypeStruct((t, N_BRANCH * D_MODEL), F32)],
        compiler_params=_params("parallel"),
        name="inproj",
    )(x, w_rw, w_mla, w_s5, w_g)


def _rwkv_kernel(nblk, tb, nb,
                 zf_ref, zfp_ref, zfn_ref, zb_ref, zbp_ref, zbn_ref,
                 mu_ref, w0_ref, w2_ref, a0_ref, a2_ref, g2_ref, kk_ref, ka_ref, rk_ref,
                 of_ref, bonus_ref, gate_ref, ob_ref, r_s, v_s, kn_s, lw_s, b_s, kd_s, state):
    step = pl.program_id(1)
    c = RW_CHUNK
    w = RW_W
    hr = lax.broadcasted_iota(jnp.int32, (w, w), 0) // RW_HEAD
    hc = lax.broadcasted_iota(jnp.int32, (w, w), 1) // RW_HEAD
    same_head = hr == hc
    head_ones = same_head.astype(F32)

    def prepare(d, bi, z_ref, zp_ref, zn_ref, blk):
        z = z_ref[bi]
        prev_row = jnp.where(blk == 0, 0.0, zp_ref[bi, 7:8, :])
        next_row = jnp.where(blk == nblk - 1, 0.0, zn_ref[bi, 0:1, :])
        rows = lax.broadcasted_iota(jnp.int32, z.shape, 0)
        z_prev = jnp.where(rows == 0, prev_row, pltpu.roll(z, 1, axis=0))
        z_next = jnp.where(rows == tb - 1, next_row, pltpu.roll(z, tb - 1, axis=0))
        z = z + mu_ref[...] * (0.5 * (z_prev + z_next) - z)
        r = z[:, 0:w]
        k = z[:, w:2 * w]
        v = z[:, 2 * w:3 * w]
        xw = z[:, 3 * w:3 * w + RW_LORA_W]
        xa = z[:, 3 * w + RW_LORA_W:3 * w + RW_LORA_W + RW_LORA_A]
        xg = z[:, 3 * w + RW_LORA_W + RW_LORA_A:]
        kk = k * kk_ref[...]
        kk_ss = _mm_split(kk * kk, head_ones, 2, 1)
        kk = kk * lax.rsqrt(jnp.maximum(kk_ss, 1e-12))
        y = w0_ref[d] + _mm_split(jnp.tanh(xw), w2_ref[d], 2, 2)
        lw = -math.exp(-0.5) * _sigmoid(y)
        a = _sigmoid(a0_ref[d] + _mm(xa, a2_ref[d]))
        r_s[d, bi] = r
        v_s[d, bi] = v
        kn_s[d, bi] = kk
        lw_s[d, bi] = lw
        b_s[d, bi] = kk * a
        kd_s[d, bi] = k * (1.0 + (a - 1.0) * ka_ref[...])
        if d == 0:
            rk = _mm_split(r * k * rk_ref[...], head_ones, 2, 1)
            bonus_ref[bi] = rk * v
            gate_ref[bi] = _mm(_sigmoid(xg), g2_ref[...])

    for bi in range(nb):
        prepare(0, bi, zf_ref, zfp_ref, zfn_ref, step)
        prepare(1, bi, zb_ref, zbp_ref, zbn_ref, nblk - 1 - step)

    @pl.when(step == 0)
    def _():
        state[...] = jnp.zeros_like(state)

    ti = lax.broadcasted_iota(jnp.int32, (c, c), 0)
    si = lax.broadcasted_iota(jnp.int32, (c, c), 1)
    tw = lax.broadcasted_iota(jnp.int32, (c, w), 0)
    sw = lax.broadcasted_iota(jnp.int32, (c, w), 1) % RW_HEAD
    eye = (tw == sw).astype(F32)

    def stack(x):
        return jnp.where(same_head, jnp.concatenate([x] * RW_HEADS, axis=0), 0.0).astype(BF16)

    nchunk = tb // c

    def chunk(reverse, bi, cpos):
        d = 1 if reverse else 0
        o_ref = ob_ref if reverse else of_ref
        cum_mat = ((si >= ti) if reverse else (si <= ti)).astype(F32)
        strict = (tw < sw) if reverse else (tw > sw)
        incl = (tw <= sw) if reverse else (tw >= sw)
        sl = pl.ds(pl.multiple_of(cpos * c, c), c)
        lwc = lw_s[d, bi, sl, :]
        l_in = _mm_split(cum_mat, lwc, 1, 3)
        l_tot = jnp.sum(lwc, axis=0, keepdims=True)
        e_in = jnp.exp(l_in)
        e_neg = jnp.exp(-l_in)
        e_tot = jnp.exp(l_tot)
        kap = kn_s[d, bi, sl, :] * jnp.exp(l_in - lwc)
        bt = b_s[d, bi, sl, :] * e_neg
        kt = kd_s[d, bi, sl, :] * e_neg
        rt = (r_s[d, bi, sl, :] * e_in).astype(BF16)
        v = v_s[d, bi, sl, :]
        kap_w, bt_w, kt_w, v_w = stack(kap), stack(bt), stack(kt), stack(v)
        kap = kap.astype(BF16)
        end_w = jnp.concatenate([bt * e_tot, kt * e_tot], axis=0).astype(BF16)
        s_bd = state[d, bi]
        s_w = s_bd.astype(BF16)
        yield
        a_b = jnp.where(strict, _mm_nt(kap, bt_w), 0.0)
        a_k = jnp.where(strict, _mm_nt(kap, kt_w), 0.0)
        yield
        m_b = jnp.where(incl, _mm_nt(rt, bt_w), 0.0)
        m_k = jnp.where(incl, _mm_nt(rt, kt_w), 0.0)
        yield
        assert c == 2 ** int(math.log2(c)) and int(math.log2(c)) % 2 == 0
        powers = [-a_b]
        pairs = []
        akv = _mm(a_k, v_w)
        for level in range(1, int(math.log2(c))):
            sq_w = stack(powers[-1])
            powers.append(_mm(powers[-1], sq_w))
            if level % 2 == 0:
                lo = eye + powers[level - 2]
                pairs.append(lo + _mm(lo, sq_w))
            yield
        lo = eye + powers[-2]
        pairs.append(lo + _mm(lo, stack(powers[-1])))
        tinv = pairs[0]
        for pr in pairs[1:-1]:
            tinv = _mm(tinv, stack(pr))
        yield
        tinv = _mm(tinv, stack(pairs[-1]))
        yield
        pm = _mm(tinv, kap_w)
        qm = _mm(tinv, stack(akv))
        yield
        u = -(_mm_nt(pm, s_w) + qm)
        yield
        o = _mm_nt(rt, s_w) + _mm(m_b, stack(u)) + _mm(m_k, v_w)
        upd = _mm_tn(jnp.concatenate([u, v], axis=0), end_w)
        yield
        o_ref[bi, sl, :] = o
        state[d, bi] = s_bd * e_tot + jnp.where(same_head, upd, 0.0)

    def all_scans(ci, carry):
        scans = [chunk(False, bi, ci) for bi in range(nb)] + [chunk(True, bi, nchunk - 1 - ci) for bi in range(nb)]
        while scans:
            scans = [g for g in scans if next(g, StopIteration) is not StopIteration]
        return carry

    lax.fori_loop(0, nchunk, all_scans, 0)


def _rwkv_scan(z, mu, w0, w2, a0, a2, g2, k_k, k_a, r_k, tb):
    bsz, seq, _ = z.shape
    nblk = seq // tb
    nb = 2 if bsz % 2 == 0 else 1
    z8 = z.reshape(bsz, seq // 8, 8, RW_IN)
    t8 = tb // 8

    def specs(pos):
        z_spec = pl.BlockSpec((nb, tb, RW_IN), lambda b, i: (b, pos(i), 0))
        zp_spec = pl.BlockSpec((nb, None, 8, RW_IN), lambda b, i: (b, jnp.maximum(pos(i) * t8 - 1, 0), 0, 0))
        zn_spec = pl.BlockSpec((nb, None, 8, RW_IN),
                               lambda b, i: (b, jnp.minimum((pos(i) + 1) * t8, seq // 8 - 1), 0, 0))
        o_spec = pl.BlockSpec((nb, tb, RW_W), lambda b, i: (b, pos(i), 0))
        return [z_spec, zp_spec, zn_spec], o_spec

    in_f, o_f = specs(lambda i: i)
    in_b, o_b = specs(lambda i: nblk - 1 - i)
    o_shape = jax.ShapeDtypeStruct((bsz, seq, RW_W), F32)
    weights = [mu, w0, w2, a0, a2, g2, k_k, k_a, r_k]
    return pl.pallas_call(
        functools.partial(_rwkv_kernel, nblk, tb, nb),
        grid=(bsz // nb, nblk),
        in_specs=in_f + in_b + [_full(wt) for wt in weights],
        out_specs=[o_f, o_f, o_f, o_b],
        out_shape=[o_shape] * 4,
        scratch_shapes=[pltpu.VMEM((2, nb, tb, RW_W), F32)] * 6 + [pltpu.VMEM((2, nb, RW_W, RW_W), F32)],
        compiler_params=_params("parallel", "arbitrary"),
        name="rwkv_scan",
    )(z, z8, z8, z, z8, z8, *weights)


def _rope_tables(seq):
    half = MLA_ROPE // 2
    inv = (ROPE_THETA ** (-np.arange(half, dtype=np.float32) / half)).astype(np.float32)
    ang = np.arange(seq, dtype=np.float32)[:, None] * inv[None, :]
    cos = np.cos(ang).astype(np.float32)
    sin = np.sin(ang).astype(np.float32)
    ct = np.zeros((seq, MLA_QK_PAD), np.float32)
    st = np.zeros((seq, MLA_QK_PAD), np.float32)
    ct[:, :MLA_NOPE] = 1.0
    ct[:, MLA_NOPE:MLA_NOPE + half] = cos
    ct[:, MLA_NOPE + half:MLA_NOPE + 2 * half] = cos
    st[:, MLA_NOPE:MLA_NOPE + half] = -sin
    st[:, MLA_NOPE + half:MLA_NOPE + 2 * half] = sin
    return jnp.asarray(ct), jnp.asarray(st), jnp.asarray(ct.T), jnp.asarray(st.T)


def _rope(x, cos_t, sin_t):
    half = MLA_ROPE // 2
    lane = lax.broadcasted_iota(jnp.int32, x.shape, 1)
    swapped = jnp.where(lane < MLA_NOPE + half,
                        pltpu.roll(x, MLA_QK_PAD - half, axis=1), pltpu.roll(x, half, axis=1))
    return x * cos_t + swapped * sin_t


def _rope_rows(x, cos_t, sin_t):
    half = MLA_ROPE // 2
    a, b = MLA_NOPE, MLA_NOPE + half
    swapped = jnp.concatenate([x[0:a], x[b:b + half], x[a:b], x[b + half:]], axis=0)
    return x * cos_t + swapped * sin_t


def _mla_qkv_kernel(z_ref, cos_ref, sin_ref, cosr_ref, sinr_ref, qn_ref, wuq_ref, kvn_ref, wuk_ref, wuv_ref,
                    q_ref, k_ref, v_ref):
    z = z_ref[...]
    cos_t = cos_ref[...]
    sin_t = sin_ref[...]
    cos_r = cosr_ref[...]
    sin_r = sinr_ref[...]
    c_q = z[:, :MLA_Q_LORA]
    c_kv = z[:, MLA_Q_LORA:MLA_Q_LORA + MLA_KV_LORA]
    k_rope = _rope(z[:, MLA_Q_LORA + MLA_KV_LORA:], cos_t, sin_t)
    c_q = (c_q * lax.rsqrt(jnp.mean(c_q * c_q, -1, keepdims=True) + RMS_EPS) * qn_ref[...]).astype(BF16)
    c_kv = (c_kv * lax.rsqrt(jnp.mean(c_kv * c_kv, -1, keepdims=True) + RMS_EPS) * kvn_ref[...]).astype(BF16)
    ones_row = (lax.broadcasted_iota(jnp.int32, (MLA_VT_ROWS - MLA_V, z.shape[0]), 0) == 0).astype(BF16)
    nt = (((1,), (1,)), ((), ()))
    for h in range(MLA_HEADS):
        q_t = lax.dot_general(wuq_ref[h], c_q, nt, preferred_element_type=F32)
        q_ref[h] = (_rope_rows(q_t, cos_r, sin_r) * (MLA_SCALE * LOG2_E)).astype(BF16)
        kh = jnp.dot(c_kv, wuk_ref[h], preferred_element_type=F32)
        k_ref[h] = (kh + k_rope).astype(BF16)
        v_t = lax.dot_general(wuv_ref[h], c_kv, nt, preferred_element_type=F32)
        v_ref[h, 0:MLA_V, :] = v_t.astype(BF16)
        v_ref[h, MLA_V:MLA_VT_ROWS, :] = ones_row


def _mla_qkv(z, rope, q_norm, w_uq, kv_norm, w_uk, w_uv, seq, tm):
    t = z.shape[0]
    nseq = seq // tm
    tab = pl.BlockSpec((tm, MLA_QK_PAD), lambda i: (i % nseq, 0))
    tab_r = pl.BlockSpec((MLA_QK_PAD, tm), lambda i: (0, i % nseq))
    cols = lambda n: pl.BlockSpec((MLA_HEADS, n, tm), lambda i: (0, 0, i))
    return pl.pallas_call(
        _mla_qkv_kernel,
        grid=(t // tm,),
        in_specs=[pl.BlockSpec((tm, MLA_IN_PAD), lambda i: (i, 0)), tab, tab, tab_r, tab_r,
                  _full(q_norm), _full(w_uq), _full(kv_norm), _full(w_uk), _full(w_uv)],
        out_specs=[cols(MLA_QK_PAD), pl.BlockSpec((MLA_HEADS, tm, MLA_QK_PAD), lambda i: (0, i, 0)),
                   cols(MLA_VT_ROWS)],
        out_shape=[jax.ShapeDtypeStruct((MLA_HEADS, MLA_QK_PAD, t), BF16),
                   jax.ShapeDtypeStruct((MLA_HEADS, t, MLA_QK_PAD), BF16),
                   jax.ShapeDtypeStruct((MLA_HEADS, MLA_VT_ROWS, t), BF16)],
        compiler_params=_params("parallel"),
        name="mla_qkv",
    )(z, *rope, q_norm, w_uq, kv_norm, w_uk, w_uv)


def _mla_attn_kernel(nk, q_ref, k_ref, v_ref, o_ref, m_s, acc_s, bm_s, pv_s):
    j = pl.program_id(2)

    @pl.when(j == 0)
    def _():
        m_s[...] = jnp.full(m_s.shape, -jnp.inf, F32)
        acc_s[...] = jnp.zeros_like(acc_s)

    def scores(h):
        return jnp.dot(k_ref[h], q_ref[h], preferred_element_type=F32)

    s_next = scores(0)
    for h in range(MLA_HEADS):
        s = s_next
        if h + 1 < MLA_HEADS:
            s_next = scores(h + 1)
        bm_s[h:h + 1, :] = jnp.max(s, axis=0, keepdims=True)
        p = jnp.exp2(s - m_s[h:h + 1, :]).astype(BF16)
        pv_s[h] = jnp.dot(v_ref[h], p, preferred_element_type=F32)

    stale_shift_ok = jnp.max(bm_s[...] - m_s[...]) < MLA_SHIFT_SLACK

    @pl.when(stale_shift_ok)
    def _():
        for h in range(MLA_HEADS):
            m_prev = m_s[h:h + 1, :]
            m_new = jnp.maximum(m_prev, bm_s[h:h + 1, :])
            acc_s[h] = (acc_s[h] + pv_s[h]) * jnp.exp2(m_prev - m_new)
            m_s[h:h + 1, :] = m_new

    @pl.when(jnp.logical_not(stale_shift_ok))
    def _():
        for h in range(MLA_HEADS):
            s = scores(h)
            m_prev = m_s[h:h + 1, :]
            m_new = jnp.maximum(m_prev, bm_s[h:h + 1, :])
            p = jnp.exp2(s - m_new).astype(BF16)
            acc_s[h] = jnp.exp2(m_prev - m_new) * acc_s[h] + jnp.dot(v_ref[h], p, preferred_element_type=F32)
            m_s[h:h + 1, :] = m_new

    @pl.when(j == nk - 1)
    def _():
        heads = []
        for h in range(MLA_HEADS):
            acc = acc_s[h]
            heads.append(acc[0:MLA_V] / acc[MLA_V:MLA_V + 1])
        o_ref[...] = jnp.concatenate(heads, axis=0).T


def _mla_attn(q, k, v, bsz, seq, tq, tk):
    nq = seq // tq
    nk = seq // tk
    t = bsz * seq
    return pl.pallas_call(
        functools.partial(_mla_attn_kernel, nk),
        grid=(bsz, nq, nk),
        in_specs=[pl.BlockSpec((MLA_HEADS, MLA_QK_PAD, tq), lambda b, i, j: (0, 0, b * nq + i)),
                  pl.BlockSpec((MLA_HEADS, tk, MLA_QK_PAD), lambda b, i, j: (0, b * nk + j, 0)),
                  pl.BlockSpec((MLA_HEADS, MLA_VT_ROWS, tk), lambda b, i, j: (0, 0, b * nk + j))],
        out_specs=pl.BlockSpec((tq, MLA_W), lambda b, i, j: (b * nq + i, 0)),
        out_shape=jax.ShapeDtypeStruct((t, MLA_W), F32),
        scratch_shapes=[pltpu.VMEM((MLA_HEADS, tq), F32), pltpu.VMEM((MLA_HEADS, MLA_VT_ROWS, tq), F32),
                        pltpu.VMEM((MLA_HEADS, tq), F32), pltpu.VMEM((MLA_HEADS, MLA_VT_ROWS, tq), F32)],
        compiler_params=_params("parallel", "parallel", "arbitrary"),
        name="mla_attn",
    )(q, k, v)


def _s5_disc_kernel(lre_ref, lim_ref, ldt_ref, bre_ref, bim_ref, are_ref, aim_ref, bbre_ref, bbim_ref):
    lam_re = lre_ref[...]
    lam_im = lim_ref[...]
    dt = jnp.exp(ldt_ref[...])
    mag = jnp.exp(lam_re * dt)
    ab_re = mag * jnp.cos(lam_im * dt)
    ab_im = mag * jnp.sin(lam_im * dt)
    den = lam_re * lam_re + lam_im * lam_im
    nr = ab_re - 1.0
    ni = ab_im
    coef_re = (nr * lam_re + ni * lam_im) / den
    coef_im = (ni * lam_re - nr * lam_im) / den
    are_ref[...] = ab_re
    aim_ref[...] = ab_im
    b_re = bre_ref[...]
    b_im = bim_ref[...]
    bbre_ref[...] = coef_re * b_re - coef_im * b_im
    bbim_ref[...] = coef_re * b_im + coef_im * b_re


def _s5_discretise(lam_re, lam_im, log_dt, b_re, b_im):
    shape = (2, S5_GROUPS, S5_STATE, S5_GROUP)
    flat = lambda a: jnp.broadcast_to(a, shape).reshape(-1, 128)
    out = jax.ShapeDtypeStruct((math.prod(shape) // 128, 128), F32)
    a_re, a_im, bb_re, bb_im = pl.pallas_call(_s5_disc_kernel, out_shape=[out] * 4, name="s5_disc")(
        flat(lam_re[..., None]), flat(lam_im[..., None]), flat(log_dt[..., None, None]), flat(b_re), flat(b_im))
    g2 = 2 * S5_GROUPS
    mat = lambda a: a.reshape(g2, S5_STATE, S5_GROUP)
    return mat(a_re)[..., 0], mat(a_im)[..., 0], mat(bb_re), mat(bb_im)


def _block_diag(blocks):
    g, m, n = blocks.shape
    eye = jnp.eye(g, dtype=blocks.dtype)
    return (eye[:, None, :, None] * blocks[:, :, None, :]).reshape(g * m, g * n)


def _s5_scan_kernel(tb, uf_ref, ub_ref, a_ref, bin_ref, cout_ref, yf_ref, yb_ref, xf_s, xb_s, carry_s):
    i = pl.program_id(0)
    n = S5_N
    rows = S5_BATCH

    @pl.when(i == 0)
    def _():
        carry_s[...] = jnp.zeros_like(carry_s)

    xf_s[...] = _mm(uf_ref[...], bin_ref[0])
    xb_s[...] = _mm(ub_ref[...], bin_ref[1])
    af_re = jnp.broadcast_to(a_ref[0:1, :], (rows, n))
    af_im = jnp.broadcast_to(a_ref[1:2, :], (rows, n))
    ab_re = jnp.broadcast_to(a_ref[2:3, :], (rows, n))
    ab_im = jnp.broadcast_to(a_ref[3:4, :], (rows, n))

    def step(t, carry):
        f_re, f_im, b_re, b_im = carry
        sf = pl.ds(pl.multiple_of(t * rows, rows), rows)
        sb = pl.ds(pl.multiple_of((tb - 1 - t) * rows, rows), rows)
        nf_re = af_re * f_re - af_im * f_im + xf_s[sf, 0:n]
        nf_im = af_re * f_im + af_im * f_re + xf_s[sf, n:2 * n]
        nb_re = ab_re * b_re - ab_im * b_im + xb_s[sb, 0:n]
        nb_im = ab_re * b_im + ab_im * b_re + xb_s[sb, n:2 * n]
        xf_s[sf, 0:n] = nf_re
        xf_s[sf, n:2 * n] = nf_im
        xb_s[sb, 0:n] = nb_re
        xb_s[sb, n:2 * n] = nb_im
        return nf_re, nf_im, nb_re, nb_im

    init = (carry_s[0], carry_s[1], carry_s[2], carry_s[3])
    f_re, f_im, b_re, b_im = lax.fori_loop(0, tb, step, init)
    carry_s[0] = f_re
    carry_s[1] = f_im
    carry_s[2] = b_re
    carry_s[3] = b_im
    yf_ref[...] = _mm(xf_s[...], cout_ref[0])
    yb_ref[...] = _mm(xb_s[...], cout_ref[1])


def _s5_scan(u_tm, a_vec, b_in, c_out, seq, tb):
    nblk = seq // tb
    rows = tb * S5_BATCH
    fwd = pl.BlockSpec((rows, S5_W), lambda i: (i, 0))
    bwd = pl.BlockSpec((rows, S5_W), lambda i: (nblk - 1 - i, 0))
    shape = jax.ShapeDtypeStruct((seq * S5_BATCH, S5_W), F32)
    return pl.pallas_call(
        functools.partial(_s5_scan_kernel, tb),
        grid=(nblk,),
        in_specs=[fwd, bwd, _full(a_vec), _full(b_in), _full(c_out)],
        out_specs=[fwd, bwd],
        out_shape=[shape, shape],
        scratch_shapes=[pltpu.VMEM((rows, 2 * S5_N), F32), pltpu.VMEM((rows, 2 * S5_N), F32),
                        pltpu.VMEM((4, S5_BATCH, S5_N), F32)],
        compiler_params=_params("arbitrary"),
        name="s5_scan",
    )(u_tm, u_tm, a_vec, b_in, c_out)


def _s5_post_kernel(yf_ref, yb_ref, u_ref, d_ref, w_ref, b_ref, o_ref):
    y = yf_ref[...] + yb_ref[...] + d_ref[...] * u_ref[...]
    y = jax.nn.gelu(y, approximate=True)
    o_ref[...] = y * _sigmoid(_mm(y, w_ref[...]) + b_ref[...])


def _s5_post(yf, yb, u_tm, d_skip, glu_w, glu_b, tm):
    t = u_tm.shape[0]
    row = pl.BlockSpec((tm, S5_W), lambda i: (i, 0))
    return pl.pallas_call(
        _s5_post_kernel,
        grid=(t // tm,),
        in_specs=[row, row, row, _full(d_skip), _full(glu_w), _full(glu_b)],
        out_specs=row,
        out_shape=jax.ShapeDtypeStruct((t, S5_W), F32),
        compiler_params=_params("parallel"),
        name="s5_post",
    )(yf, yb, u_tm, d_skip, glu_w, glu_b)


def _merge_kernel(x_ref, of_ref, ob_ref, bonus_ref, rgate_ref, mla_ref, s5_ref, g_ref,
                  gng_ref, gnb_ref, wrw_ref, wmla_ref, ws5_ref, wout_ref, lng_ref, lnb_ref, o_ref):
    w = RW_W
    hr = lax.broadcasted_iota(jnp.int32, (w, w), 0) // RW_HEAD
    hc = lax.broadcasted_iota(jnp.int32, (w, w), 1) // RW_HEAD
    head_mean = (hr == hc).astype(F32) * (1.0 / RW_HEAD)
    o = of_ref[...] + ob_ref[...]
    mean = _mm_split(o, head_mean, 2, 1)
    oc = o - mean
    var = _mm_split(oc * oc, head_mean, 2, 1)
    o = oc * lax.rsqrt(var + RW_GN_EPS) * gng_ref[...] + gnb_ref[...]
    y_rw = _mm((o + bonus_ref[...]) * rgate_ref[...], wrw_ref[...])
    y_mla = _mm(mla_ref[...], wmla_ref[...])
    y_s5 = _mm(s5_ref[...], ws5_ref[...])
    d = D_MODEL
    merged = g_ref[:, 0:d] * y_rw + g_ref[:, d:2 * d] * y_mla + g_ref[:, 2 * d:3 * d] * y_s5
    o_ref[...] = _layer_norm(DN_ALPHA * x_ref[...] + _mm(merged, wout_ref[...]), lng_ref[...], lnb_ref[...])


def _merge(x, o_f, o_b, bonus, rgate, o_mla, y_s5, gates, gn_g, gn_b, w_rw, w_mla, w_s5, w_out, ln_g, ln_b, tm):
    t = x.shape[0]
    row = lambda n: pl.BlockSpec((tm, n), lambda i: (i, 0))
    weights = [gn_g, gn_b, w_rw, w_mla, w_s5, w_out, ln_g, ln_b]
    return pl.pallas_call(
        _merge_kernel,
        grid=(t // tm,),
        in_specs=[row(D_MODEL), row(RW_W), row(RW_W), row(RW_W), row(RW_W), row(MLA_W), row(S5_W),
                  row(N_BRANCH * D_MODEL)] + [_full(wt) for wt in weights],
        out_specs=row(D_MODEL),
        out_shape=jax.ShapeDtypeStruct((t, D_MODEL), F32),
        compiler_params=_params("parallel"),
        name="merge_ln1",
    )(x, o_f, o_b, bonus, rgate, o_mla, y_s5, gates, *weights)


def _mlp_kernel(nf, x_ref, w1_ref, w2_ref, lng_ref, lnb_ref, o_ref, acc_s):
    j = pl.program_id(1)

    @pl.when(j == 0)
    def _():
        acc_s[...] = jnp.zeros_like(acc_s)

    h = jnp.maximum(_mm(x_ref[...], w1_ref[...]), 0.0)
    acc_s[...] += _mm(h * h, w2_ref[...])

    @pl.when(j == nf - 1)
    def _():
        o_ref[...] = _layer_norm(DN_ALPHA * x_ref[...] + acc_s[...], lng_ref[...], lnb_ref[...])


def _mlp(x, w1, w2, ln_g, ln_b, tm, tf):
    t = x.shape[0]
    nf = D_FF // tf
    return pl.pallas_call(
        functools.partial(_mlp_kernel, nf),
        grid=(t // tm, nf),
        in_specs=[pl.BlockSpec((tm, D_MODEL), lambda i, j: (i, 0)),
                  pl.BlockSpec((D_MODEL, tf), lambda i, j: (0, j)),
                  pl.BlockSpec((tf, D_MODEL), lambda i, j: (j, 0)),
                  _full(ln_g), _full(ln_b)],
        out_specs=pl.BlockSpec((tm, D_MODEL), lambda i, j: (i, 0)),
        out_shape=jax.ShapeDtypeStruct((t, D_MODEL), F32),
        scratch_shapes=[pltpu.VMEM((tm, D_MODEL), F32)],
        compiler_params=_params("parallel", "arbitrary"),
        name="mlp_ln2",
    )(x, w1, w2, ln_g, ln_b)


def _tile(n, pref):
    t = min(n, pref)
    assert n % t == 0, (n, pref)
    return t


def _prep_layer(w_in, rw_mu, rw_w0, rw_w2, rw_a0, rw_a2, rw_g2, rw_k_k, rw_k_a, rw_r_k, rw_gn_g, rw_gn_b, rw_proj,
                mla_q_norm, mla_w_uq, mla_kv_norm, mla_w_ukv, mla_proj,
                s5_lam_re, s5_lam_im, s5_log_dt, s5_b_re, s5_b_im, s5_c_re, s5_c_im, s5_d, s5_glu_w, s5_glu_b,
                s5_proj, w_out, ln1_g, ln1_b, mlp_w1, mlp_w2, ln2_g, ln2_b):
    row = lambda a: a.reshape(1, -1)
    p = {}
    p["w_rw"] = w_in[:, :OFF_MLA].astype(BF16)
    w_mla = w_in[:, OFF_MLA:OFF_S5]
    zeros = lambda n: jnp.zeros((D_MODEL, n), F32)
    p["w_mla"] = jnp.concatenate([w_mla[:, :MLA_Q_LORA + MLA_KV_LORA], zeros(MLA_NOPE),
                                  w_mla[:, MLA_Q_LORA + MLA_KV_LORA:],
                                  zeros(MLA_QK_PAD - MLA_NOPE - MLA_ROPE)], axis=1).astype(BF16)
    p["w_s5"] = w_in[:, OFF_S5:OFF_GATE].astype(BF16)
    p["w_gate"] = w_in[:, OFF_GATE:].astype(BF16)
    p["rw"] = [row(rw_mu)]
    p["rw_dir"] = [rw_w0[:, None, :], rw_w2, rw_a0[:, None, :], rw_a2]
    p["rw_shared"] = [rw_g2, row(rw_k_k), row(rw_k_a), row(rw_r_k)]
    p["rw_gn"] = [row(rw_gn_g), row(rw_gn_b)]
    p["rw_proj"] = rw_proj.astype(BF16)
    uq = mla_w_uq.reshape(MLA_Q_LORA, MLA_HEADS, MLA_NOPE + MLA_ROPE)
    uq = jnp.pad(uq, ((0, 0), (0, 0), (0, MLA_QK_PAD - MLA_NOPE - MLA_ROPE)))
    p["w_uq"] = uq.transpose(1, 2, 0).astype(BF16)
    ukv = mla_w_ukv.reshape(MLA_KV_LORA, MLA_HEADS, MLA_NOPE + MLA_V)
    uk = jnp.pad(ukv[:, :, :MLA_NOPE], ((0, 0), (0, 0), (0, MLA_QK_PAD - MLA_NOPE)))
    p["w_uk"] = uk.transpose(1, 0, 2).astype(BF16)
    p["w_uv"] = ukv[:, :, MLA_NOPE:].transpose(1, 2, 0).astype(BF16)
    p["q_norm"] = row(mla_q_norm)
    p["kv_norm"] = row(mla_kv_norm)
    p["mla_proj"] = mla_proj.astype(BF16)
    a_re, a_im, bb_re, bb_im = _s5_discretise(s5_lam_re, s5_lam_im, s5_log_dt, s5_b_re, s5_b_im)
    p["s5_a"] = jnp.stack([a_re[:S5_GROUPS].reshape(-1), a_im[:S5_GROUPS].reshape(-1),
                           a_re[S5_GROUPS:].reshape(-1), a_im[S5_GROUPS:].reshape(-1)])
    b_in, c_out = [], []
    for d in range(2):
        sl = slice(d * S5_GROUPS, (d + 1) * S5_GROUPS)
        b_in.append(jnp.concatenate([_block_diag(bb_re[sl].transpose(0, 2, 1)),
                                     _block_diag(bb_im[sl].transpose(0, 2, 1))], axis=1))
        c_out.append(jnp.concatenate([_block_diag(s5_c_re[d].transpose(0, 2, 1)),
                                      -_block_diag(s5_c_im[d].transpose(0, 2, 1))], axis=0))
    p["s5_b_in"] = jnp.stack(b_in).astype(BF16)
    p["s5_c_out"] = jnp.stack(c_out).astype(BF16)
    p["s5_post"] = [row(s5_d), s5_glu_w.astype(BF16), row(s5_glu_b)]
    p["s5_proj"] = s5_proj.astype(BF16)
    p["w_out"] = w_out.astype(BF16)
    p["ln1"] = [row(ln1_g), row(ln1_b)]
    p["w1"] = mlp_w1.astype(BF16)
    p["w2"] = mlp_w2.astype(BF16)
    p["ln2"] = [row(ln2_g), row(ln2_b)]
    return p


def _layer(x, p, bsz, seq, rope):
    t = bsz * seq
    tm = _tile(t, 256)
    z_rw, z_mla, z_s5, gates = _inproj(x, p["w_rw"], p["w_mla"], p["w_s5"], p["w_gate"], tm)

    tb = _tile(seq, 512)
    z3 = z_rw.reshape(bsz, seq, RW_IN)
    o_f, bonus, rgate, o_b = _rwkv_scan(z3, *p["rw"], *p["rw_dir"], *p["rw_shared"], tb=tb)
    flat = lambda a: a.reshape(t, RW_W)

    q, k, v = _mla_qkv(z_mla, rope, p["q_norm"], p["w_uq"], p["kv_norm"], p["w_uk"], p["w_uv"],
                       seq, _tile(seq, 512))
    o_mla = _mla_attn(q, k, v, bsz, seq, _tile(seq, 512), _tile(seq, 1024))

    u_tm = jnp.pad(z_s5.reshape(bsz, seq, S5_W).transpose(1, 0, 2), ((0, 0), (0, S5_BATCH - bsz), (0, 0)))
    u_tm = u_tm.reshape(seq * S5_BATCH, S5_W)
    y_f, y_b = _s5_scan(u_tm, p["s5_a"], p["s5_b_in"], p["s5_c_out"], seq, _tile(seq, 64))
    y_s5 = _s5_post(y_f, y_b, u_tm, *p["s5_post"], _tile(seq * S5_BATCH, 1024))
    y_s5 = y_s5.reshape(seq, S5_BATCH, S5_W)[:, :bsz].transpose(1, 0, 2).reshape(t, S5_W)

    x1 = _merge(x, flat(o_f), flat(o_b), flat(bonus), flat(rgate), o_mla, y_s5, gates,
                *p["rw_gn"], p["rw_proj"], p["mla_proj"], p["s5_proj"], p["w_out"], *p["ln1"], tm)
    return _mlp(x1, p["w1"], p["w2"], *p["ln2"], _tile(t, 1024), 512)


def _trunk(x, layers):
    bsz, seq, _ = x.shape
    assert bsz <= S5_BATCH and seq % RW_CHUNK == 0
    rope = _rope_tables(seq)
    h = x.reshape(bsz * seq, D_MODEL)
    for p in layers:
        h = _layer(h, p, bsz, seq, rope)
    return h.reshape(bsz, seq, D_MODEL)


def kernel(x_prompt, x_sample, w_in, rw_mu, rw_w0, rw_w2, rw_a0, rw_a2, rw_g2, rw_k_k, rw_k_a, rw_r_k, rw_gn_g, rw_gn_b, rw_proj, mla_q_norm, mla_w_uq, mla_kv_norm, mla_w_ukv, mla_proj, s5_lam_re, s5_lam_im, s5_log_dt, s5_b_re, s5_b_im, s5_c_re, s5_c_im, s5_d, s5_glu_w, s5_glu_b, s5_proj, w_out, ln1_g, ln1_b, mlp_w1, mlp_w2, ln2_g, ln2_b):
    weights = (w_in, rw_mu, rw_w0, rw_w2, rw_a0, rw_a2, rw_g2, rw_k_k, rw_k_a, rw_r_k, rw_gn_g, rw_gn_b, rw_proj,
               mla_q_norm, mla_w_uq, mla_kv_norm, mla_w_ukv, mla_proj,
               s5_lam_re, s5_lam_im, s5_log_dt, s5_b_re, s5_b_im, s5_c_re, s5_c_im, s5_d, s5_glu_w, s5_glu_b,
               s5_proj, w_out, ln1_g, ln1_b, mlp_w1, mlp_w2, ln2_g, ln2_b)
    layers = [_prep_layer(*[wt[l] for wt in weights]) for l in range(w_in.shape[0])]
    return _trunk(x_prompt, layers), _trunk(x_sample, layers)
```

```python
import functools
import math

import numpy as np
import jax
import jax.numpy as jnp
from jax import lax
from jax.experimental import pallas as pl
from jax.experimental.pallas import tpu as pltpu

F32 = jnp.float32
BF16 = jnp.bfloat16
HIGHEST = lax.Precision.HIGHEST

D_MODEL = 1024
DEPTH = 2
RW_HEADS = 4
RW_HEAD = 64
RW_W = RW_HEADS * RW_HEAD
RW_LORA_W = 32
RW_LORA_A = 32
RW_LORA_G = 64
RW_GN_EPS = 64e-5
RW_IN = 3 * RW_W + RW_LORA_W + RW_LORA_A + RW_LORA_G
MLA_HEADS = 8
MLA_NOPE = 64
MLA_ROPE = 32
MLA_V = 64
MLA_Q_LORA = 256
MLA_KV_LORA = 128
MLA_W = MLA_HEADS * MLA_V
MLA_QK_PAD = 128
MLA_IN_PAD = MLA_Q_LORA + MLA_KV_LORA + MLA_QK_PAD
MLA_SCALE = (MLA_NOPE + MLA_ROPE) ** -0.5
MLA_VT_ROWS = MLA_V + 16
LOG2_E = math.log2(math.e)
MLA_SHIFT_SLACK = 64.0
ROPE_THETA = 10000.0
RMS_EPS = 1e-6
S5_W = 256
S5_GROUP = 16
S5_GROUPS = S5_W // S5_GROUP
S5_STATE = 64
S5_N = S5_GROUPS * S5_STATE
S5_BATCH = 8
D_FF = 4 * D_MODEL
LN_EPS = 1e-5
N_BRANCH = 3
DN_ALPHA = (2 * DEPTH) ** 0.25
OFF_MLA = RW_IN
OFF_S5 = OFF_MLA + MLA_Q_LORA + MLA_KV_LORA + MLA_ROPE
OFF_GATE = OFF_S5 + S5_W
RW_CHUNK = 64
VMEM_LIMIT = 56 * 1024 * 1024


def _params(*sem):
    return pltpu.CompilerParams(dimension_semantics=sem, vmem_limit_bytes=VMEM_LIMIT)


def _mm(a, b):
    return jnp.dot(a.astype(BF16), b.astype(BF16), preferred_element_type=F32)


def _mm_nt(a, b):
    return lax.dot_general(a.astype(BF16), b.astype(BF16), (((1,), (1,)), ((), ())),
                           preferred_element_type=F32)


def _mm_tn(a, b):
    return lax.dot_general(a.astype(BF16), b.astype(BF16), (((0,), (0,)), ((), ())),
                           preferred_element_type=F32)


def _mm_f32(a, b):
    return jnp.dot(a, b, preferred_element_type=F32, precision=HIGHEST)


def _bf16_terms(x, n):
    terms = []
    for _ in range(n):
        t = x.astype(BF16)
        terms.append(t)
        x = x - t.astype(F32)
    return terms


def _mm_split(a, b, a_terms, b_terms):
    at = _bf16_terms(a, a_terms)
    bt = _bf16_terms(b, b_terms)
    out = None
    for i, x in enumerate(at):
        for j, y in enumerate(bt):
            if i + j < max(a_terms, b_terms):
                d = jnp.dot(x, y, preferred_element_type=F32)
                out = d if out is None else out + d
    return out


def _sigmoid(x):
    return 1.0 / (1.0 + jnp.exp(-x))


def _full(a):
    nd = a.ndim
    return pl.BlockSpec(a.shape, lambda *_: (0,) * nd)


def _layer_norm(x, g, b):
    mu = jnp.mean(x, -1, keepdims=True)
    xc = x - mu
    var = jnp.mean(xc * xc, -1, keepdims=True)
    return xc * lax.rsqrt(var + LN_EPS) * g + b


def _inproj_kernel(x_ref, wrw_ref, wmla_ref, ws5_ref, wg_ref, zrw_ref, zmla_ref, zs5_ref, g_ref):
    xb = x_ref[...].astype(BF16)
    zrw_ref[...] = jnp.dot(xb, wrw_ref[...], preferred_element_type=F32)
    zmla_ref[...] = jnp.dot(xb, wmla_ref[...], preferred_element_type=F32)
    zs5_ref[...] = jnp.dot(xb, ws5_ref[...], preferred_element_type=F32)
    g_ref[...] = _sigmoid(jnp.dot(xb, wg_ref[...], preferred_element_type=F32)).astype(g_ref.dtype)


def _inproj(x, w_rw, w_mla, w_s5, w_g, tm):
    t = x.shape[0]
    row = lambda n: pl.BlockSpec((tm, n), lambda i: (i, 0))
    return pl.pallas_call(
        _inproj_kernel,
        grid=(t // tm,),
        in_specs=[row(D_MODEL), _full(w_rw), _full(w_mla), _full(w_s5), _full(w_g)],
        out_specs=[row(RW_IN), row(MLA_IN_PAD), row(S5_W), row(N_BRANCH * D_MODEL)],
        out_shape=[jax.ShapeDtypeStruct((t, RW_IN), F32), jax.ShapeDtypeStruct((t, MLA_IN_PAD), F32),
                   jax.ShapeDtypeStruct((t, S5_W), F32), jax.ShapeDtypeStruct((t, N_BRANCH * D_MODEL), BF16)],
        compiler_params=_params("parallel"),
        name="inproj",
    )(x, w_rw, w_mla, w_s5, w_g)


def _rwkv_kernel(nblk, tb, nb,
                 zf_ref, zfp_ref, zfn_ref, zb_ref, zbp_ref, zbn_ref,
                 mu_ref, w0_ref, w2_ref, a0_ref, a2_ref, g2_ref, kk_ref, ka_ref, rk_ref,
                 of_ref, bonus_ref, gate_ref, ob_ref, r_s, v_s, kn_s, lw_s, b_s, kd_s, state):
    step = pl.program_id(1)
    c = RW_CHUNK
    w = RW_W
    hr = lax.broadcasted_iota(jnp.int32, (w, w), 0) // RW_HEAD
    hc = lax.broadcasted_iota(jnp.int32, (w, w), 1) // RW_HEAD
    same_head = hr == hc
    head_ones = same_head.astype(F32)

    def prepare(d, bi, z_ref, zp_ref, zn_ref, blk):
        z = z_ref[bi]
        prev_row = jnp.where(blk == 0, 0.0, zp_ref[bi, 7:8, :])
        next_row = jnp.where(blk == nblk - 1, 0.0, zn_ref[bi, 0:1, :])
        rows = lax.broadcasted_iota(jnp.int32, z.shape, 0)
        z_prev = jnp.where(rows == 0, prev_row, pltpu.roll(z, 1, axis=0))
        z_next = jnp.where(rows == tb - 1, next_row, pltpu.roll(z, tb - 1, axis=0))
        z = z + mu_ref[...] * (0.5 * (z_prev + z_next) - z)
        r = z[:, 0:w]
        k = z[:, w:2 * w]
        v = z[:, 2 * w:3 * w]
        xw = z[:, 3 * w:3 * w + RW_LORA_W]
        xa = z[:, 3 * w + RW_LORA_W:3 * w + RW_LORA_W + RW_LORA_A]
        xg = z[:, 3 * w + RW_LORA_W + RW_LORA_A:]
        kk = k * kk_ref[...]
        kk_ss = _mm_split(kk * kk, head_ones, 2, 1)
        kk = kk * lax.rsqrt(jnp.maximum(kk_ss, 1e-12))
        y = w0_ref[d] + _mm_split(jnp.tanh(xw), w2_ref[d], 2, 2)
        lw = -math.exp(-0.5) * _sigmoid(y)
        a = _sigmoid(a0_ref[d] + _mm(xa, a2_ref[d]))
        r_s[d, bi] = r
        v_s[d, bi] = v
        kn_s[d, bi] = kk
        lw_s[d, bi] = lw
        b_s[d, bi] = kk * a
        kd_s[d, bi] = k * (1.0 + (a - 1.0) * ka_ref[...])
        if d == 0:
            rk = _mm_split(r * k * rk_ref[...], head_ones, 2, 1)
            bonus_ref[bi] = rk * v
            gate_ref[bi] = _mm(_sigmoid(xg), g2_ref[...])

    for bi in range(nb):
        prepare(0, bi, zf_ref, zfp_ref, zfn_ref, step)
        prepare(1, bi, zb_ref, zbp_ref, zbn_ref, nblk - 1 - step)

    @pl.when(step == 0)
    def _():
        state[...] = jnp.zeros_like(state)

    ti = lax.broadcasted_iota(jnp.int32, (c, c), 0)
    si = lax.broadcasted_iota(jnp.int32, (c, c), 1)
    tw = lax.broadcasted_iota(jnp.int32, (c, w), 0)
    sw = lax.broadcasted_iota(jnp.int32, (c, w), 1) % RW_HEAD
    eye = (tw == sw).astype(F32)

    def stack(x):
        return jnp.where(same_head, jnp.concatenate([x] * RW_HEADS, axis=0), 0.0).astype(BF16)

    nchunk = tb // c

    def chunk(reverse, bi, cpos):
        d = 1 if reverse else 0
        o_ref = ob_ref if reverse else of_ref
        cum_mat = ((si >= ti) if reverse else (si <= ti)).astype(F32)
        strict = (tw < sw) if reverse else (tw > sw)
        incl = (tw <= sw) if reverse else (tw >= sw)
        sl = pl.ds(pl.multiple_of(cpos * c, c), c)
        lwc = lw_s[d, bi, sl, :]
        l_in = _mm_split(cum_mat, lwc, 1, 3)
        l_tot = jnp.sum(lwc, axis=0, keepdims=True)
        e_in = jnp.exp(l_in)
        e_neg = jnp.exp(-l_in)
        e_tot = jnp.exp(l_tot)
        kap = kn_s[d, bi, sl, :] * jnp.exp(l_in - lwc)
        bt = b_s[d, bi, sl, :] * e_neg
        kt = kd_s[d, bi, sl, :] * e_neg
        rt = (r_s[d, bi, sl, :] * e_in).astype(BF16)
        v = v_s[d, bi, sl, :]
        kap_w, bt_w, kt_w, v_w = stack(kap), stack(bt), stack(kt), stack(v)
        kap = kap.astype(BF16)
        end_w = jnp.concatenate([bt * e_tot, kt * e_tot], axis=0).astype(BF16)
        s_bd = state[d, bi]
        s_w = s_bd.astype(BF16)
        yield
        kap_rt = jnp.concatenate([kap, rt], axis=0)
        am_b = _mm_nt(kap_rt, bt_w)
        a_b = jnp.where(strict, am_b[0:c], 0.0)
        m_b = jnp.where(incl, am_b[c:2 * c], 0.0)
        yield
        am_k = _mm_nt(kap_rt, kt_w)
        a_k = jnp.where(strict, am_k[0:c], 0.0)
        m_k = jnp.where(incl, am_k[c:2 * c], 0.0)
        yield
        assert c == 2 ** int(math.log2(c)) and int(math.log2(c)) % 2 == 0
        powers = [-a_b]
        pairs = []
        akv = _mm(a_k, v_w)
        for level in range(1, int(math.log2(c))):
            sq_w = stack(powers[-1])
            powers.append(_mm(powers[-1], sq_w))
            if level % 2 == 0:
                lo = eye + powers[level - 2]
                pairs.append(lo + _mm(lo, sq_w))
            yield
        lo = eye + powers[-2]
        pairs.append(lo + _mm(lo, stack(powers[-1])))
        tinv = pairs[0]
        for pr in pairs[1:-1]:
            tinv = _mm(tinv, stack(pr))
        yield
        tinv = _mm(tinv, stack(pairs[-1]))
        yield
        pm = _mm(tinv, kap_w)
        qm = _mm(tinv, stack(akv))
        yield
        u = -(_mm_nt(pm, s_w) + qm)
        yield
        o = _mm_nt(rt, s_w) + _mm(m_b, stack(u)) + _mm(m_k, v_w)
        upd = _mm_tn(jnp.concatenate([u, v], axis=0), end_w)
        yield
        o_ref[bi, sl, :] = o
        state[d, bi] = s_bd * e_tot + jnp.where(same_head, upd, 0.0)

    def all_scans(ci, carry):
        scans = [chunk(False, bi, ci) for bi in range(nb)] + [chunk(True, bi, nchunk - 1 - ci) for bi in range(nb)]
        while scans:
            scans = [g for g in scans if next(g, StopIteration) is not StopIteration]
        return carry

    lax.fori_loop(0, nchunk, all_scans, 0)


def _rwkv_scan(z, mu, w0, w2, a0, a2, g2, k_k, k_a, r_k, tb):
    bsz, seq, _ = z.shape
    nblk = seq // tb
    nb = 2 if bsz % 2 == 0 else 1
    z8 = z.reshape(bsz, seq // 8, 8, RW_IN)
    t8 = tb // 8

    def specs(pos):
        z_spec = pl.BlockSpec((nb, tb, RW_IN), lambda b, i: (b, pos(i), 0))
        zp_spec = pl.BlockSpec((nb, None, 8, RW_IN), lambda b, i: (b, jnp.maximum(pos(i) * t8 - 1, 0), 0, 0))
        zn_spec = pl.BlockSpec((nb, None, 8, RW_IN),
                               lambda b, i: (b, jnp.minimum((pos(i) + 1) * t8, seq // 8 - 1), 0, 0))
        o_spec = pl.BlockSpec((nb, tb, RW_W), lambda b, i: (b, pos(i), 0))
        return [z_spec, zp_spec, zn_spec], o_spec

    in_f, o_f = specs(lambda i: i)
    in_b, o_b = specs(lambda i: nblk - 1 - i)
    o_shape = jax.ShapeDtypeStruct((bsz, seq, RW_W), F32)
    weights = [mu, w0, w2, a0, a2, g2, k_k, k_a, r_k]
    return pl.pallas_call(
        functools.partial(_rwkv_kernel, nblk, tb, nb),
        grid=(bsz // nb, nblk),
        in_specs=in_f + in_b + [_full(wt) for wt in weights],
        out_specs=[o_f, o_f, o_f, o_b],
        out_shape=[o_shape] * 4,
        scratch_shapes=[pltpu.VMEM((2, nb, tb, RW_W), F32)] * 6 + [pltpu.VMEM((2, nb, RW_W, RW_W), F32)],
        compiler_params=_params("parallel", "arbitrary"),
        name="rwkv_scan",
    )(z, z8, z8, z, z8, z8, *weights)


def _rope_tables(seq):
    half = MLA_ROPE // 2
    inv = (ROPE_THETA ** (-np.arange(half, dtype=np.float32) / half)).astype(np.float32)
    ang = np.arange(seq, dtype=np.float32)[:, None] * inv[None, :]
    cos = np.cos(ang).astype(np.float32)
    sin = np.sin(ang).astype(np.float32)
    ct = np.zeros((seq, MLA_QK_PAD), np.float32)
    st = np.zeros((seq, MLA_QK_PAD), np.float32)
    ct[:, :MLA_NOPE] = 1.0
    ct[:, MLA_NOPE:MLA_NOPE + half] = cos
    ct[:, MLA_NOPE + half:MLA_NOPE + 2 * half] = cos
    st[:, MLA_NOPE:MLA_NOPE + half] = -sin
    st[:, MLA_NOPE + half:MLA_NOPE + 2 * half] = sin
    return jnp.asarray(ct), jnp.asarray(st), jnp.asarray(ct.T), jnp.asarray(st.T)


def _rope(x, cos_t, sin_t):
    half = MLA_ROPE // 2
    lane = lax.broadcasted_iota(jnp.int32, x.shape, 1)
    swapped = jnp.where(lane < MLA_NOPE + half,
                        pltpu.roll(x, MLA_QK_PAD - half, axis=1), pltpu.roll(x, half, axis=1))
    return x * cos_t + swapped * sin_t


def _rope_rows(x, cos_t, sin_t):
    half = MLA_ROPE // 2
    a, b = MLA_NOPE, MLA_NOPE + half
    swapped = jnp.concatenate([x[0:a], x[b:b + half], x[a:b], x[b + half:]], axis=0)
    return x * cos_t + swapped * sin_t


def _mla_qkv_kernel(z_ref, cos_ref, sin_ref, cosr_ref, sinr_ref, qn_ref, wuq_ref, kvn_ref, wuk_ref, wuv_ref,
                    q_ref, k_ref, v_ref):
    z = z_ref[...]
    cos_t = cos_ref[...]
    sin_t = sin_ref[...]
    cos_r = cosr_ref[...]
    sin_r = sinr_ref[...]
    c_q = z[:, :MLA_Q_LORA]
    c_kv = z[:, MLA_Q_LORA:MLA_Q_LORA + MLA_KV_LORA]
    k_rope = _rope(z[:, MLA_Q_LORA + MLA_KV_LORA:], cos_t, sin_t)
    c_q = (c_q * lax.rsqrt(jnp.mean(c_q * c_q, -1, keepdims=True) + RMS_EPS) * qn_ref[...]).astype(BF16)
    c_kv = (c_kv * lax.rsqrt(jnp.mean(c_kv * c_kv, -1, keepdims=True) + RMS_EPS) * kvn_ref[...]).astype(BF16)
    ones_row = (lax.broadcasted_iota(jnp.int32, (MLA_VT_ROWS - MLA_V, z.shape[0]), 0) == 0).astype(BF16)
    nt = (((1,), (1,)), ((), ()))
    for h in range(MLA_HEADS):
        q_t = lax.dot_general(wuq_ref[h], c_q, nt, preferred_element_type=F32)
        q_ref[h] = (_rope_rows(q_t, cos_r, sin_r) * (MLA_SCALE * LOG2_E)).astype(BF16)
        kh = jnp.dot(c_kv, wuk_ref[h], preferred_element_type=F32)
        k_ref[h] = (kh + k_rope).astype(BF16)
        v_t = lax.dot_general(wuv_ref[h], c_kv, nt, preferred_element_type=F32)
        v_ref[h, 0:MLA_V, :] = v_t.astype(BF16)
        v_ref[h, MLA_V:MLA_VT_ROWS, :] = ones_row


def _mla_qkv(z, rope, q_norm, w_uq, kv_norm, w_uk, w_uv, seq, tm):
    t = z.shape[0]
    nseq = seq // tm
    tab = pl.BlockSpec((tm, MLA_QK_PAD), lambda i: (i % nseq, 0))
    tab_r = pl.BlockSpec((MLA_QK_PAD, tm), lambda i: (0, i % nseq))
    cols = lambda n: pl.BlockSpec((MLA_HEADS, n, tm), lambda i: (0, 0, i))
    return pl.pallas_call(
        _mla_qkv_kernel,
        grid=(t // tm,),
        in_specs=[pl.BlockSpec((tm, MLA_IN_PAD), lambda i: (i, 0)), tab, tab, tab_r, tab_r,
                  _full(q_norm), _full(w_uq), _full(kv_norm), _full(w_uk), _full(w_uv)],
        out_specs=[cols(MLA_QK_PAD), pl.BlockSpec((MLA_HEADS, tm, MLA_QK_PAD), lambda i: (0, i, 0)),
                   cols(MLA_VT_ROWS)],
        out_shape=[jax.ShapeDtypeStruct((MLA_HEADS, MLA_QK_PAD, t), BF16),
                   jax.ShapeDtypeStruct((MLA_HEADS, t, MLA_QK_PAD), BF16),
                   jax.ShapeDtypeStruct((MLA_HEADS, MLA_VT_ROWS, t), BF16)],
        compiler_params=_params("parallel"),
        name="mla_qkv",
    )(z, *rope, q_norm, w_uq, kv_norm, w_uk, w_uv)


def _mla_attn_kernel(nk, q_ref, k_ref, v_ref, o_ref, m_s, acc_s, bm_s, pv_s):
    j = pl.program_id(2)

    def scores(h):
        return jnp.dot(k_ref[h], q_ref[h], preferred_element_type=F32)

    @pl.when(j == 0)
    def _():
        m_s[...] = jnp.full(m_s.shape, -jnp.inf, F32)
        acc_s[...] = jnp.zeros_like(acc_s)
        for h in range(MLA_HEADS):
            bm_s[h:h + 1, :] = jnp.max(scores(h), axis=0, keepdims=True)

    @pl.when(j > 0)
    def _():
        s_next = scores(0)
        for h in range(MLA_HEADS):
            s = s_next
            if h + 1 < MLA_HEADS:
                s_next = scores(h + 1)
            bm_s[h:h + 1, :] = jnp.max(s, axis=0, keepdims=True)
            p = jnp.exp2(s - m_s[h:h + 1, :]).astype(BF16)
            pv_s[h] = jnp.dot(v_ref[h], p, preferred_element_type=F32)

    stale_shift_ok = jnp.max(bm_s[...] - m_s[...]) < MLA_SHIFT_SLACK

    @pl.when(stale_shift_ok)
    def _():
        for h in range(MLA_HEADS):
            m_prev = m_s[h:h + 1, :]
            m_new = jnp.maximum(m_prev, bm_s[h:h + 1, :])
            acc_s[h] = (acc_s[h] + pv_s[h]) * jnp.exp2(m_prev - m_new)
            m_s[h:h + 1, :] = m_new

    @pl.when(jnp.logical_not(stale_shift_ok))
    def _():
        for h in range(MLA_HEADS):
            s = scores(h)
            m_prev = m_s[h:h + 1, :]
            m_new = jnp.maximum(m_prev, bm_s[h:h + 1, :])
            p = jnp.exp2(s - m_new).astype(BF16)
            acc_s[h] = jnp.exp2(m_prev - m_new) * acc_s[h] + jnp.dot(v_ref[h], p, preferred_element_type=F32)
            m_s[h:h + 1, :] = m_new

    @pl.when(j == nk - 1)
    def _():
        heads = []
        for h in range(MLA_HEADS):
            acc = acc_s[h]
            heads.append(acc[0:MLA_V] / acc[MLA_V:MLA_V + 1])
        o_ref[...] = jnp.concatenate(heads, axis=0).T


def _mla_attn(q, k, v, bsz, seq, tq, tk):
    nq = seq // tq
    nk = seq // tk
    t = bsz * seq
    return pl.pallas_call(
        functools.partial(_mla_attn_kernel, nk),
        grid=(bsz, nq, nk),
        in_specs=[pl.BlockSpec((MLA_HEADS, MLA_QK_PAD, tq), lambda b, i, j: (0, 0, b * nq + i)),
                  pl.BlockSpec((MLA_HEADS, tk, MLA_QK_PAD), lambda b, i, j: (0, b * nk + j, 0)),
                  pl.BlockSpec((MLA_HEADS, MLA_VT_ROWS, tk), lambda b, i, j: (0, 0, b * nk + j))],
        out_specs=pl.BlockSpec((tq, MLA_W), lambda b, i, j: (b * nq + i, 0)),
        out_shape=jax.ShapeDtypeStruct((t, MLA_W), F32),
        scratch_shapes=[pltpu.VMEM((MLA_HEADS, tq), F32), pltpu.VMEM((MLA_HEADS, MLA_VT_ROWS, tq), F32),
                        pltpu.VMEM((MLA_HEADS, tq), F32), pltpu.VMEM((MLA_HEADS, MLA_VT_ROWS, tq), F32)],
        compiler_params=_params("parallel", "parallel", "arbitrary"),
        name="mla_attn",
    )(q, k, v)


def _s5_disc_kernel(lre_ref, lim_ref, ldt_ref, bre_ref, bim_ref, are_ref, aim_ref, bbre_ref, bbim_ref):
    lam_re = lre_ref[...]
    lam_im = lim_ref[...]
    dt = jnp.exp(ldt_ref[...])
    mag = jnp.exp(lam_re * dt)
    ab_re = mag * jnp.cos(lam_im * dt)
    ab_im = mag * jnp.sin(lam_im * dt)
    den = lam_re * lam_re + lam_im * lam_im
    nr = ab_re - 1.0
    ni = ab_im
    coef_re = (nr * lam_re + ni * lam_im) / den
    coef_im = (ni * lam_re - nr * lam_im) / den
    are_ref[...] = ab_re
    aim_ref[...] = ab_im
    b_re = bre_ref[...]
    b_im = bim_ref[...]
    bbre_ref[...] = coef_re * b_re - coef_im * b_im
    bbim_ref[...] = coef_re * b_im + coef_im * b_re


def _s5_discretise(lam_re, lam_im, log_dt, b_re, b_im):
    shape = (2, S5_GROUPS, S5_STATE, S5_GROUP)
    flat = lambda a: jnp.broadcast_to(a, shape).reshape(-1, 128)
    out = jax.ShapeDtypeStruct((math.prod(shape) // 128, 128), F32)
    a_re, a_im, bb_re, bb_im = pl.pallas_call(_s5_disc_kernel, out_shape=[out] * 4, name="s5_disc")(
        flat(lam_re[..., None]), flat(lam_im[..., None]), flat(log_dt[..., None, None]), flat(b_re), flat(b_im))
    g2 = 2 * S5_GROUPS
    mat = lambda a: a.reshape(g2, S5_STATE, S5_GROUP)
    return mat(a_re)[..., 0], mat(a_im)[..., 0], mat(bb_re), mat(bb_im)


def _block_diag(blocks):
    g, m, n = blocks.shape
    eye = jnp.eye(g, dtype=blocks.dtype)
    return (eye[:, None, :, None] * blocks[:, :, None, :]).reshape(g * m, g * n)


def _s5_scan_kernel(tb, uf_ref, ub_ref, a_ref, bin_ref, cout_ref, yf_ref, yb_ref, xf_s, xb_s, carry_s):
    i = pl.program_id(0)
    n = S5_N
    rows = S5_BATCH

    @pl.when(i == 0)
    def _():
        carry_s[...] = jnp.zeros_like(carry_s)

    xf_s[...] = _mm(uf_ref[...], bin_ref[0])
    xb_s[...] = _mm(ub_ref[...], bin_ref[1])
    af_re = jnp.broadcast_to(a_ref[0:1, :], (rows, n))
    af_im = jnp.broadcast_to(a_ref[1:2, :], (rows, n))
    ab_re = jnp.broadcast_to(a_ref[2:3, :], (rows, n))
    ab_im = jnp.broadcast_to(a_ref[3:4, :], (rows, n))

    def step(t, carry):
        f_re, f_im, b_re, b_im = carry
        sf = pl.ds(pl.multiple_of(t * rows, rows), rows)
        sb = pl.ds(pl.multiple_of((tb - 1 - t) * rows, rows), rows)
        nf_re = af_re * f_re - af_im * f_im + xf_s[sf, 0:n]
        nf_im = af_re * f_im + af_im * f_re + xf_s[sf, n:2 * n]
        nb_re = ab_re * b_re - ab_im * b_im + xb_s[sb, 0:n]
        nb_im = ab_re * b_im + ab_im * b_re + xb_s[sb, n:2 * n]
        xf_s[sf, 0:n] = nf_re
        xf_s[sf, n:2 * n] = nf_im
        xb_s[sb, 0:n] = nb_re
        xb_s[sb, n:2 * n] = nb_im
        return nf_re, nf_im, nb_re, nb_im

    init = (carry_s[0], carry_s[1], carry_s[2], carry_s[3])
    f_re, f_im, b_re, b_im = lax.fori_loop(0, tb, step, init)
    carry_s[0] = f_re
    carry_s[1] = f_im
    carry_s[2] = b_re
    carry_s[3] = b_im
    yf_ref[...] = _mm(xf_s[...], cout_ref[0])
    yb_ref[...] = _mm(xb_s[...], cout_ref[1])


def _s5_scan(u_tm, a_vec, b_in, c_out, seq, tb):
    nblk = seq // tb
    rows = tb * S5_BATCH
    fwd = pl.BlockSpec((rows, S5_W), lambda i: (i, 0))
    bwd = pl.BlockSpec((rows, S5_W), lambda i: (nblk - 1 - i, 0))
    shape = jax.ShapeDtypeStruct((seq * S5_BATCH, S5_W), F32)
    return pl.pallas_call(
        functools.partial(_s5_scan_kernel, tb),
        grid=(nblk,),
        in_specs=[fwd, bwd, _full(a_vec), _full(b_in), _full(c_out)],
        out_specs=[fwd, bwd],
        out_shape=[shape, shape],
        scratch_shapes=[pltpu.VMEM((rows, 2 * S5_N), F32), pltpu.VMEM((rows, 2 * S5_N), F32),
                        pltpu.VMEM((4, S5_BATCH, S5_N), F32)],
        compiler_params=_params("arbitrary"),
        name="s5_scan",
    )(u_tm, u_tm, a_vec, b_in, c_out)


def _s5_post_kernel(yf_ref, yb_ref, u_ref, d_ref, w_ref, b_ref, o_ref):
    y = yf_ref[...] + yb_ref[...] + d_ref[...] * u_ref[...]
    y = jax.nn.gelu(y, approximate=True)
    o_ref[...] = y * _sigmoid(_mm(y, w_ref[...]) + b_ref[...])


def _s5_post(yf, yb, u_tm, d_skip, glu_w, glu_b, tm):
    t = u_tm.shape[0]
    row = pl.BlockSpec((tm, S5_W), lambda i: (i, 0))
    return pl.pallas_call(
        _s5_post_kernel,
        grid=(t // tm,),
        in_specs=[row, row, row, _full(d_skip), _full(glu_w), _full(glu_b)],
        out_specs=row,
        out_shape=jax.ShapeDtypeStruct((t, S5_W), F32),
        compiler_params=_params("parallel"),
        name="s5_post",
    )(yf, yb, u_tm, d_skip, glu_w, glu_b)


def _merge_kernel(x_ref, of_ref, ob_ref, bonus_ref, rgate_ref, mla_ref, s5_ref, g_ref,
                  gng_ref, gnb_ref, wrw_ref, wmla_ref, ws5_ref, wout_ref, lng_ref, lnb_ref, o_ref):
    w = RW_W
    hr = lax.broadcasted_iota(jnp.int32, (w, w), 0) // RW_HEAD
    hc = lax.broadcasted_iota(jnp.int32, (w, w), 1) // RW_HEAD
    head_mean = (hr == hc).astype(F32) * (1.0 / RW_HEAD)
    o = of_ref[...] + ob_ref[...]
    mean = _mm_split(o, head_mean, 2, 1)
    oc = o - mean
    var = _mm_split(oc * oc, head_mean, 2, 1)
    o = oc * lax.rsqrt(var + RW_GN_EPS) * gng_ref[...] + gnb_ref[...]
    y_rw = _mm((o + bonus_ref[...]) * rgate_ref[...], wrw_ref[...])
    y_mla = _mm(mla_ref[...], wmla_ref[...])
    y_s5 = _mm(s5_ref[...], ws5_ref[...])
    d = D_MODEL
    merged = g_ref[:, 0:d] * y_rw + g_ref[:, d:2 * d] * y_mla + g_ref[:, 2 * d:3 * d] * y_s5
    o_ref[...] = _layer_norm(DN_ALPHA * x_ref[...] + _mm(merged, wout_ref[...]), lng_ref[...], lnb_ref[...])


def _merge(x, o_f, o_b, bonus, rgate, o_mla, y_s5, gates, gn_g, gn_b, w_rw, w_mla, w_s5, w_out, ln_g, ln_b, tm):
    t = x.shape[0]
    row = lambda n: pl.BlockSpec((tm, n), lambda i: (i, 0))
    weights = [gn_g, gn_b, w_rw, w_mla, w_s5, w_out, ln_g, ln_b]
    return pl.pallas_call(
        _merge_kernel,
        grid=(t // tm,),
        in_specs=[row(D_MODEL), row(RW_W), row(RW_W), row(RW_W), row(RW_W), row(MLA_W), row(S5_W),
                  row(N_BRANCH * D_MODEL)] + [_full(wt) for wt in weights],
        out_specs=row(D_MODEL),
        out_shape=jax.ShapeDtypeStruct((t, D_MODEL), F32),
        compiler_params=_params("parallel"),
        name="merge_ln1",
    )(x, o_f, o_b, bonus, rgate, o_mla, y_s5, gates, *weights)


def _mlp_kernel(nf, x_ref, w1_ref, w2_ref, lng_ref, lnb_ref, o_ref, acc_s):
    j = pl.program_id(1)

    @pl.when(j == 0)
    def _():
        acc_s[...] = jnp.zeros_like(acc_s)

    h = jnp.maximum(_mm(x_ref[...], w1_ref[...]), 0.0)
    acc_s[...] += _mm(h * h, w2_ref[...])

    @pl.when(j == nf - 1)
    def _():
        o_ref[...] = _layer_norm(DN_ALPHA * x_ref[...] + acc_s[...], lng_ref[...], lnb_ref[...])


def _mlp(x, w1, w2, ln_g, ln_b, tm, tf):
    t = x.shape[0]
    nf = D_FF // tf
    return pl.pallas_call(
        functools.partial(_mlp_kernel, nf),
        grid=(t // tm, nf),
        in_specs=[pl.BlockSpec((tm, D_MODEL), lambda i, j: (i, 0)),
                  pl.BlockSpec((D_MODEL, tf), lambda i, j: (0, j)),
                  pl.BlockSpec((tf, D_MODEL), lambda i, j: (j, 0)),
                  _full(ln_g), _full(ln_b)],
        out_specs=pl.BlockSpec((tm, D_MODEL), lambda i, j: (i, 0)),
        out_shape=jax.ShapeDtypeStruct((t, D_MODEL), F32),
        scratch_shapes=[pltpu.VMEM((tm, D_MODEL), F32)],
        compiler_params=_params("parallel", "arbitrary"),
        name="mlp_ln2",
    )(x, w1, w2, ln_g, ln_b)


def _tile(n, pref):
    t = min(n, pref)
    assert n % t == 0, (n, pref)
    return t


def _prep_layer(w_in, rw_mu, rw_w0, rw_w2, rw_a0, rw_a2, rw_g2, rw_k_k, rw_k_a, rw_r_k, rw_gn_g, rw_gn_b, rw_proj,
                mla_q_norm, mla_w_uq, mla_kv_norm, mla_w_ukv, mla_proj,
                s5_lam_re, s5_lam_im, s5_log_dt, s5_b_re, s5_b_im, s5_c_re, s5_c_im, s5_d, s5_glu_w, s5_glu_b,
                s5_proj, w_out, ln1_g, ln1_b, mlp_w1, mlp_w2, ln2_g, ln2_b):
    row = lambda a: a.reshape(1, -1)
    p = {}
    p["w_rw"] = w_in[:, :OFF_MLA].astype(BF16)
    w_mla = w_in[:, OFF_MLA:OFF_S5]
    zeros = lambda n: jnp.zeros((D_MODEL, n), F32)
    p["w_mla"] = jnp.concatenate([w_mla[:, :MLA_Q_LORA + MLA_KV_LORA], zeros(MLA_NOPE),
                                  w_mla[:, MLA_Q_LORA + MLA_KV_LORA:],
                                  zeros(MLA_QK_PAD - MLA_NOPE - MLA_ROPE)], axis=1).astype(BF16)
    p["w_s5"] = w_in[:, OFF_S5:OFF_GATE].astype(BF16)
    p["w_gate"] = w_in[:, OFF_GATE:].astype(BF16)
    p["rw"] = [row(rw_mu)]
    p["rw_dir"] = [rw_w0[:, None, :], rw_w2, rw_a0[:, None, :], rw_a2]
    p["rw_shared"] = [rw_g2, row(rw_k_k), row(rw_k_a), row(rw_r_k)]
    p["rw_gn"] = [row(rw_gn_g), row(rw_gn_b)]
    p["rw_proj"] = rw_proj.astype(BF16)
    uq = mla_w_uq.reshape(MLA_Q_LORA, MLA_HEADS, MLA_NOPE + MLA_ROPE)
    uq = jnp.pad(uq, ((0, 0), (0, 0), (0, MLA_QK_PAD - MLA_NOPE - MLA_ROPE)))
    p["w_uq"] = uq.transpose(1, 2, 0).astype(BF16)
    ukv = mla_w_ukv.reshape(MLA_KV_LORA, MLA_HEADS, MLA_NOPE + MLA_V)
    uk = jnp.pad(ukv[:, :, :MLA_NOPE], ((0, 0), (0, 0), (0, MLA_QK_PAD - MLA_NOPE)))
    p["w_uk"] = uk.transpose(1, 0, 2).astype(BF16)
    p["w_uv"] = ukv[:, :, MLA_NOPE:].transpose(1, 2, 0).astype(BF16)
    p["q_norm"] = row(mla_q_norm)
    p["kv_norm"] = row(mla_kv_norm)
    p["mla_proj"] = mla_proj.astype(BF16)
    a_re, a_im, bb_re, bb_im = _s5_discretise(s5_lam_re, s5_lam_im, s5_log_dt, s5_b_re, s5_b_im)
    p["s5_a"] = jnp.stack([a_re[:S5_GROUPS].reshape(-1), a_im[:S5_GROUPS].reshape(-1),
                           a_re[S5_GROUPS:].reshape(-1), a_im[S5_GROUPS:].reshape(-1)])
    b_in, c_out = [], []
    for d in range(2):
        sl = slice(d * S5_GROUPS, (d + 1) * S5_GROUPS)
        b_in.append(jnp.concatenate([_block_diag(bb_re[sl].transpose(0, 2, 1)),
                                     _block_diag(bb_im[sl].transpose(0, 2, 1))], axis=1))
        c_out.append(jnp.concatenate([_block_diag(s5_c_re[d].transpose(0, 2, 1)),
                                      -_block_diag(s5_c_im[d].transpose(0, 2, 1))], axis=0))
    p["s5_b_in"] = jnp.stack(b_in).astype(BF16)
    p["s5_c_out"] = jnp.stack(c_out).astype(BF16)
    p["s5_post"] = [row(s5_d), s5_glu_w.astype(BF16), row(s5_glu_b)]
    p["s5_proj"] = s5_proj.astype(BF16)
    p["w_out"] = w_out.astype(BF16)
    p["ln1"] = [row(ln1_g), row(ln1_b)]
    p["w1"] = mlp_w1.astype(BF16)
    p["w2"] = mlp_w2.astype(BF16)
    p["ln2"] = [row(ln2_g), row(ln2_b)]
    return p


def _layer(x, p, bsz, seq, rope):
    t = bsz * seq
    tm = _tile(t, 512)
    z_rw, z_mla, z_s5, gates = _inproj(x, p["w_rw"], p["w_mla"], p["w_s5"], p["w_gate"], tm)

    tb = _tile(seq, 512)
    z3 = z_rw.reshape(bsz, seq, RW_IN)
    o_f, bonus, rgate, o_b = _rwkv_scan(z3, *p["rw"], *p["rw_dir"], *p["rw_shared"], tb=tb)
    flat = lambda a: a.reshape(t, RW_W)

    q, k, v = _mla_qkv(z_mla, rope, p["q_norm"], p["w_uq"], p["kv_norm"], p["w_uk"], p["w_uv"],
                       seq, _tile(seq, 512))
    o_mla = _mla_attn(q, k, v, bsz, seq, _tile(seq, 512), _tile(seq, 1024))

    u_tm = jnp.pad(z_s5.reshape(bsz, seq, S5_W).transpose(1, 0, 2), ((0, 0), (0, S5_BATCH - bsz), (0, 0)))
    u_tm = u_tm.reshape(seq * S5_BATCH, S5_W)
    y_f, y_b = _s5_scan(u_tm, p["s5_a"], p["s5_b_in"], p["s5_c_out"], seq, _tile(seq, 128))
    y_s5 = _s5_post(y_f, y_b, u_tm, *p["s5_post"], _tile(seq * S5_BATCH, 1024))
    y_s5 = y_s5.reshape(seq, S5_BATCH, S5_W)[:, :bsz].transpose(1, 0, 2).reshape(t, S5_W)

    x1 = _merge(x, flat(o_f), flat(o_b), flat(bonus), flat(rgate), o_mla, y_s5, gates,
                *p["rw_gn"], p["rw_proj"], p["mla_proj"], p["s5_proj"], p["w_out"], *p["ln1"], tm)
    return _mlp(x1, p["w1"], p["w2"], *p["ln2"], _tile(t, 1024), 1024)


def _trunk(x, layers):
    bsz, seq, _ = x.shape
    assert bsz <= S5_BATCH and seq % RW_CHUNK == 0
    rope = _rope_tables(seq)
    h = x.reshape(bsz * seq, D_MODEL)
    for p in layers:
        h = _layer(h, p, bsz, seq, rope)
    return h.reshape(bsz, seq, D_MODEL)


def kernel(x_prompt, x_sample, w_in, rw_mu, rw_w0, rw_w2, rw_a0, rw_a2, rw_g2, rw_k_k, rw_k_a, rw_r_k, rw_gn_g, rw_gn_b, rw_proj, mla_q_norm, mla_w_uq, mla_kv_norm, mla_w_ukv, mla_proj, s5_lam_re, s5_lam_im, s5_log_dt, s5_b_re, s5_b_im, s5_c_re, s5_c_im, s5_d, s5_glu_w, s5_glu_b, s5_proj, w_out, ln1_g, ln1_b, mlp_w1, mlp_w2, ln2_g, ln2_b):
    weights = (w_in, rw_mu, rw_w0, rw_w2, rw_a0, rw_a2, rw_g2, rw_k_k, rw_k_a, rw_r_k, rw_gn_g, rw_gn_b, rw_proj,
               mla_q_norm, mla_w_uq, mla_kv_norm, mla_w_ukv, mla_proj,
               s5_lam_re, s5_lam_im, s5_log_dt, s5_b_re, s5_b_im, s5_c_re, s5_c_im, s5_d, s5_glu_w, s5_glu_b,
               s5_proj, w_out, ln1_g, ln1_b, mlp_w1, mlp_w2, ln2_g, ln2_b)
    layers = [_prep_layer(*[wt[l] for wt in weights]) for l in range(w_in.shape[0])]
    return _trunk(x_prompt, layers), _trunk(x_sample, layers)
```

```python
import functools
import math

import numpy as np
import jax
import jax.numpy as jnp
from jax import lax
from jax.experimental import pallas as pl
from jax.experimental.pallas import tpu as pltpu

F32 = jnp.float32
BF16 = jnp.bfloat16
HIGHEST = lax.Precision.HIGHEST

D_MODEL = 1024
DEPTH = 2
RW_HEADS = 4
RW_HEAD = 64
RW_W = RW_HEADS * RW_HEAD
RW_LORA_W = 32
RW_LORA_A = 32
RW_LORA_G = 64
RW_GN_EPS = 64e-5
RW_IN = 3 * RW_W + RW_LORA_W + RW_LORA_A + RW_LORA_G
MLA_HEADS = 8
MLA_NOPE = 64
MLA_ROPE = 32
MLA_V = 64
MLA_Q_LORA = 256
MLA_KV_LORA = 128
MLA_W = MLA_HEADS * MLA_V
MLA_QK_PAD = 128
MLA_IN_PAD = MLA_Q_LORA + MLA_KV_LORA + MLA_QK_PAD
MLA_SCALE = (MLA_NOPE + MLA_ROPE) ** -0.5
LOG2_E = math.log2(math.e)
MLA_SHIFT_SLACK = 64.0
ROPE_THETA = 10000.0
RMS_EPS = 1e-6
S5_W = 256
S5_GROUP = 16
S5_GROUPS = S5_W // S5_GROUP
S5_STATE = 64
S5_N = S5_GROUPS * S5_STATE
S5_BATCH = 8
D_FF = 4 * D_MODEL
LN_EPS = 1e-5
N_BRANCH = 3
DN_ALPHA = (2 * DEPTH) ** 0.25
OFF_MLA = RW_IN
OFF_S5 = OFF_MLA + MLA_Q_LORA + MLA_KV_LORA + MLA_ROPE
OFF_GATE = OFF_S5 + S5_W
RW_CHUNK = 64
VMEM_LIMIT = 56 * 1024 * 1024


def _params(*sem):
    return pltpu.CompilerParams(dimension_semantics=sem, vmem_limit_bytes=VMEM_LIMIT)


def _mm(a, b):
    return jnp.dot(a.astype(BF16), b.astype(BF16), preferred_element_type=F32)


def _mm_nt(a, b):
    return lax.dot_general(a.astype(BF16), b.astype(BF16), (((1,), (1,)), ((), ())),
                           preferred_element_type=F32)


def _mm_tn(a, b):
    return lax.dot_general(a.astype(BF16), b.astype(BF16), (((0,), (0,)), ((), ())),
                           preferred_element_type=F32)


def _mm_f32(a, b):
    return jnp.dot(a, b, preferred_element_type=F32, precision=HIGHEST)


def _bf16_terms(x, n):
    terms = []
    for _ in range(n):
        t = x.astype(BF16)
        terms.append(t)
        x = x - t.astype(F32)
    return terms


def _mm_split(a, b, a_terms, b_terms):
    at = _bf16_terms(a, a_terms)
    bt = _bf16_terms(b, b_terms)
    out = None
    for i, x in enumerate(at):
        for j, y in enumerate(bt):
            if i + j < max(a_terms, b_terms):
                d = jnp.dot(x, y, preferred_element_type=F32)
                out = d if out is None else out + d
    return out


def _sigmoid(x):
    return 1.0 / (1.0 + jnp.exp(-x))


def _full(a):
    nd = a.ndim
    return pl.BlockSpec(a.shape, lambda *_: (0,) * nd)


def _layer_norm(x, g, b):
    mu = jnp.mean(x, -1, keepdims=True)
    xc = x - mu
    var = jnp.mean(xc * xc, -1, keepdims=True)
    return xc * lax.rsqrt(var + LN_EPS) * g + b


def _inproj_kernel(x_ref, wrw_ref, wmla_ref, ws5_ref, wg_ref, zrw_ref, zmla_ref, zs5_ref, g_ref):
    xb = x_ref[...].astype(BF16)
    zrw_ref[...] = jnp.dot(xb, wrw_ref[...], preferred_element_type=F32)
    zmla_ref[...] = jnp.dot(xb, wmla_ref[...], preferred_element_type=F32)
    zs5_ref[...] = jnp.dot(xb, ws5_ref[...], preferred_element_type=F32)
    g_ref[...] = _sigmoid(jnp.dot(xb, wg_ref[...], preferred_element_type=F32)).astype(g_ref.dtype)


def _inproj(x, w_rw, w_mla, w_s5, w_g, tm):
    t = x.shape[0]
    row = lambda n: pl.BlockSpec((tm, n), lambda i: (i, 0))
    return pl.pallas_call(
        _inproj_kernel,
        grid=(t // tm,),
        in_specs=[row(D_MODEL), _full(w_rw), _full(w_mla), _full(w_s5), _full(w_g)],
        out_specs=[row(RW_IN), row(MLA_IN_PAD), row(S5_W), row(N_BRANCH * D_MODEL)],
        out_shape=[jax.ShapeDtypeStruct((t, RW_IN), F32), jax.ShapeDtypeStruct((t, MLA_IN_PAD), F32),
                   jax.ShapeDtypeStruct((t, S5_W), F32), jax.ShapeDtypeStruct((t, N_BRANCH * D_MODEL), BF16)],
        compiler_params=_params("parallel"),
        name="inproj",
    )(x, w_rw, w_mla, w_s5, w_g)


def _rwkv_kernel(nblk, tb, nb,
                 zf_ref, zfp_ref, zfn_ref, zb_ref, zbp_ref, zbn_ref,
                 mu_ref, w0_ref, w2_ref, a0_ref, a2_ref, g2_ref, kk_ref, ka_ref, rk_ref,
                 of_ref, bonus_ref, gate_ref, ob_ref, r_s, v_s, kn_s, lw_s, b_s, kd_s, state):
    step = pl.program_id(1)
    c = RW_CHUNK
    w = RW_W
    hr = lax.broadcasted_iota(jnp.int32, (w, w), 0) // RW_HEAD
    hc = lax.broadcasted_iota(jnp.int32, (w, w), 1) // RW_HEAD
    same_head = hr == hc
    head_ones = same_head.astype(F32)

    def prepare(d, bi, z_ref, zp_ref, zn_ref, blk):
        z = z_ref[bi]
        prev_row = jnp.where(blk == 0, 0.0, zp_ref[bi, 7:8, :])
        next_row = jnp.where(blk == nblk - 1, 0.0, zn_ref[bi, 0:1, :])
        rows = lax.broadcasted_iota(jnp.int32, z.shape, 0)
        z_prev = jnp.where(rows == 0, prev_row, pltpu.roll(z, 1, axis=0))
        z_next = jnp.where(rows == tb - 1, next_row, pltpu.roll(z, tb - 1, axis=0))
        z = z + mu_ref[...] * (0.5 * (z_prev + z_next) - z)
        r = z[:, 0:w]
        k = z[:, w:2 * w]
        v = z[:, 2 * w:3 * w]
        xw = z[:, 3 * w:3 * w + RW_LORA_W]
        xa = z[:, 3 * w + RW_LORA_W:3 * w + RW_LORA_W + RW_LORA_A]
        xg = z[:, 3 * w + RW_LORA_W + RW_LORA_A:]
        kk = k * kk_ref[...]
        kk_ss = _mm_split(kk * kk, head_ones, 2, 1)
        kk = kk * lax.rsqrt(jnp.maximum(kk_ss, 1e-12))
        y = w0_ref[d] + _mm_split(jnp.tanh(xw), w2_ref[d], 2, 2)
        lw = -math.exp(-0.5) * _sigmoid(y)
        a = _sigmoid(a0_ref[d] + _mm(xa, a2_ref[d]))
        r_s[d, bi] = r
        v_s[d, bi] = v
        kn_s[d, bi] = kk
        lw_s[d, bi] = lw
        b_s[d, bi] = kk * a
        kd_s[d, bi] = k * (1.0 + (a - 1.0) * ka_ref[...])
        if d == 0:
            rk = _mm_split(r * k * rk_ref[...], head_ones, 2, 1)
            bonus_ref[bi] = rk * v
            gate_ref[bi] = _mm(_sigmoid(xg), g2_ref[...])

    for bi in range(nb):
        prepare(0, bi, zf_ref, zfp_ref, zfn_ref, step)
        prepare(1, bi, zb_ref, zbp_ref, zbn_ref, nblk - 1 - step)

    @pl.when(step == 0)
    def _():
        state[...] = jnp.zeros_like(state)

    ti = lax.broadcasted_iota(jnp.int32, (c, c), 0)
    si = lax.broadcasted_iota(jnp.int32, (c, c), 1)
    tw = lax.broadcasted_iota(jnp.int32, (c, w), 0)
    sw = lax.broadcasted_iota(jnp.int32, (c, w), 1) % RW_HEAD
    eye = (tw == sw).astype(F32)

    def stack(x):
        return jnp.where(same_head, jnp.concatenate([x] * RW_HEADS, axis=0), 0.0).astype(BF16)

    nchunk = tb // c

    def chunk(reverse, bi, cpos):
        d = 1 if reverse else 0
        o_ref = ob_ref if reverse else of_ref
        cum_mat = ((si >= ti) if reverse else (si <= ti)).astype(F32)
        strict = (tw < sw) if reverse else (tw > sw)
        incl = (tw <= sw) if reverse else (tw >= sw)
        sl = pl.ds(pl.multiple_of(cpos * c, c), c)
        lwc = lw_s[d, bi, sl, :]
        l_in = _mm_split(cum_mat, lwc, 1, 3)
        l_tot = jnp.sum(lwc, axis=0, keepdims=True)
        e_in = jnp.exp(l_in)
        e_neg = jnp.exp(-l_in)
        e_tot = jnp.exp(l_tot)
        kap = kn_s[d, bi, sl, :] * jnp.exp(l_in - lwc)
        bt = b_s[d, bi, sl, :] * e_neg
        kt = kd_s[d, bi, sl, :] * e_neg
        rt = (r_s[d, bi, sl, :] * e_in).astype(BF16)
        v = v_s[d, bi, sl, :]
        kap_w, bt_w, kt_w, v_w = stack(kap), stack(bt), stack(kt), stack(v)
        kap = kap.astype(BF16)
        end_w = jnp.concatenate([bt * e_tot, kt * e_tot], axis=0).astype(BF16)
        s_bd = state[d, bi]
        s_w = s_bd.astype(BF16)
        yield
        kap_rt = jnp.concatenate([kap, rt], axis=0)
        am_b = _mm_nt(kap_rt, bt_w)
        a_b = jnp.where(strict, am_b[0:c], 0.0)
        m_b = jnp.where(incl, am_b[c:2 * c], 0.0)
        yield
        am_k = _mm_nt(kap_rt, kt_w)
        a_k = jnp.where(strict, am_k[0:c], 0.0)
        m_k = jnp.where(incl, am_k[c:2 * c], 0.0)
        yield
        assert c == 2 ** int(math.log2(c)) and int(math.log2(c)) % 2 == 0
        powers = [-a_b]
        pairs = []
        akv = _mm(a_k, v_w)
        for level in range(1, int(math.log2(c))):
            sq_w = stack(powers[-1])
            powers.append(_mm(powers[-1], sq_w))
            if level % 2 == 0:
                lo = eye + powers[level - 2]
                pairs.append(lo + _mm(lo, sq_w))
            yield
        lo = eye + powers[-2]
        pairs.append(lo + _mm(lo, stack(powers[-1])))
        tinv = pairs[0]
        for pr in pairs[1:-1]:
            tinv = _mm(tinv, stack(pr))
        yield
        tinv = _mm(tinv, stack(pairs[-1]))
        yield
        pm = _mm(tinv, kap_w)
        qm = _mm(tinv, stack(akv))
        yield
        u = -(_mm_nt(pm, s_w) + qm)
        yield
        o = _mm_nt(rt, s_w) + _mm(m_b, stack(u)) + _mm(m_k, v_w)
        upd = _mm_tn(jnp.concatenate([u, v], axis=0), end_w)
        yield
        o_ref[bi, sl, :] = o
        state[d, bi] = s_bd * e_tot + jnp.where(same_head, upd, 0.0)

    def all_scans(ci, carry):
        scans = [chunk(False, bi, ci) for bi in range(nb)] + [chunk(True, bi, nchunk - 1 - ci) for bi in range(nb)]
        while scans:
            scans = [g for g in scans if next(g, StopIteration) is not StopIteration]
        return carry

    lax.fori_loop(0, nchunk, all_scans, 0)


def _rwkv_scan(z, mu, w0, w2, a0, a2, g2, k_k, k_a, r_k, tb):
    bsz, seq, _ = z.shape
    nblk = seq // tb
    nb = 2 if bsz % 2 == 0 else 1
    z8 = z.reshape(bsz, seq // 8, 8, RW_IN)
    t8 = tb // 8

    def specs(pos):
        z_spec = pl.BlockSpec((nb, tb, RW_IN), lambda b, i: (b, pos(i), 0))
        zp_spec = pl.BlockSpec((nb, None, 8, RW_IN), lambda b, i: (b, jnp.maximum(pos(i) * t8 - 1, 0), 0, 0))
        zn_spec = pl.BlockSpec((nb, None, 8, RW_IN),
                               lambda b, i: (b, jnp.minimum((pos(i) + 1) * t8, seq // 8 - 1), 0, 0))
        o_spec = pl.BlockSpec((nb, tb, RW_W), lambda b, i: (b, pos(i), 0))
        return [z_spec, zp_spec, zn_spec], o_spec

    in_f, o_f = specs(lambda i: i)
    in_b, o_b = specs(lambda i: nblk - 1 - i)
    o_shape = jax.ShapeDtypeStruct((bsz, seq, RW_W), F32)
    weights = [mu, w0, w2, a0, a2, g2, k_k, k_a, r_k]
    return pl.pallas_call(
        functools.partial(_rwkv_kernel, nblk, tb, nb),
        grid=(bsz // nb, nblk),
        in_specs=in_f + in_b + [_full(wt) for wt in weights],
        out_specs=[o_f, o_f, o_f, o_b],
        out_shape=[o_shape] * 4,
        scratch_shapes=[pltpu.VMEM((2, nb, tb, RW_W), F32)] * 6 + [pltpu.VMEM((2, nb, RW_W, RW_W), F32)],
        compiler_params=_params("parallel", "arbitrary"),
        name="rwkv_scan",
    )(z, z8, z8, z, z8, z8, *weights)


def _rope_tables(seq):
    half = MLA_ROPE // 2
    inv = (ROPE_THETA ** (-np.arange(half, dtype=np.float32) / half)).astype(np.float32)
    ang = np.arange(seq, dtype=np.float32)[:, None] * inv[None, :]
    cos = np.cos(ang).astype(np.float32)
    sin = np.sin(ang).astype(np.float32)
    ct = np.zeros((seq, MLA_QK_PAD), np.float32)
    st = np.zeros((seq, MLA_QK_PAD), np.float32)
    ct[:, :MLA_NOPE] = 1.0
    ct[:, MLA_NOPE:MLA_NOPE + half] = cos
    ct[:, MLA_NOPE + half:MLA_NOPE + 2 * half] = cos
    st[:, MLA_NOPE:MLA_NOPE + half] = -sin
    st[:, MLA_NOPE + half:MLA_NOPE + 2 * half] = sin
    return jnp.asarray(ct), jnp.asarray(st), jnp.asarray(ct.T), jnp.asarray(st.T)


def _rope(x, cos_t, sin_t):
    half = MLA_ROPE // 2
    lane = lax.broadcasted_iota(jnp.int32, x.shape, 1)
    swapped = jnp.where(lane < MLA_NOPE + half,
                        pltpu.roll(x, MLA_QK_PAD - half, axis=1), pltpu.roll(x, half, axis=1))
    return x * cos_t + swapped * sin_t


def _rope_rows(x, cos_t, sin_t):
    half = MLA_ROPE // 2
    a, b = MLA_NOPE, MLA_NOPE + half
    swapped = jnp.concatenate([x[0:a], x[b:b + half], x[a:b], x[b + half:]], axis=0)
    return x * cos_t + swapped * sin_t


def _mla_qkv_kernel(z_ref, cos_ref, sin_ref, cosr_ref, sinr_ref, qn_ref, wuq_ref, kvn_ref, wuk_ref, wuv_ref,
                    q_ref, k_ref, v_ref):
    z = z_ref[...]
    cos_t = cos_ref[...]
    sin_t = sin_ref[...]
    cos_r = cosr_ref[...]
    sin_r = sinr_ref[...]
    c_q = z[:, :MLA_Q_LORA]
    c_kv = z[:, MLA_Q_LORA:MLA_Q_LORA + MLA_KV_LORA]
    k_rope = _rope(z[:, MLA_Q_LORA + MLA_KV_LORA:], cos_t, sin_t)
    c_q = (c_q * lax.rsqrt(jnp.mean(c_q * c_q, -1, keepdims=True) + RMS_EPS) * qn_ref[...]).astype(BF16)
    c_kv = (c_kv * lax.rsqrt(jnp.mean(c_kv * c_kv, -1, keepdims=True) + RMS_EPS) * kvn_ref[...]).astype(BF16)
    nt = (((1,), (1,)), ((), ()))
    for h in range(MLA_HEADS):
        q_t = lax.dot_general(wuq_ref[h], c_q, nt, preferred_element_type=F32)
        q_ref[h] = (_rope_rows(q_t, cos_r, sin_r) * (MLA_SCALE * LOG2_E)).astype(BF16)
        kh = jnp.dot(c_kv, wuk_ref[h], preferred_element_type=F32)
        k_ref[h] = (kh + k_rope).astype(BF16)
        v_t = lax.dot_general(wuv_ref[h], c_kv, nt, preferred_element_type=F32)
        v_ref[h] = v_t.astype(BF16)


def _mla_qkv(z, rope, q_norm, w_uq, kv_norm, w_uk, w_uv, seq, tm):
    t = z.shape[0]
    nseq = seq // tm
    tab = pl.BlockSpec((tm, MLA_QK_PAD), lambda i: (i % nseq, 0))
    tab_r = pl.BlockSpec((MLA_QK_PAD, tm), lambda i: (0, i % nseq))
    cols = lambda n: pl.BlockSpec((MLA_HEADS, n, tm), lambda i: (0, 0, i))
    return pl.pallas_call(
        _mla_qkv_kernel,
        grid=(t // tm,),
        in_specs=[pl.BlockSpec((tm, MLA_IN_PAD), lambda i: (i, 0)), tab, tab, tab_r, tab_r,
                  _full(q_norm), _full(w_uq), _full(kv_norm), _full(w_uk), _full(w_uv)],
        out_specs=[cols(MLA_QK_PAD), pl.BlockSpec((MLA_HEADS, tm, MLA_QK_PAD), lambda i: (0, i, 0)),
                   cols(MLA_V)],
        out_shape=[jax.ShapeDtypeStruct((MLA_HEADS, MLA_QK_PAD, t), BF16),
                   jax.ShapeDtypeStruct((MLA_HEADS, t, MLA_QK_PAD), BF16),
                   jax.ShapeDtypeStruct((MLA_HEADS, MLA_V, t), BF16)],
        compiler_params=_params("parallel"),
        name="mla_qkv",
    )(z, *rope, q_norm, w_uq, kv_norm, w_uk, w_uv)


def _mla_attn_kernel(nk, q_ref, k_ref, v_ref, o_ref, m_s, l_s, acc_s, bm_s, bl_s, pv_s):
    j = pl.program_id(2)

    def scores(h):
        return jnp.dot(k_ref[h], q_ref[h], preferred_element_type=F32)

    @pl.when(j == 0)
    def _():
        m_s[...] = jnp.full(m_s.shape, -jnp.inf, F32)
        l_s[...] = jnp.zeros_like(l_s)
        acc_s[...] = jnp.zeros_like(acc_s)
        for h in range(MLA_HEADS):
            bm_s[h:h + 1, :] = jnp.max(scores(h), axis=0, keepdims=True)

    @pl.when(j > 0)
    def _():
        s_next = scores(0)
        for h in range(MLA_HEADS):
            s = s_next
            if h + 1 < MLA_HEADS:
                s_next = scores(h + 1)
            bm_s[h:h + 1, :] = jnp.max(s, axis=0, keepdims=True)
            p = jnp.exp2(s - m_s[h:h + 1, :])
            bl_s[h:h + 1, :] = jnp.sum(p, axis=0, keepdims=True)
            pv_s[h] = jnp.dot(v_ref[h], p.astype(BF16), preferred_element_type=F32)

    stale_shift_ok = jnp.max(bm_s[...] - m_s[...]) < MLA_SHIFT_SLACK

    @pl.when(stale_shift_ok)
    def _():
        for h in range(MLA_HEADS):
            m_prev = m_s[h:h + 1, :]
            m_new = jnp.maximum(m_prev, bm_s[h:h + 1, :])
            alpha = jnp.exp2(m_prev - m_new)
            acc_s[h] = (acc_s[h] + pv_s[h]) * alpha
            l_s[h:h + 1, :] = (l_s[h:h + 1, :] + bl_s[h:h + 1, :]) * alpha
            m_s[h:h + 1, :] = m_new

    @pl.when(jnp.logical_not(stale_shift_ok))
    def _():
        for h in range(MLA_HEADS):
            s = scores(h)
            m_prev = m_s[h:h + 1, :]
            m_new = jnp.maximum(m_prev, bm_s[h:h + 1, :])
            alpha = jnp.exp2(m_prev - m_new)
            p = jnp.exp2(s - m_new)
            l_s[h:h + 1, :] = alpha * l_s[h:h + 1, :] + jnp.sum(p, axis=0, keepdims=True)
            acc_s[h] = alpha * acc_s[h] + jnp.dot(v_ref[h], p.astype(BF16), preferred_element_type=F32)
            m_s[h:h + 1, :] = m_new

    @pl.when(j == nk - 1)
    def _():
        heads = [acc_s[h] / l_s[h:h + 1, :] for h in range(MLA_HEADS)]
        o_ref[...] = jnp.concatenate(heads, axis=0).T


def _mla_attn(q, k, v, bsz, seq, tq, tk):
    nq = seq // tq
    nk = seq // tk
    t = bsz * seq
    return pl.pallas_call(
        functools.partial(_mla_attn_kernel, nk),
        grid=(bsz, nq, nk),
        in_specs=[pl.BlockSpec((MLA_HEADS, MLA_QK_PAD, tq), lambda b, i, j: (0, 0, b * nq + i)),
                  pl.BlockSpec((MLA_HEADS, tk, MLA_QK_PAD), lambda b, i, j: (0, b * nk + j, 0)),
                  pl.BlockSpec((MLA_HEADS, MLA_V, tk), lambda b, i, j: (0, 0, b * nk + j))],
        out_specs=pl.BlockSpec((tq, MLA_W), lambda b, i, j: (b * nq + i, 0)),
        out_shape=jax.ShapeDtypeStruct((t, MLA_W), F32),
        scratch_shapes=[pltpu.VMEM((MLA_HEADS, tq), F32), pltpu.VMEM((MLA_HEADS, tq), F32),
                        pltpu.VMEM((MLA_HEADS, MLA_V, tq), F32),
                        pltpu.VMEM((MLA_HEADS, tq), F32), pltpu.VMEM((MLA_HEADS, tq), F32),
                        pltpu.VMEM((MLA_HEADS, MLA_V, tq), F32)],
        compiler_params=_params("parallel", "parallel", "arbitrary"),
        name="mla_attn",
    )(q, k, v)


def _s5_disc_kernel(lre_ref, lim_ref, ldt_ref, bre_ref, bim_ref, are_ref, aim_ref, bbre_ref, bbim_ref):
    lam_re = lre_ref[...]
    lam_im = lim_ref[...]
    dt = jnp.exp(ldt_ref[...])
    mag = jnp.exp(lam_re * dt)
    ab_re = mag * jnp.cos(lam_im * dt)
    ab_im = mag * jnp.sin(lam_im * dt)
    den = lam_re * lam_re + lam_im * lam_im
    nr = ab_re - 1.0
    ni = ab_im
    coef_re = (nr * lam_re + ni * lam_im) / den
    coef_im = (ni * lam_re - nr * lam_im) / den
    are_ref[...] = ab_re
    aim_ref[...] = ab_im
    b_re = bre_ref[...]
    b_im = bim_ref[...]
    bbre_ref[...] = coef_re * b_re - coef_im * b_im
    bbim_ref[...] = coef_re * b_im + coef_im * b_re


def _s5_discretise(lam_re, lam_im, log_dt, b_re, b_im):
    shape = (2, S5_GROUPS, S5_STATE, S5_GROUP)
    flat = lambda a: jnp.broadcast_to(a, shape).reshape(-1, 128)
    out = jax.ShapeDtypeStruct((math.prod(shape) // 128, 128), F32)
    a_re, a_im, bb_re, bb_im = pl.pallas_call(_s5_disc_kernel, out_shape=[out] * 4, name="s5_disc")(
        flat(lam_re[..., None]), flat(lam_im[..., None]), flat(log_dt[..., None, None]), flat(b_re), flat(b_im))
    g2 = 2 * S5_GROUPS
    mat = lambda a: a.reshape(g2, S5_STATE, S5_GROUP)
    return mat(a_re)[..., 0], mat(a_im)[..., 0], mat(bb_re), mat(bb_im)


def _block_diag(blocks):
    g, m, n = blocks.shape
    eye = jnp.eye(g, dtype=blocks.dtype)
    return (eye[:, None, :, None] * blocks[:, :, None, :]).reshape(g * m, g * n)


def _s5_scan_kernel(tb, uf_ref, ub_ref, a_ref, bin_ref, cout_ref, yf_ref, yb_ref, xf_s, xb_s, carry_s):
    i = pl.program_id(0)
    n = S5_N
    rows = S5_BATCH

    @pl.when(i == 0)
    def _():
        carry_s[...] = jnp.zeros_like(carry_s)

    xf_s[...] = _mm(uf_ref[...], bin_ref[0])
    xb_s[...] = _mm(ub_ref[...], bin_ref[1])
    af_re = jnp.broadcast_to(a_ref[0:1, :], (rows, n))
    af_im = jnp.broadcast_to(a_ref[1:2, :], (rows, n))
    ab_re = jnp.broadcast_to(a_ref[2:3, :], (rows, n))
    ab_im = jnp.broadcast_to(a_ref[3:4, :], (rows, n))

    def step(t, carry):
        f_re, f_im, b_re, b_im = carry
        sf = pl.ds(pl.multiple_of(t * rows, rows), rows)
        sb = pl.ds(pl.multiple_of((tb - 1 - t) * rows, rows), rows)
        nf_re = af_re * f_re - af_im * f_im + xf_s[sf, 0:n]
        nf_im = af_re * f_im + af_im * f_re + xf_s[sf, n:2 * n]
        nb_re = ab_re * b_re - ab_im * b_im + xb_s[sb, 0:n]
        nb_im = ab_re * b_im + ab_im * b_re + xb_s[sb, n:2 * n]
        xf_s[sf, 0:n] = nf_re
        xf_s[sf, n:2 * n] = nf_im
        xb_s[sb, 0:n] = nb_re
        xb_s[sb, n:2 * n] = nb_im
        return nf_re, nf_im, nb_re, nb_im

    init = (carry_s[0], carry_s[1], carry_s[2], carry_s[3])
    f_re, f_im, b_re, b_im = lax.fori_loop(0, tb, step, init)
    carry_s[0] = f_re
    carry_s[1] = f_im
    carry_s[2] = b_re
    carry_s[3] = b_im
    yf_ref[...] = _mm(xf_s[...], cout_ref[0])
    yb_ref[...] = _mm(xb_s[...], cout_ref[1])


def _s5_scan(u_tm, a_vec, b_in, c_out, seq, tb):
    nblk = seq // tb
    rows = tb * S5_BATCH
    fwd = pl.BlockSpec((rows, S5_W), lambda i: (i, 0))
    bwd = pl.BlockSpec((rows, S5_W), lambda i: (nblk - 1 - i, 0))
    shape = jax.ShapeDtypeStruct((seq * S5_BATCH, S5_W), F32)
    return pl.pallas_call(
        functools.partial(_s5_scan_kernel, tb),
        grid=(nblk,),
        in_specs=[fwd, bwd, _full(a_vec), _full(b_in), _full(c_out)],
        out_specs=[fwd, bwd],
        out_shape=[shape, shape],
        scratch_shapes=[pltpu.VMEM((rows, 2 * S5_N), F32), pltpu.VMEM((rows, 2 * S5_N), F32),
                        pltpu.VMEM((4, S5_BATCH, S5_N), F32)],
        compiler_params=_params("arbitrary"),
        name="s5_scan",
    )(u_tm, u_tm, a_vec, b_in, c_out)


def _s5_post_kernel(yf_ref, yb_ref, u_ref, d_ref, w_ref, b_ref, o_ref):
    y = yf_ref[...] + yb_ref[...] + d_ref[...] * u_ref[...]
    y = jax.nn.gelu(y, approximate=True)
    o_ref[...] = y * _sigmoid(_mm(y, w_ref[...]) + b_ref[...])


def _s5_post(yf, yb, u_tm, d_skip, glu_w, glu_b, tm):
    t = u_tm.shape[0]
    row = pl.BlockSpec((tm, S5_W), lambda i: (i, 0))
    return pl.pallas_call(
        _s5_post_kernel,
        grid=(t // tm,),
        in_specs=[row, row, row, _full(d_skip), _full(glu_w), _full(glu_b)],
        out_specs=row,
        out_shape=jax.ShapeDtypeStruct((t, S5_W), F32),
        compiler_params=_params("parallel"),
        name="s5_post",
    )(yf, yb, u_tm, d_skip, glu_w, glu_b)


def _merge_kernel(x_ref, of_ref, ob_ref, bonus_ref, rgate_ref, mla_ref, s5_ref, g_ref,
                  gng_ref, gnb_ref, wrw_ref, wmla_ref, ws5_ref, wout_ref, lng_ref, lnb_ref, o_ref):
    w = RW_W
    hr = lax.broadcasted_iota(jnp.int32, (w, w), 0) // RW_HEAD
    hc = lax.broadcasted_iota(jnp.int32, (w, w), 1) // RW_HEAD
    head_mean = (hr == hc).astype(F32) * (1.0 / RW_HEAD)
    o = of_ref[...] + ob_ref[...]
    mean = _mm_split(o, head_mean, 2, 1)
    oc = o - mean
    var = _mm_split(oc * oc, head_mean, 2, 1)
    o = oc * lax.rsqrt(var + RW_GN_EPS) * gng_ref[...] + gnb_ref[...]
    y_rw = _mm((o + bonus_ref[...]) * rgate_ref[...], wrw_ref[...])
    y_mla = _mm(mla_ref[...], wmla_ref[...])
    y_s5 = _mm(s5_ref[...], ws5_ref[...])
    d = D_MODEL
    merged = g_ref[:, 0:d] * y_rw + g_ref[:, d:2 * d] * y_mla + g_ref[:, 2 * d:3 * d] * y_s5
    o_ref[...] = _layer_norm(DN_ALPHA * x_ref[...] + _mm(merged, wout_ref[...]), lng_ref[...], lnb_ref[...])


def _merge(x, o_f, o_b, bonus, rgate, o_mla, y_s5, gates, gn_g, gn_b, w_rw, w_mla, w_s5, w_out, ln_g, ln_b, tm):
    t = x.shape[0]
    row = lambda n: pl.BlockSpec((tm, n), lambda i: (i, 0))
    weights = [gn_g, gn_b, w_rw, w_mla, w_s5, w_out, ln_g, ln_b]
    return pl.pallas_call(
        _merge_kernel,
        grid=(t // tm,),
        in_specs=[row(D_MODEL), row(RW_W), row(RW_W), row(RW_W), row(RW_W), row(MLA_W), row(S5_W),
                  row(N_BRANCH * D_MODEL)] + [_full(wt) for wt in weights],
        out_specs=row(D_MODEL),
        out_shape=jax.ShapeDtypeStruct((t, D_MODEL), F32),
        compiler_params=_params("parallel"),
        name="merge_ln1",
    )(x, o_f, o_b, bonus, rgate, o_mla, y_s5, gates, *weights)


def _mlp_kernel(nf, x_ref, w1_ref, w2_ref, lng_ref, lnb_ref, o_ref, acc_s):
    j = pl.program_id(1)

    @pl.when(j == 0)
    def _():
        acc_s[...] = jnp.zeros_like(acc_s)

    h = jnp.maximum(_mm(x_ref[...], w1_ref[...]), 0.0)
    acc_s[...] += _mm(h * h, w2_ref[...])

    @pl.when(j == nf - 1)
    def _():
        o_ref[...] = _layer_norm(DN_ALPHA * x_ref[...] + acc_s[...], lng_ref[...], lnb_ref[...])


def _mlp(x, w1, w2, ln_g, ln_b, tm, tf):
    t = x.shape[0]
    nf = D_FF // tf
    return pl.pallas_call(
        functools.partial(_mlp_kernel, nf),
        grid=(t // tm, nf),
        in_specs=[pl.BlockSpec((tm, D_MODEL), lambda i, j: (i, 0)),
                  pl.BlockSpec((D_MODEL, tf), lambda i, j: (0, j)),
                  pl.BlockSpec((tf, D_MODEL), lambda i, j: (j, 0)),
                  _full(ln_g), _full(ln_b)],
        out_specs=pl.BlockSpec((tm, D_MODEL), lambda i, j: (i, 0)),
        out_shape=jax.ShapeDtypeStruct((t, D_MODEL), F32),
        scratch_shapes=[pltpu.VMEM((tm, D_MODEL), F32)],
        compiler_params=_params("parallel", "arbitrary"),
        name="mlp_ln2",
    )(x, w1, w2, ln_g, ln_b)


def _tile(n, pref):
    t = min(n, pref)
    assert n % t == 0, (n, pref)
    return t


def _prep_layer(w_in, rw_mu, rw_w0, rw_w2, rw_a0, rw_a2, rw_g2, rw_k_k, rw_k_a, rw_r_k, rw_gn_g, rw_gn_b, rw_proj,
                mla_q_norm, mla_w_uq, mla_kv_norm, mla_w_ukv, mla_proj,
                s5_lam_re, s5_lam_im, s5_log_dt, s5_b_re, s5_b_im, s5_c_re, s5_c_im, s5_d, s5_glu_w, s5_glu_b,
                s5_proj, w_out, ln1_g, ln1_b, mlp_w1, mlp_w2, ln2_g, ln2_b):
    row = lambda a: a.reshape(1, -1)
    p = {}
    p["w_rw"] = w_in[:, :OFF_MLA].astype(BF16)
    w_mla = w_in[:, OFF_MLA:OFF_S5]
    zeros = lambda n: jnp.zeros((D_MODEL, n), F32)
    p["w_mla"] = jnp.concatenate([w_mla[:, :MLA_Q_LORA + MLA_KV_LORA], zeros(MLA_NOPE),
                                  w_mla[:, MLA_Q_LORA + MLA_KV_LORA:],
                                  zeros(MLA_QK_PAD - MLA_NOPE - MLA_ROPE)], axis=1).astype(BF16)
    p["w_s5"] = w_in[:, OFF_S5:OFF_GATE].astype(BF16)
    p["w_gate"] = w_in[:, OFF_GATE:].astype(BF16)
    p["rw"] = [row(rw_mu)]
    p["rw_dir"] = [rw_w0[:, None, :], rw_w2, rw_a0[:, None, :], rw_a2]
    p["rw_shared"] = [rw_g2, row(rw_k_k), row(rw_k_a), row(rw_r_k)]
    p["rw_gn"] = [row(rw_gn_g), row(rw_gn_b)]
    p["rw_proj"] = rw_proj.astype(BF16)
    uq = mla_w_uq.reshape(MLA_Q_LORA, MLA_HEADS, MLA_NOPE + MLA_ROPE)
    uq = jnp.pad(uq, ((0, 0), (0, 0), (0, MLA_QK_PAD - MLA_NOPE - MLA_ROPE)))
    p["w_uq"] = uq.transpose(1, 2, 0).astype(BF16)
    ukv = mla_w_ukv.reshape(MLA_KV_LORA, MLA_HEADS, MLA_NOPE + MLA_V)
    uk = jnp.pad(ukv[:, :, :MLA_NOPE], ((0, 0), (0, 0), (0, MLA_QK_PAD - MLA_NOPE)))
    p["w_uk"] = uk.transpose(1, 0, 2).astype(BF16)
    p["w_uv"] = ukv[:, :, MLA_NOPE:].transpose(1, 2, 0).astype(BF16)
    p["q_norm"] = row(mla_q_norm)
    p["kv_norm"] = row(mla_kv_norm)
    p["mla_proj"] = mla_proj.astype(BF16)
    a_re, a_im, bb_re, bb_im = _s5_discretise(s5_lam_re, s5_lam_im, s5_log_dt, s5_b_re, s5_b_im)
    p["s5_a"] = jnp.stack([a_re[:S5_GROUPS].reshape(-1), a_im[:S5_GROUPS].reshape(-1),
                           a_re[S5_GROUPS:].reshape(-1), a_im[S5_GROUPS:].reshape(-1)])
    b_in, c_out = [], []
    for d in range(2):
        sl = slice(d * S5_GROUPS, (d + 1) * S5_GROUPS)
        b_in.append(jnp.concatenate([_block_diag(bb_re[sl].transpose(0, 2, 1)),
                                     _block_diag(bb_im[sl].transpose(0, 2, 1))], axis=1))
        c_out.append(jnp.concatenate([_block_diag(s5_c_re[d].transpose(0, 2, 1)),
                                      -_block_diag(s5_c_im[d].transpose(0, 2, 1))], axis=0))
    p["s5_b_in"] = jnp.stack(b_in).astype(BF16)
    p["s5_c_out"] = jnp.stack(c_out).astype(BF16)
    p["s5_post"] = [row(s5_d), s5_glu_w.astype(BF16), row(s5_glu_b)]
    p["s5_proj"] = s5_proj.astype(BF16)
    p["w_out"] = w_out.astype(BF16)
    p["ln1"] = [row(ln1_g), row(ln1_b)]
    p["w1"] = mlp_w1.astype(BF16)
    p["w2"] = mlp_w2.astype(BF16)
    p["ln2"] = [row(ln2_g), row(ln2_b)]
    return p


def _layer(x, p, bsz, seq, rope):
    t = bsz * seq
    tm = _tile(t, 512)
    z_rw, z_mla, z_s5, gates = _inproj(x, p["w_rw"], p["w_mla"], p["w_s5"], p["w_gate"], _tile(t, 256))

    tb = _tile(seq, 512)
    z3 = z_rw.reshape(bsz, seq, RW_IN)
    o_f, bonus, rgate, o_b = _rwkv_scan(z3, *p["rw"], *p["rw_dir"], *p["rw_shared"], tb=tb)
    flat = lambda a: a.reshape(t, RW_W)

    q, k, v = _mla_qkv(z_mla, rope, p["q_norm"], p["w_uq"], p["kv_norm"], p["w_uk"], p["w_uv"],
                       seq, _tile(seq, 512))
    o_mla = _mla_attn(q, k, v, bsz, seq, _tile(seq, 1024), _tile(seq, 1024))

    u_tm = jnp.pad(z_s5.reshape(bsz, seq, S5_W).transpose(1, 0, 2), ((0, 0), (0, S5_BATCH - bsz), (0, 0)))
    u_tm = u_tm.reshape(seq * S5_BATCH, S5_W)
    y_f, y_b = _s5_scan(u_tm, p["s5_a"], p["s5_b_in"], p["s5_c_out"], seq, _tile(seq, 128))
    y_s5 = _s5_post(y_f, y_b, u_tm, *p["s5_post"], _tile(seq * S5_BATCH, 1024))
    y_s5 = y_s5.reshape(seq, S5_BATCH, S5_W)[:, :bsz].transpose(1, 0, 2).reshape(t, S5_W)

    x1 = _merge(x, flat(o_f), flat(o_b), flat(bonus), flat(rgate), o_mla, y_s5, gates,
                *p["rw_gn"], p["rw_proj"], p["mla_proj"], p["s5_proj"], p["w_out"], *p["ln1"], tm)
    return _mlp(x1, p["w1"], p["w2"], *p["ln2"], _tile(t, 1024), 1024)


def _trunk(x, layers):
    bsz, seq, _ = x.shape
    assert bsz <= S5_BATCH and seq % RW_CHUNK == 0
    rope = _rope_tables(seq)
    h = x.reshape(bsz * seq, D_MODEL)
    for p in layers:
        h = _layer(h, p, bsz, seq, rope)
    return h.reshape(bsz, seq, D_MODEL)


def kernel(x_prompt, x_sample, w_in, rw_mu, rw_w0, rw_w2, rw_a0, rw_a2, rw_g2, rw_k_k, rw_k_a, rw_r_k, rw_gn_g, rw_gn_b, rw_proj, mla_q_norm, mla_w_uq, mla_kv_norm, mla_w_ukv, mla_proj, s5_lam_re, s5_lam_im, s5_log_dt, s5_b_re, s5_b_im, s5_c_re, s5_c_im, s5_d, s5_glu_w, s5_glu_b, s5_proj, w_out, ln1_g, ln1_b, mlp_w1, mlp_w2, ln2_g, ln2_b):
    weights = (w_in, rw_mu, rw_w0, rw_w2, rw_a0, rw_a2, rw_g2, rw_k_k, rw_k_a, rw_r_k, rw_gn_g, rw_gn_b, rw_proj,
               mla_q_norm, mla_w_uq, mla_kv_norm, mla_w_ukv, mla_proj,
               s5_lam_re, s5_lam_im, s5_log_dt, s5_b_re, s5_b_im, s5_c_re, s5_c_im, s5_d, s5_glu_w, s5_glu_b,
               s5_proj, w_out, ln1_g, ln1_b, mlp_w1, mlp_w2, ln2_g, ln2_b)
    layers = [_prep_layer(*[wt[l] for wt in weights]) for l in range(w_in.shape[0])]
    return _trunk(x_prompt, layers), _trunk(x_sample, layers)
```

```python
import functools
import math

import numpy as np
import jax
import jax.numpy as jnp
from jax import lax
from jax.experimental import pallas as pl
from jax.experimental.pallas import tpu as pltpu

F32 = jnp.float32
BF16 = jnp.bfloat16
HIGHEST = lax.Precision.HIGHEST

D_MODEL = 1024
DEPTH = 2
RW_HEADS = 4
RW_HEAD = 64
RW_W = RW_HEADS * RW_HEAD
RW_LORA_W = 32
RW_LORA_A = 32
RW_LORA_G = 64
RW_GN_EPS = 64e-5
RW_IN = 3 * RW_W + RW_LORA_W + RW_LORA_A + RW_LORA_G
MLA_HEADS = 8
MLA_NOPE = 64
MLA_ROPE = 32
MLA_V = 64
MLA_Q_LORA = 256
MLA_KV_LORA = 128
MLA_W = MLA_HEADS * MLA_V
MLA_QK_PAD = 128
MLA_IN_PAD = MLA_Q_LORA + MLA_KV_LORA + MLA_QK_PAD
MLA_SCALE = (MLA_NOPE + MLA_ROPE) ** -0.5
LOG2_E = math.log2(math.e)
MLA_SHIFT_SLACK = 64.0
ROPE_THETA = 10000.0
RMS_EPS = 1e-6
S5_W = 256
S5_GROUP = 16
S5_GROUPS = S5_W // S5_GROUP
S5_STATE = 64
S5_CHUNK = 16
S5_CW = S5_CHUNK * S5_GROUP
S5_ROWS = 8
D_FF = 4 * D_MODEL
LN_EPS = 1e-5
N_BRANCH = 3
DN_ALPHA = (2 * DEPTH) ** 0.25
OFF_MLA = RW_IN
OFF_S5 = OFF_MLA + MLA_Q_LORA + MLA_KV_LORA + MLA_ROPE
OFF_GATE = OFF_S5 + S5_W
RW_CHUNK = 64
VMEM_LIMIT = 56 * 1024 * 1024


def _params(*sem):
    return pltpu.CompilerParams(dimension_semantics=sem, vmem_limit_bytes=VMEM_LIMIT)


def _mm(a, b):
    return jnp.dot(a.astype(BF16), b.astype(BF16), preferred_element_type=F32)


def _mm_nt(a, b):
    return lax.dot_general(a.astype(BF16), b.astype(BF16), (((1,), (1,)), ((), ())),
                           preferred_element_type=F32)


def _mm_tn(a, b):
    return lax.dot_general(a.astype(BF16), b.astype(BF16), (((0,), (0,)), ((), ())),
                           preferred_element_type=F32)


def _mm_f32(a, b):
    return jnp.dot(a, b, preferred_element_type=F32, precision=HIGHEST)


def _bf16_terms(x, n):
    terms = []
    for _ in range(n):
        t = x.astype(BF16)
        terms.append(t)
        x = x - t.astype(F32)
    return terms


def _mm_split(a, b, a_terms, b_terms):
    at = _bf16_terms(a, a_terms)
    bt = _bf16_terms(b, b_terms)
    out = None
    for i, x in enumerate(at):
        for j, y in enumerate(bt):
            if i + j < max(a_terms, b_terms):
                d = jnp.dot(x, y, preferred_element_type=F32)
                out = d if out is None else out + d
    return out


def _sigmoid(x):
    return 1.0 / (1.0 + jnp.exp(-x))


def _full(a):
    nd = a.ndim
    return pl.BlockSpec(a.shape, lambda *_: (0,) * nd)


def _layer_norm(x, g, b):
    mu = jnp.mean(x, -1, keepdims=True)
    xc = x - mu
    var = jnp.mean(xc * xc, -1, keepdims=True)
    return xc * lax.rsqrt(var + LN_EPS) * g + b


def _inproj_kernel(x_ref, wrw_ref, wmla_ref, ws5_ref, wg_ref, zrw_ref, zmla_ref, zs5_ref, g_ref):
    xb = x_ref[...].astype(BF16)
    zrw_ref[...] = jnp.dot(xb, wrw_ref[...], preferred_element_type=F32)
    zmla_ref[...] = jnp.dot(xb, wmla_ref[...], preferred_element_type=F32)
    zs5_ref[...] = jnp.dot(xb, ws5_ref[...], preferred_element_type=F32)
    g_ref[...] = _sigmoid(jnp.dot(xb, wg_ref[...], preferred_element_type=F32)).astype(g_ref.dtype)


def _inproj(x, w_rw, w_mla, w_s5, w_g, tm):
    t = x.shape[0]
    row = lambda n: pl.BlockSpec((tm, n), lambda i: (i, 0))
    return pl.pallas_call(
        _inproj_kernel,
        grid=(t // tm,),
        in_specs=[row(D_MODEL), _full(w_rw), _full(w_mla), _full(w_s5), _full(w_g)],
        out_specs=[row(RW_IN), row(MLA_IN_PAD), row(S5_W), row(N_BRANCH * D_MODEL)],
        out_shape=[jax.ShapeDtypeStruct((t, RW_IN), F32), jax.ShapeDtypeStruct((t, MLA_IN_PAD), F32),
                   jax.ShapeDtypeStruct((t, S5_W), F32), jax.ShapeDtypeStruct((t, N_BRANCH * D_MODEL), BF16)],
        compiler_params=_params("parallel"),
        name="inproj",
    )(x, w_rw, w_mla, w_s5, w_g)


def _rwkv_kernel(nblk, tb, nb,
                 zf_ref, zfp_ref, zfn_ref, zb_ref, zbp_ref, zbn_ref,
                 mu_ref, w0_ref, w2_ref, a0_ref, a2_ref, g2_ref, kk_ref, ka_ref, rk_ref,
                 of_ref, bonus_ref, gate_ref, ob_ref, r_s, v_s, kn_s, lw_s, b_s, kd_s, state):
    step = pl.program_id(1)
    c = RW_CHUNK
    w = RW_W
    hr = lax.broadcasted_iota(jnp.int32, (w, w), 0) // RW_HEAD
    hc = lax.broadcasted_iota(jnp.int32, (w, w), 1) // RW_HEAD
    same_head = hr == hc
    head_ones = same_head.astype(F32)

    def prepare(d, bi, z_ref, zp_ref, zn_ref, blk):
        z = z_ref[bi]
        prev_row = jnp.where(blk == 0, 0.0, zp_ref[bi, 7:8, :])
        next_row = jnp.where(blk == nblk - 1, 0.0, zn_ref[bi, 0:1, :])
        rows = lax.broadcasted_iota(jnp.int32, z.shape, 0)
        z_prev = jnp.where(rows == 0, prev_row, pltpu.roll(z, 1, axis=0))
        z_next = jnp.where(rows == tb - 1, next_row, pltpu.roll(z, tb - 1, axis=0))
        z = z + mu_ref[...] * (0.5 * (z_prev + z_next) - z)
        r = z[:, 0:w]
        k = z[:, w:2 * w]
        v = z[:, 2 * w:3 * w]
        xw = z[:, 3 * w:3 * w + RW_LORA_W]
        xa = z[:, 3 * w + RW_LORA_W:3 * w + RW_LORA_W + RW_LORA_A]
        xg = z[:, 3 * w + RW_LORA_W + RW_LORA_A:]
        kk = k * kk_ref[...]
        kk_ss = _mm_split(kk * kk, head_ones, 2, 1)
        kk = kk * lax.rsqrt(jnp.maximum(kk_ss, 1e-12))
        y = w0_ref[d] + _mm_split(jnp.tanh(xw), w2_ref[d], 2, 2)
        lw = -math.exp(-0.5) * _sigmoid(y)
        a = _sigmoid(a0_ref[d] + _mm(xa, a2_ref[d]))
        r_s[d, bi] = r
        v_s[d, bi] = v
        kn_s[d, bi] = kk
        lw_s[d, bi] = lw
        b_s[d, bi] = kk * a
        kd_s[d, bi] = k * (1.0 + (a - 1.0) * ka_ref[...])
        if d == 0:
            rk = _mm_split(r * k * rk_ref[...], head_ones, 2, 1)
            bonus_ref[bi] = rk * v
            gate_ref[bi] = _mm(_sigmoid(xg), g2_ref[...])

    for bi in range(nb):
        prepare(0, bi, zf_ref, zfp_ref, zfn_ref, step)
        prepare(1, bi, zb_ref, zbp_ref, zbn_ref, nblk - 1 - step)

    @pl.when(step == 0)
    def _():
        state[...] = jnp.zeros_like(state)

    ti = lax.broadcasted_iota(jnp.int32, (c, c), 0)
    si = lax.broadcasted_iota(jnp.int32, (c, c), 1)
    tw = lax.broadcasted_iota(jnp.int32, (c, w), 0)
    sw = lax.broadcasted_iota(jnp.int32, (c, w), 1) % RW_HEAD
    eye = (tw == sw).astype(F32)

    def stack(x):
        return jnp.where(same_head, jnp.concatenate([x] * RW_HEADS, axis=0), 0.0).astype(BF16)

    nchunk = tb // c

    def chunk(reverse, bi, cpos):
        d = 1 if reverse else 0
        o_ref = ob_ref if reverse else of_ref
        cum_mat = ((si >= ti) if reverse else (si <= ti)).astype(F32)
        strict = (tw < sw) if reverse else (tw > sw)
        incl = (tw <= sw) if reverse else (tw >= sw)
        sl = pl.ds(pl.multiple_of(cpos * c, c), c)
        lwc = lw_s[d, bi, sl, :]
        l_in = _mm_split(cum_mat, lwc, 1, 3)
        l_tot = jnp.sum(lwc, axis=0, keepdims=True)
        e_in = jnp.exp(l_in)
        e_neg = jnp.exp(-l_in)
        e_tot = jnp.exp(l_tot)
        kap = kn_s[d, bi, sl, :] * jnp.exp(l_in - lwc)
        bt = b_s[d, bi, sl, :] * e_neg
        kt = kd_s[d, bi, sl, :] * e_neg
        rt = (r_s[d, bi, sl, :] * e_in).astype(BF16)
        v = v_s[d, bi, sl, :]
        kap_w, bt_w, kt_w, v_w = stack(kap), stack(bt), stack(kt), stack(v)
        kap = kap.astype(BF16)
        end_w = jnp.concatenate([bt * e_tot, kt * e_tot], axis=0).astype(BF16)
        s_bd = state[d, bi]
        s_w = s_bd.astype(BF16)
        yield
        kap_rt = jnp.concatenate([kap, rt], axis=0)
        am_b = _mm_nt(kap_rt, bt_w)
        a_b = jnp.where(strict, am_b[0:c], 0.0)
        m_b = jnp.where(incl, am_b[c:2 * c], 0.0)
        yield
        am_k = _mm_nt(kap_rt, kt_w)
        a_k = jnp.where(strict, am_k[0:c], 0.0)
        m_k = jnp.where(incl, am_k[c:2 * c], 0.0)
        yield
        assert c == 2 ** int(math.log2(c)) and int(math.log2(c)) % 2 == 0
        powers = [-a_b]
        pairs = []
        akv = _mm(a_k, v_w)
        for level in range(1, int(math.log2(c))):
            sq_w = stack(powers[-1])
            powers.append(_mm(powers[-1], sq_w))
            if level % 2 == 0:
                lo = eye + powers[level - 2]
                pairs.append(lo + _mm(lo, sq_w))
            yield
        lo = eye + powers[-2]
        pairs.append(lo + _mm(lo, stack(powers[-1])))
        tinv = pairs[0]
        for pr in pairs[1:-1]:
            tinv = _mm(tinv, stack(pr))
        yield
        tinv = _mm(tinv, stack(pairs[-1]))
        yield
        pm = _mm(tinv, kap_w)
        qm = _mm(tinv, stack(akv))
        yield
        u = -(_mm_nt(pm, s_w) + qm)
        yield
        o = _mm_nt(rt, s_w) + _mm(m_b, stack(u)) + _mm(m_k, v_w)
        upd = _mm_tn(jnp.concatenate([u, v], axis=0), end_w)
        yield
        o_ref[bi, sl, :] = o
        state[d, bi] = s_bd * e_tot + jnp.where(same_head, upd, 0.0)

    def all_scans(ci, carry):
        scans = [chunk(False, bi, ci) for bi in range(nb)] + [chunk(True, bi, nchunk - 1 - ci) for bi in range(nb)]
        while scans:
            scans = [g for g in scans if next(g, StopIteration) is not StopIteration]
        return carry

    lax.fori_loop(0, nchunk, all_scans, 0)


def _rwkv_scan(z, mu, w0, w2, a0, a2, g2, k_k, k_a, r_k, tb):
    bsz, seq, _ = z.shape
    nblk = seq // tb
    nb = 2 if bsz % 2 == 0 else 1
    z8 = z.reshape(bsz, seq // 8, 8, RW_IN)
    t8 = tb // 8

    def specs(pos):
        z_spec = pl.BlockSpec((nb, tb, RW_IN), lambda b, i: (b, pos(i), 0))
        zp_spec = pl.BlockSpec((nb, None, 8, RW_IN), lambda b, i: (b, jnp.maximum(pos(i) * t8 - 1, 0), 0, 0))
        zn_spec = pl.BlockSpec((nb, None, 8, RW_IN),
                               lambda b, i: (b, jnp.minimum((pos(i) + 1) * t8, seq // 8 - 1), 0, 0))
        o_spec = pl.BlockSpec((nb, tb, RW_W), lambda b, i: (b, pos(i), 0))
        return [z_spec, zp_spec, zn_spec], o_spec

    in_f, o_f = specs(lambda i: i)
    in_b, o_b = specs(lambda i: nblk - 1 - i)
    o_shape = jax.ShapeDtypeStruct((bsz, seq, RW_W), F32)
    weights = [mu, w0, w2, a0, a2, g2, k_k, k_a, r_k]
    return pl.pallas_call(
        functools.partial(_rwkv_kernel, nblk, tb, nb),
        grid=(bsz // nb, nblk),
        in_specs=in_f + in_b + [_full(wt) for wt in weights],
        out_specs=[o_f, o_f, o_f, o_b],
        out_shape=[o_shape] * 4,
        scratch_shapes=[pltpu.VMEM((2, nb, tb, RW_W), F32)] * 6 + [pltpu.VMEM((2, nb, RW_W, RW_W), F32)],
        compiler_params=_params("parallel", "arbitrary"),
        name="rwkv_scan",
    )(z, z8, z8, z, z8, z8, *weights)


def _rope_tables(seq):
    half = MLA_ROPE // 2
    inv = (ROPE_THETA ** (-np.arange(half, dtype=np.float32) / half)).astype(np.float32)
    ang = np.arange(seq, dtype=np.float32)[:, None] * inv[None, :]
    cos = np.cos(ang).astype(np.float32)
    sin = np.sin(ang).astype(np.float32)
    ct = np.zeros((seq, MLA_QK_PAD), np.float32)
    st = np.zeros((seq, MLA_QK_PAD), np.float32)
    ct[:, :MLA_NOPE] = 1.0
    ct[:, MLA_NOPE:MLA_NOPE + half] = cos
    ct[:, MLA_NOPE + half:MLA_NOPE + 2 * half] = cos
    st[:, MLA_NOPE:MLA_NOPE + half] = -sin
    st[:, MLA_NOPE + half:MLA_NOPE + 2 * half] = sin
    return jnp.asarray(ct), jnp.asarray(st), jnp.asarray(ct.T), jnp.asarray(st.T)


def _rope(x, cos_t, sin_t):
    half = MLA_ROPE // 2
    lane = lax.broadcasted_iota(jnp.int32, x.shape, 1)
    swapped = jnp.where(lane < MLA_NOPE + half,
                        pltpu.roll(x, MLA_QK_PAD - half, axis=1), pltpu.roll(x, half, axis=1))
    return x * cos_t + swapped * sin_t


def _rope_rows(x, cos_t, sin_t):
    half = MLA_ROPE // 2
    a, b = MLA_NOPE, MLA_NOPE + half
    swapped = jnp.concatenate([x[0:a], x[b:b + half], x[a:b], x[b + half:]], axis=0)
    return x * cos_t + swapped * sin_t


def _mla_qkv_kernel(z_ref, cos_ref, sin_ref, cosr_ref, sinr_ref, qn_ref, wuq_ref, kvn_ref, wuk_ref, wuv_ref,
                    q_ref, k_ref, v_ref):
    z = z_ref[...]
    cos_t = cos_ref[...]
    sin_t = sin_ref[...]
    cos_r = cosr_ref[...]
    sin_r = sinr_ref[...]
    c_q = z[:, :MLA_Q_LORA]
    c_kv = z[:, MLA_Q_LORA:MLA_Q_LORA + MLA_KV_LORA]
    k_rope = _rope(z[:, MLA_Q_LORA + MLA_KV_LORA:], cos_t, sin_t)
    c_q = (c_q * lax.rsqrt(jnp.mean(c_q * c_q, -1, keepdims=True) + RMS_EPS) * qn_ref[...]).astype(BF16)
    c_kv = (c_kv * lax.rsqrt(jnp.mean(c_kv * c_kv, -1, keepdims=True) + RMS_EPS) * kvn_ref[...]).astype(BF16)
    nt = (((1,), (1,)), ((), ()))
    for h in range(MLA_HEADS):
        q_t = lax.dot_general(wuq_ref[h], c_q, nt, preferred_element_type=F32)
        q_ref[h] = (_rope_rows(q_t, cos_r, sin_r) * (MLA_SCALE * LOG2_E)).astype(BF16)
        kh = jnp.dot(c_kv, wuk_ref[h], preferred_element_type=F32)
        k_ref[h] = (kh + k_rope).astype(BF16)
        v_t = lax.dot_general(wuv_ref[h], c_kv, nt, preferred_element_type=F32)
        v_ref[h] = v_t.astype(BF16)


def _mla_qkv(z, rope, q_norm, w_uq, kv_norm, w_uk, w_uv, seq, tm):
    t = z.shape[0]
    nseq = seq // tm
    tab = pl.BlockSpec((tm, MLA_QK_PAD), lambda i: (i % nseq, 0))
    tab_r = pl.BlockSpec((MLA_QK_PAD, tm), lambda i: (0, i % nseq))
    cols = lambda n: pl.BlockSpec((MLA_HEADS, n, tm), lambda i: (0, 0, i))
    return pl.pallas_call(
        _mla_qkv_kernel,
        grid=(t // tm,),
        in_specs=[pl.BlockSpec((tm, MLA_IN_PAD), lambda i: (i, 0)), tab, tab, tab_r, tab_r,
                  _full(q_norm), _full(w_uq), _full(kv_norm), _full(w_uk), _full(w_uv)],
        out_specs=[cols(MLA_QK_PAD), pl.BlockSpec((MLA_HEADS, tm, MLA_QK_PAD), lambda i: (0, i, 0)),
                   cols(MLA_V)],
        out_shape=[jax.ShapeDtypeStruct((MLA_HEADS, MLA_QK_PAD, t), BF16),
                   jax.ShapeDtypeStruct((MLA_HEADS, t, MLA_QK_PAD), BF16),
                   jax.ShapeDtypeStruct((MLA_HEADS, MLA_V, t), BF16)],
        compiler_params=_params("parallel"),
        name="mla_qkv",
    )(z, *rope, q_norm, w_uq, kv_norm, w_uk, w_uv)


def _mla_attn_kernel(nk, q_ref, k_ref, v_ref, o_ref, m_s, l_s, acc_s, bm_s, bl_s, pv_s):
    j = pl.program_id(2)

    def scores(h):
        return jnp.dot(k_ref[h], q_ref[h], preferred_element_type=F32)

    @pl.when(j == 0)
    def _():
        m_s[...] = jnp.full(m_s.shape, -jnp.inf, F32)
        l_s[...] = jnp.zeros_like(l_s)
        acc_s[...] = jnp.zeros_like(acc_s)
        for h in range(MLA_HEADS):
            bm_s[h:h + 1, :] = jnp.max(scores(h), axis=0, keepdims=True)

    @pl.when(j > 0)
    def _():
        s_next = scores(0)
        for h in range(MLA_HEADS):
            s = s_next
            if h + 1 < MLA_HEADS:
                s_next = scores(h + 1)
            bm_s[h:h + 1, :] = jnp.max(s, axis=0, keepdims=True)
            p = jnp.exp2(s - m_s[h:h + 1, :])
            bl_s[h:h + 1, :] = jnp.sum(p, axis=0, keepdims=True)
            pv_s[h] = jnp.dot(v_ref[h], p.astype(BF16), preferred_element_type=F32)

    stale_shift_ok = jnp.max(bm_s[...] - m_s[...]) < MLA_SHIFT_SLACK

    @pl.when(stale_shift_ok)
    def _():
        for h in range(MLA_HEADS):
            m_prev = m_s[h:h + 1, :]
            m_new = jnp.maximum(m_prev, bm_s[h:h + 1, :])
            alpha = jnp.exp2(m_prev - m_new)
            acc_s[h] = (acc_s[h] + pv_s[h]) * alpha
            l_s[h:h + 1, :] = (l_s[h:h + 1, :] + bl_s[h:h + 1, :]) * alpha
            m_s[h:h + 1, :] = m_new

    @pl.when(jnp.logical_not(stale_shift_ok))
    def _():
        for h in range(MLA_HEADS):
            s = scores(h)
            m_prev = m_s[h:h + 1, :]
            m_new = jnp.maximum(m_prev, bm_s[h:h + 1, :])
            alpha = jnp.exp2(m_prev - m_new)
            p = jnp.exp2(s - m_new)
            l_s[h:h + 1, :] = alpha * l_s[h:h + 1, :] + jnp.sum(p, axis=0, keepdims=True)
            acc_s[h] = alpha * acc_s[h] + jnp.dot(v_ref[h], p.astype(BF16), preferred_element_type=F32)
            m_s[h:h + 1, :] = m_new

    @pl.when(j == nk - 1)
    def _():
        heads = [acc_s[h] / l_s[h:h + 1, :] for h in range(MLA_HEADS)]
        o_ref[...] = jnp.concatenate(heads, axis=0).T


def _mla_attn(q, k, v, bsz, seq, tq, tk):
    nq = seq // tq
    nk = seq // tk
    t = bsz * seq
    return pl.pallas_call(
        functools.partial(_mla_attn_kernel, nk),
        grid=(bsz, nq, nk),
        in_specs=[pl.BlockSpec((MLA_HEADS, MLA_QK_PAD, tq), lambda b, i, j: (0, 0, b * nq + i)),
                  pl.BlockSpec((MLA_HEADS, tk, MLA_QK_PAD), lambda b, i, j: (0, b * nk + j, 0)),
                  pl.BlockSpec((MLA_HEADS, MLA_V, tk), lambda b, i, j: (0, 0, b * nk + j))],
        out_specs=pl.BlockSpec((tq, MLA_W), lambda b, i, j: (b * nq + i, 0)),
        out_shape=jax.ShapeDtypeStruct((t, MLA_W), F32),
        scratch_shapes=[pltpu.VMEM((MLA_HEADS, tq), F32), pltpu.VMEM((MLA_HEADS, tq), F32),
                        pltpu.VMEM((MLA_HEADS, MLA_V, tq), F32),
                        pltpu.VMEM((MLA_HEADS, tq), F32), pltpu.VMEM((MLA_HEADS, tq), F32),
                        pltpu.VMEM((MLA_HEADS, MLA_V, tq), F32)],
        compiler_params=_params("parallel", "parallel", "arbitrary"),
        name="mla_attn",
    )(q, k, v)


def _s5_param_kernel(lre_ref, lim_ref, ldt_ref, btre_ref, btim_ref, cre_ref, cim_ref, ek_ref, ew_ref, ee_ref,
                     kt_ref, wre_ref, wim_ref, ere_ref, eim_ref, are_ref, aim_ref):
    lam_re = lre_ref[...]
    lam_im = lim_ref[...]
    dt = jnp.exp(ldt_ref[...])

    def power(e):
        mag = jnp.exp(lam_re * dt * e)
        ang = lam_im * dt * e
        return mag * jnp.cos(ang), mag * jnp.sin(ang)

    a_re, a_im = power(jnp.ones_like(lam_re))
    den = lam_re * lam_re + lam_im * lam_im
    nr = a_re - 1.0
    coef_re = (nr * lam_re + a_im * lam_im) / den
    coef_im = (a_im * lam_re - nr * lam_im) / den
    bt_re = btre_ref[...]
    bt_im = btim_ref[...]
    bb_re = coef_re * bt_re - coef_im * bt_im
    bb_im = coef_re * bt_im + coef_im * bt_re
    tile = lambda x: jnp.concatenate([x] * S5_CHUNK, axis=0)
    c_re = tile(cre_ref[...])
    c_im = tile(cim_ref[...])
    pk_re, pk_im = power(ek_ref[...])
    ck_re = c_re * pk_re - c_im * pk_im
    ck_im = c_re * pk_im + c_im * pk_re
    nt = (((1,), (1,)), ((), ()))
    kt_ref[...] = (lax.dot_general(bb_re, ck_re, nt, precision=HIGHEST, preferred_element_type=F32)
                   - lax.dot_general(bb_im, ck_im, nt, precision=HIGHEST, preferred_element_type=F32))
    pw_re, pw_im = power(ew_ref[...])
    bbt_re = tile(bb_re)
    bbt_im = tile(bb_im)
    wre_ref[...] = pw_re * bbt_re - pw_im * bbt_im
    wim_ref[...] = pw_re * bbt_im + pw_im * bbt_re
    pe_re, pe_im = power(ee_ref[...])
    ere_ref[...] = c_re * pe_re - c_im * pe_im
    eim_ref[...] = -(c_re * pe_im + c_im * pe_re)
    ac_re, ac_im = power(jnp.full_like(lam_re, float(S5_CHUNK)))
    are_ref[...] = ac_re
    aim_ref[...] = ac_im


def _s5_params(lam_re, lam_im, log_dt, b_re, b_im, c_re, c_im):
    n = 2 * S5_GROUPS
    p, c, cw = S5_STATE, S5_GROUP, S5_CW
    vec = lambda a: a.reshape(n, 1, p)
    ldt = jnp.broadcast_to(log_dt.reshape(n, 1, 1), (n, 1, p))
    bt = lambda a: a.transpose(0, 1, 3, 2).reshape(n, c, p)
    cc = lambda a: a.reshape(n, c, p)
    steps = np.repeat(np.arange(S5_CHUNK, dtype=np.float32), c)
    rows = lambda e: jnp.asarray(np.broadcast_to(e[:, :, None], (2, cw, p)).copy())
    e_k = rows(np.stack([steps, steps]))
    e_w = rows(np.stack([S5_CHUNK - 1 - steps, steps]))
    e_e = rows(np.stack([steps + 1, S5_CHUNK - steps]))
    per = lambda *shape: pl.BlockSpec((None,) + shape, lambda i: (i,) + (0,) * len(shape))
    per_dir = pl.BlockSpec((None, cw, p), lambda i: (i // S5_GROUPS, 0, 0))
    out = lambda *shape: jax.ShapeDtypeStruct((n,) + shape, F32)
    return pl.pallas_call(
        _s5_param_kernel,
        grid=(n,),
        in_specs=[per(1, p)] * 3 + [per(c, p)] * 4 + [per_dir] * 3,
        out_specs=[per(c, cw)] + [per(cw, p)] * 4 + [per(1, p)] * 2,
        out_shape=[out(c, cw)] + [out(cw, p)] * 4 + [out(1, p)] * 2,
        compiler_params=_params("parallel"),
        name="s5_params",
    )(vec(lam_re), vec(lam_im), ldt, bt(b_re), bt(b_im), cc(c_re), cc(c_im), e_k, e_w, e_e)


def _s5_toeplitz(kt):
    g, c, n = S5_GROUPS, S5_GROUP, S5_CHUNK
    k = kt.reshape(2, g, c, n, c)
    s_in = np.arange(n)[:, None]
    s_out = np.arange(n)[None, :]
    lag_f = np.clip(s_out - s_in, 0, n - 1)
    lag_b = np.clip(s_in - s_out, 0, n - 1)
    m_f = jnp.asarray((s_out >= s_in).astype(np.float32))[None, None, :, :, None]
    m_b = jnp.asarray((s_in >= s_out).astype(np.float32))[None, None, :, :, None]
    m = k[0][:, :, lag_f, :] * m_f + k[1][:, :, lag_b, :] * m_b
    return m.transpose(0, 2, 1, 3, 4).reshape(g, n * c, n * c)


def _s5_operators(lam_re, lam_im, log_dt, b_re, b_im, c_re, c_im):
    kt, w_re, w_im, e_re, e_im, a_re, a_im = _s5_params(lam_re, lam_im, log_dt, b_re, b_im, c_re, c_im)
    g = S5_GROUPS
    parts = lambda re, im: jnp.stack([re[:g], im[:g], re[g:], im[g:]], axis=1)
    return (_s5_toeplitz(kt).astype(BF16), parts(w_re, w_im).astype(BF16), parts(e_re, e_im).astype(BF16),
            parts(a_re[:, 0], a_im[:, 0]))


def _s5_chunk_kernel(nk, u_ref, m_ref, w_ref, e_ref, a_ref, y_ref, wfr_s, wfi_s, wbr_s, wbi_s):
    rows = S5_ROWS
    u = u_ref[...].astype(BF16)
    y_ref[...] = jnp.dot(u, m_ref[...], preferred_element_type=F32)
    for i, dst in enumerate((wfr_s, wfi_s, wbr_s, wbi_s)):
        dst[...] = jnp.dot(u, w_ref[i], preferred_element_type=F32)
    a = [jnp.broadcast_to(a_ref[i:i + 1, :], (rows, S5_STATE)) for i in range(4)]

    def step(k, carry):
        fr, fi, br, bi = carry
        sf = pl.ds(pl.multiple_of(k * rows, rows), rows)
        sb = pl.ds(pl.multiple_of((nk - 1 - k) * rows, rows), rows)
        wfr, wfi, wbr, wbi = wfr_s[sf, :], wfi_s[sf, :], wbr_s[sb, :], wbi_s[sb, :]
        wfr_s[sf, :] = fr
        wfi_s[sf, :] = fi
        wbr_s[sb, :] = br
        wbi_s[sb, :] = bi
        return (a[0] * fr - a[1] * fi + wfr, a[0] * fi + a[1] * fr + wfi,
                a[2] * br - a[3] * bi + wbr, a[2] * bi + a[3] * br + wbi)

    zero = jnp.zeros((rows, S5_STATE), F32)
    lax.fori_loop(0, nk, step, (zero, zero, zero, zero))
    carry_in = 0.0
    for i, src in enumerate((wfr_s, wfi_s, wbr_s, wbi_s)):
        carry_in = carry_in + _mm_nt(src[...], e_ref[i])
    y_ref[...] += carry_in


def _s5_scan(z_s5, bsz, seq, operators):
    m_mat, w_mat, e_mat, a_vec = operators
    nk = seq // S5_CHUNK
    r = nk * S5_ROWS
    g, c = S5_GROUPS, S5_GROUP
    u = z_s5.reshape(bsz, nk, S5_CHUNK, g, c).transpose(3, 1, 0, 2, 4)
    u = jnp.pad(u, ((0, 0), (0, 0), (0, S5_ROWS - bsz), (0, 0), (0, 0))).reshape(g, r, S5_CW)
    grp = lambda *shape: pl.BlockSpec((None,) + shape, lambda i: (i,) + (0,) * len(shape))
    y = pl.pallas_call(
        functools.partial(_s5_chunk_kernel, nk),
        grid=(g,),
        in_specs=[grp(r, S5_CW), grp(S5_CW, S5_CW), grp(4, S5_CW, S5_STATE), grp(4, S5_CW, S5_STATE),
                  grp(4, S5_STATE)],
        out_specs=grp(r, S5_CW),
        out_shape=jax.ShapeDtypeStruct((g, r, S5_CW), F32),
        scratch_shapes=[pltpu.VMEM((r, S5_STATE), F32)] * 4,
        compiler_params=_params("parallel"),
        name="s5_scan",
    )(u, m_mat, w_mat, e_mat, a_vec)
    y = y.reshape(g, nk, S5_ROWS, S5_CHUNK, c)[:, :, :bsz].transpose(2, 1, 3, 0, 4)
    return y.reshape(bsz * seq, S5_W)


def _s5_post_kernel(y_ref, u_ref, d_ref, w_ref, b_ref, o_ref):
    y = y_ref[...] + d_ref[...] * u_ref[...]
    y = jax.nn.gelu(y, approximate=True)
    o_ref[...] = y * _sigmoid(_mm(y, w_ref[...]) + b_ref[...])


def _s5_post(y, u, d_skip, glu_w, glu_b, tm):
    t = u.shape[0]
    row = pl.BlockSpec((tm, S5_W), lambda i: (i, 0))
    return pl.pallas_call(
        _s5_post_kernel,
        grid=(t // tm,),
        in_specs=[row, row, _full(d_skip), _full(glu_w), _full(glu_b)],
        out_specs=row,
        out_shape=jax.ShapeDtypeStruct((t, S5_W), F32),
        compiler_params=_params("parallel"),
        name="s5_post",
    )(y, u, d_skip, glu_w, glu_b)


def _merge_kernel(x_ref, of_ref, ob_ref, bonus_ref, rgate_ref, mla_ref, s5_ref, g_ref,
                  gng_ref, gnb_ref, wrw_ref, wmla_ref, ws5_ref, wout_ref, lng_ref, lnb_ref, o_ref):
    w = RW_W
    hr = lax.broadcasted_iota(jnp.int32, (w, w), 0) // RW_HEAD
    hc = lax.broadcasted_iota(jnp.int32, (w, w), 1) // RW_HEAD
    head_mean = (hr == hc).astype(F32) * (1.0 / RW_HEAD)
    o = of_ref[...] + ob_ref[...]
    mean = _mm_split(o, head_mean, 2, 1)
    oc = o - mean
    var = _mm_split(oc * oc, head_mean, 2, 1)
    o = oc * lax.rsqrt(var + RW_GN_EPS) * gng_ref[...] + gnb_ref[...]
    y_rw = _mm((o + bonus_ref[...]) * rgate_ref[...], wrw_ref[...])
    y_mla = _mm(mla_ref[...], wmla_ref[...])
    y_s5 = _mm(s5_ref[...], ws5_ref[...])
    d = D_MODEL
    merged = g_ref[:, 0:d] * y_rw + g_ref[:, d:2 * d] * y_mla + g_ref[:, 2 * d:3 * d] * y_s5
    o_ref[...] = _layer_norm(DN_ALPHA * x_ref[...] + _mm(merged, wout_ref[...]), lng_ref[...], lnb_ref[...])


def _merge(x, o_f, o_b, bonus, rgate, o_mla, y_s5, gates, gn_g, gn_b, w_rw, w_mla, w_s5, w_out, ln_g, ln_b, tm):
    t = x.shape[0]
    row = lambda n: pl.BlockSpec((tm, n), lambda i: (i, 0))
    weights = [gn_g, gn_b, w_rw, w_mla, w_s5, w_out, ln_g, ln_b]
    return pl.pallas_call(
        _merge_kernel,
        grid=(t // tm,),
        in_specs=[row(D_MODEL), row(RW_W), row(RW_W), row(RW_W), row(RW_W), row(MLA_W), row(S5_W),
                  row(N_BRANCH * D_MODEL)] + [_full(wt) for wt in weights],
        out_specs=row(D_MODEL),
        out_shape=jax.ShapeDtypeStruct((t, D_MODEL), F32),
        compiler_params=_params("parallel"),
        name="merge_ln1",
    )(x, o_f, o_b, bonus, rgate, o_mla, y_s5, gates, *weights)


def _mlp_kernel(nf, x_ref, w1_ref, w2_ref, lng_ref, lnb_ref, o_ref, acc_s):
    j = pl.program_id(1)

    @pl.when(j == 0)
    def _():
        acc_s[...] = jnp.zeros_like(acc_s)

    h = jnp.maximum(_mm(x_ref[...], w1_ref[...]), 0.0)
    acc_s[...] += _mm(h * h, w2_ref[...])

    @pl.when(j == nf - 1)
    def _():
        o_ref[...] = _layer_norm(DN_ALPHA * x_ref[...] + acc_s[...], lng_ref[...], lnb_ref[...])


def _mlp(x, w1, w2, ln_g, ln_b, tm, tf):
    t = x.shape[0]
    nf = D_FF // tf
    return pl.pallas_call(
        functools.partial(_mlp_kernel, nf),
        grid=(t // tm, nf),
        in_specs=[pl.BlockSpec((tm, D_MODEL), lambda i, j: (i, 0)),
                  pl.BlockSpec((D_MODEL, tf), lambda i, j: (0, j)),
                  pl.BlockSpec((tf, D_MODEL), lambda i, j: (j, 0)),
                  _full(ln_g), _full(ln_b)],
        out_specs=pl.BlockSpec((tm, D_MODEL), lambda i, j: (i, 0)),
        out_shape=jax.ShapeDtypeStruct((t, D_MODEL), F32),
        scratch_shapes=[pltpu.VMEM((tm, D_MODEL), F32)],
        compiler_params=_params("parallel", "arbitrary"),
        name="mlp_ln2",
    )(x, w1, w2, ln_g, ln_b)


def _tile(n, pref):
    t = min(n, pref)
    assert n % t == 0, (n, pref)
    return t


def _prep_layer(w_in, rw_mu, rw_w0, rw_w2, rw_a0, rw_a2, rw_g2, rw_k_k, rw_k_a, rw_r_k, rw_gn_g, rw_gn_b, rw_proj,
                mla_q_norm, mla_w_uq, mla_kv_norm, mla_w_ukv, mla_proj,
                s5_lam_re, s5_lam_im, s5_log_dt, s5_b_re, s5_b_im, s5_c_re, s5_c_im, s5_d, s5_glu_w, s5_glu_b,
                s5_proj, w_out, ln1_g, ln1_b, mlp_w1, mlp_w2, ln2_g, ln2_b):
    row = lambda a: a.reshape(1, -1)
    p = {}
    p["w_rw"] = w_in[:, :OFF_MLA].astype(BF16)
    w_mla = w_in[:, OFF_MLA:OFF_S5]
    zeros = lambda n: jnp.zeros((D_MODEL, n), F32)
    p["w_mla"] = jnp.concatenate([w_mla[:, :MLA_Q_LORA + MLA_KV_LORA], zeros(MLA_NOPE),
                                  w_mla[:, MLA_Q_LORA + MLA_KV_LORA:],
                                  zeros(MLA_QK_PAD - MLA_NOPE - MLA_ROPE)], axis=1).astype(BF16)
    p["w_s5"] = w_in[:, OFF_S5:OFF_GATE].astype(BF16)
    p["w_gate"] = w_in[:, OFF_GATE:].astype(BF16)
    p["rw"] = [row(rw_mu)]
    p["rw_dir"] = [rw_w0[:, None, :], rw_w2, rw_a0[:, None, :], rw_a2]
    p["rw_shared"] = [rw_g2, row(rw_k_k), row(rw_k_a), row(rw_r_k)]
    p["rw_gn"] = [row(rw_gn_g), row(rw_gn_b)]
    p["rw_proj"] = rw_proj.astype(BF16)
    uq = mla_w_uq.reshape(MLA_Q_LORA, MLA_HEADS, MLA_NOPE + MLA_ROPE)
    uq = jnp.pad(uq, ((0, 0), (0, 0), (0, MLA_QK_PAD - MLA_NOPE - MLA_ROPE)))
    p["w_uq"] = uq.transpose(1, 2, 0).astype(BF16)
    ukv = mla_w_ukv.reshape(MLA_KV_LORA, MLA_HEADS, MLA_NOPE + MLA_V)
    uk = jnp.pad(ukv[:, :, :MLA_NOPE], ((0, 0), (0, 0), (0, MLA_QK_PAD - MLA_NOPE)))
    p["w_uk"] = uk.transpose(1, 0, 2).astype(BF16)
    p["w_uv"] = ukv[:, :, MLA_NOPE:].transpose(1, 2, 0).astype(BF16)
    p["q_norm"] = row(mla_q_norm)
    p["kv_norm"] = row(mla_kv_norm)
    p["mla_proj"] = mla_proj.astype(BF16)
    p["s5_ops"] = _s5_operators(s5_lam_re, s5_lam_im, s5_log_dt, s5_b_re, s5_b_im, s5_c_re, s5_c_im)
    p["s5_post"] = [row(s5_d), s5_glu_w.astype(BF16), row(s5_glu_b)]
    p["s5_proj"] = s5_proj.astype(BF16)
    p["w_out"] = w_out.astype(BF16)
    p["ln1"] = [row(ln1_g), row(ln1_b)]
    p["w1"] = mlp_w1.astype(BF16)
    p["w2"] = mlp_w2.astype(BF16)
    p["ln2"] = [row(ln2_g), row(ln2_b)]
    return p


def _layer(x, p, bsz, seq, rope):
    t = bsz * seq
    tm = _tile(t, 512)
    z_rw, z_mla, z_s5, gates = _inproj(x, p["w_rw"], p["w_mla"], p["w_s5"], p["w_gate"], _tile(t, 256))

    tb = _tile(seq, 512)
    z3 = z_rw.reshape(bsz, seq, RW_IN)
    o_f, bonus, rgate, o_b = _rwkv_scan(z3, *p["rw"], *p["rw_dir"], *p["rw_shared"], tb=tb)
    flat = lambda a: a.reshape(t, RW_W)

    q, k, v = _mla_qkv(z_mla, rope, p["q_norm"], p["w_uq"], p["kv_norm"], p["w_uk"], p["w_uv"],
                       seq, _tile(seq, 512))
    o_mla = _mla_attn(q, k, v, bsz, seq, _tile(seq, 1024), _tile(seq, 1024))

    y_s5 = _s5_post(_s5_scan(z_s5, bsz, seq, p["s5_ops"]), z_s5, *p["s5_post"], _tile(t, 1024))

    x1 = _merge(x, flat(o_f), flat(o_b), flat(bonus), flat(rgate), o_mla, y_s5, gates,
                *p["rw_gn"], p["rw_proj"], p["mla_proj"], p["s5_proj"], p["w_out"], *p["ln1"], tm)
    return _mlp(x1, p["w1"], p["w2"], *p["ln2"], _tile(t, 1024), 1024)


def _trunk(x, layers):
    bsz, seq, _ = x.shape
    assert bsz <= S5_ROWS and seq % RW_CHUNK == 0 and seq % S5_CHUNK == 0
    rope = _rope_tables(seq)
    h = x.reshape(bsz * seq, D_MODEL)
    for p in layers:
        h = _layer(h, p, bsz, seq, rope)
    return h.reshape(bsz, seq, D_MODEL)


def kernel(x_prompt, x_sample, w_in, rw_mu, rw_w0, rw_w2, rw_a0, rw_a2, rw_g2, rw_k_k, rw_k_a, rw_r_k, rw_gn_g, rw_gn_b, rw_proj, mla_q_norm, mla_w_uq, mla_kv_norm, mla_w_ukv, mla_proj, s5_lam_re, s5_lam_im, s5_log_dt, s5_b_re, s5_b_im, s5_c_re, s5_c_im, s5_d, s5_glu_w, s5_glu_b, s5_proj, w_out, ln1_g, ln1_b, mlp_w1, mlp_w2, ln2_g, ln2_b):
    weights = (w_in, rw_mu, rw_w0, rw_w2, rw_a0, rw_a2, rw_g2, rw_k_k, rw_k_a, rw_r_k, rw_gn_g, rw_gn_b, rw_proj,
               mla_q_norm, mla_w_uq, mla_kv_norm, mla_w_ukv, mla_proj,
               s5_lam_re, s5_lam_im, s5_log_dt, s5_b_re, s5_b_im, s5_c_re, s5_c_im, s5_d, s5_glu_w, s5_glu_b,
               s5_proj, w_out, ln1_g, ln1_b, mlp_w1, mlp_w2, ln2_g, ln2_b)
    layers = [_prep_layer(*[wt[l] for wt in weights]) for l in range(w_in.shape[0])]
    return _trunk(x_prompt, layers), _trunk(x_sample, layers)
```

```python
import functools
import math

import numpy as np
import jax
import jax.numpy as jnp
from jax import lax
from jax.experimental import pallas as pl
from jax.experimental.pallas import tpu as pltpu

F32 = jnp.float32
BF16 = jnp.bfloat16
HIGHEST = lax.Precision.HIGHEST

D_MODEL = 1024
DEPTH = 2
RW_HEADS = 4
RW_HEAD = 64
RW_W = RW_HEADS * RW_HEAD
RW_LORA_W = 32
RW_LORA_A = 32
RW_LORA_G = 64
RW_GN_EPS = 64e-5
RW_IN = 3 * RW_W + RW_LORA_W + RW_LORA_A + RW_LORA_G
MLA_HEADS = 8
MLA_NOPE = 64
MLA_ROPE = 32
MLA_V = 64
MLA_Q_LORA = 256
MLA_KV_LORA = 128
MLA_W = MLA_HEADS * MLA_V
MLA_QK_PAD = 128
MLA_IN_PAD = MLA_Q_LORA + MLA_KV_LORA + MLA_QK_PAD
MLA_SCALE = (MLA_NOPE + MLA_ROPE) ** -0.5
LOG2_E = math.log2(math.e)
MLA_SHIFT_SLACK = 64.0
ROPE_THETA = 10000.0
RMS_EPS = 1e-6
S5_W = 256
S5_GROUP = 16
S5_GROUPS = S5_W // S5_GROUP
S5_STATE = 64
S5_CHUNK = 16
S5_CW = S5_CHUNK * S5_GROUP
S5_ROWS = 8
D_FF = 4 * D_MODEL
LN_EPS = 1e-5
N_BRANCH = 3
DN_ALPHA = (2 * DEPTH) ** 0.25
OFF_MLA = RW_IN
OFF_S5 = OFF_MLA + MLA_Q_LORA + MLA_KV_LORA + MLA_ROPE
OFF_GATE = OFF_S5 + S5_W
RW_CHUNK = 64
VMEM_LIMIT = 56 * 1024 * 1024


def _params(*sem):
    return pltpu.CompilerParams(dimension_semantics=sem, vmem_limit_bytes=VMEM_LIMIT)


def _mm(a, b):
    return jnp.dot(a.astype(BF16), b.astype(BF16), preferred_element_type=F32)


def _mm_nt(a, b):
    return lax.dot_general(a.astype(BF16), b.astype(BF16), (((1,), (1,)), ((), ())),
                           preferred_element_type=F32)


def _mm_tn(a, b):
    return lax.dot_general(a.astype(BF16), b.astype(BF16), (((0,), (0,)), ((), ())),
                           preferred_element_type=F32)


def _mm_f32(a, b):
    return jnp.dot(a, b, preferred_element_type=F32, precision=HIGHEST)


def _bf16_terms(x, n):
    terms = []
    for _ in range(n):
        t = x.astype(BF16)
        terms.append(t)
        x = x - t.astype(F32)
    return terms


def _mm_split(a, b, a_terms, b_terms):
    at = _bf16_terms(a, a_terms)
    bt = _bf16_terms(b, b_terms)
    out = None
    for i, x in enumerate(at):
        for j, y in enumerate(bt):
            if i + j < max(a_terms, b_terms):
                d = jnp.dot(x, y, preferred_element_type=F32)
                out = d if out is None else out + d
    return out


def _sigmoid(x):
    return 1.0 / (1.0 + jnp.exp(-x))


def _full(a):
    nd = a.ndim
    return pl.BlockSpec(a.shape, lambda *_: (0,) * nd)


def _layer_norm(x, g, b):
    mu = jnp.mean(x, -1, keepdims=True)
    xc = x - mu
    var = jnp.mean(xc * xc, -1, keepdims=True)
    return xc * lax.rsqrt(var + LN_EPS) * g + b


def _to_chunk_rows(x):
    n = S5_CHUNK
    assert n == S5_GROUPS and x.shape[1] == S5_CW and x.shape[0] % n == 0
    x = x.reshape(x.shape[0] // n, n, S5_CW)
    row = lax.broadcasted_iota(jnp.int32, x.shape, 1)
    pkt = lax.broadcasted_iota(jnp.int32, x.shape, 2) // S5_GROUP
    d = 1
    while d < n:
        row_bit = (row // d) % 2
        pkt_bit = (pkt // d) % 2
        up = pltpu.roll(pltpu.roll(x, n - d, axis=1), S5_GROUP * d, axis=2)
        down = pltpu.roll(pltpu.roll(x, d, axis=1), S5_CW - S5_GROUP * d, axis=2)
        x = jnp.where(row_bit == pkt_bit, x, jnp.where(row_bit == 0, up, down))
        d *= 2
    return x.reshape(x.shape[0] * n, S5_CW)


def _inproj_kernel(x_ref, wrw_ref, wmla_ref, ws5_ref, wg_ref, zrw_ref, zmla_ref, zs5_ref, g_ref):
    xb = x_ref[...].astype(BF16)
    zrw_ref[...] = jnp.dot(xb, wrw_ref[...], preferred_element_type=F32)
    zmla_ref[...] = jnp.dot(xb, wmla_ref[...], preferred_element_type=F32)
    z_s5 = jnp.dot(xb, ws5_ref[...], preferred_element_type=F32)
    zs5_ref[...] = _to_chunk_rows(z_s5)
    g_ref[...] = _sigmoid(jnp.dot(xb, wg_ref[...], preferred_element_type=F32)).astype(g_ref.dtype)


def _inproj(x, w_rw, w_mla, w_s5, w_g, tm):
    t = x.shape[0]
    row = lambda n: pl.BlockSpec((tm, n), lambda i: (i, 0))
    return pl.pallas_call(
        _inproj_kernel,
        grid=(t // tm,),
        in_specs=[row(D_MODEL), _full(w_rw), _full(w_mla), _full(w_s5), _full(w_g)],
        out_specs=[row(RW_IN), row(MLA_IN_PAD), row(S5_W), row(N_BRANCH * D_MODEL)],
        out_shape=[jax.ShapeDtypeStruct((t, RW_IN), F32), jax.ShapeDtypeStruct((t, MLA_IN_PAD), F32),
                   jax.ShapeDtypeStruct((t, S5_W), F32), jax.ShapeDtypeStruct((t, N_BRANCH * D_MODEL), BF16)],
        compiler_params=_params("parallel"),
        name="inproj",
    )(x, w_rw, w_mla, w_s5, w_g)


def _rwkv_kernel(nblk, tb, nb,
                 zf_ref, zfp_ref, zfn_ref, zb_ref, zbp_ref, zbn_ref,
                 mu_ref, w0_ref, w2_ref, a0_ref, a2_ref, g2_ref, kk_ref, ka_ref, rk_ref,
                 of_ref, bonus_ref, gate_ref, ob_ref, r_s, v_s, kn_s, lw_s, b_s, kd_s, state):
    step = pl.program_id(1)
    c = RW_CHUNK
    w = RW_W
    hr = lax.broadcasted_iota(jnp.int32, (w, w), 0) // RW_HEAD
    hc = lax.broadcasted_iota(jnp.int32, (w, w), 1) // RW_HEAD
    same_head = hr == hc
    head_ones = same_head.astype(F32)

    def prepare(d, bi, z_ref, zp_ref, zn_ref, blk):
        z = z_ref[bi]
        prev_row = jnp.where(blk == 0, 0.0, zp_ref[bi, 7:8, :])
        next_row = jnp.where(blk == nblk - 1, 0.0, zn_ref[bi, 0:1, :])
        rows = lax.broadcasted_iota(jnp.int32, z.shape, 0)
        z_prev = jnp.where(rows == 0, prev_row, pltpu.roll(z, 1, axis=0))
        z_next = jnp.where(rows == tb - 1, next_row, pltpu.roll(z, tb - 1, axis=0))
        z = z + mu_ref[...] * (0.5 * (z_prev + z_next) - z)
        r = z[:, 0:w]
        k = z[:, w:2 * w]
        v = z[:, 2 * w:3 * w]
        xw = z[:, 3 * w:3 * w + RW_LORA_W]
        xa = z[:, 3 * w + RW_LORA_W:3 * w + RW_LORA_W + RW_LORA_A]
        xg = z[:, 3 * w + RW_LORA_W + RW_LORA_A:]
        kk = k * kk_ref[...]
        kk_ss = _mm_split(kk * kk, head_ones, 2, 1)
        kk = kk * lax.rsqrt(jnp.maximum(kk_ss, 1e-12))
        y = w0_ref[d] + _mm_split(jnp.tanh(xw), w2_ref[d], 2, 2)
        lw = -math.exp(-0.5) * _sigmoid(y)
        a = _sigmoid(a0_ref[d] + _mm(xa, a2_ref[d]))
        r_s[d, bi] = r
        v_s[d, bi] = v
        kn_s[d, bi] = kk
        lw_s[d, bi] = lw
        b_s[d, bi] = kk * a
        kd_s[d, bi] = k * (1.0 + (a - 1.0) * ka_ref[...])
        if d == 0:
            rk = _mm_split(r * k * rk_ref[...], head_ones, 2, 1)
            bonus_ref[bi] = rk * v
            gate_ref[bi] = _mm(_sigmoid(xg), g2_ref[...])

    for bi in range(nb):
        prepare(0, bi, zf_ref, zfp_ref, zfn_ref, step)
        prepare(1, bi, zb_ref, zbp_ref, zbn_ref, nblk - 1 - step)

    @pl.when(step == 0)
    def _():
        state[...] = jnp.zeros_like(state)

    ti = lax.broadcasted_iota(jnp.int32, (c, c), 0)
    si = lax.broadcasted_iota(jnp.int32, (c, c), 1)
    tw = lax.broadcasted_iota(jnp.int32, (c, w), 0)
    sw = lax.broadcasted_iota(jnp.int32, (c, w), 1) % RW_HEAD
    eye = (tw == sw).astype(F32)

    def stack(x):
        return jnp.where(same_head, jnp.concatenate([x] * RW_HEADS, axis=0), 0.0).astype(BF16)

    nchunk = tb // c

    def chunk(reverse, bi, cpos):
        d = 1 if reverse else 0
        o_ref = ob_ref if reverse else of_ref
        cum_mat = ((si >= ti) if reverse else (si <= ti)).astype(F32)
        strict = (tw < sw) if reverse else (tw > sw)
        incl = (tw <= sw) if reverse else (tw >= sw)
        sl = pl.ds(pl.multiple_of(cpos * c, c), c)
        lwc = lw_s[d, bi, sl, :]
        l_in = _mm_split(cum_mat, lwc, 1, 3)
        l_tot = jnp.sum(lwc, axis=0, keepdims=True)
        e_in = jnp.exp(l_in)
        e_neg = jnp.exp(-l_in)
        e_tot = jnp.exp(l_tot)
        kap = kn_s[d, bi, sl, :] * jnp.exp(l_in - lwc)
        bt = b_s[d, bi, sl, :] * e_neg
        kt = kd_s[d, bi, sl, :] * e_neg
        rt = (r_s[d, bi, sl, :] * e_in).astype(BF16)
        v = v_s[d, bi, sl, :]
        kap_w, bt_w, kt_w, v_w = stack(kap), stack(bt), stack(kt), stack(v)
        kap = kap.astype(BF16)
        end_w = jnp.concatenate([bt * e_tot, kt * e_tot], axis=0).astype(BF16)
        s_bd = state[d, bi]
        s_w = s_bd.astype(BF16)
        yield
        kap_rt = jnp.concatenate([kap, rt], axis=0)
        am_b = _mm_nt(kap_rt, bt_w)
        a_b = jnp.where(strict, am_b[0:c], 0.0)
        m_b = jnp.where(incl, am_b[c:2 * c], 0.0)
        yield
        am_k = _mm_nt(kap_rt, kt_w)
        a_k = jnp.where(strict, am_k[0:c], 0.0)
        m_k = jnp.where(incl, am_k[c:2 * c], 0.0)
        yield
        assert c == 2 ** int(math.log2(c)) and int(math.log2(c)) % 2 == 0
        powers = [-a_b]
        pairs = []
        akv = _mm(a_k, v_w)
        for level in range(1, int(math.log2(c))):
            sq_w = stack(powers[-1])
            powers.append(_mm(powers[-1], sq_w))
            if level % 2 == 0:
                lo = eye + powers[level - 2]
                pairs.append(lo + _mm(lo, sq_w))
            yield
        lo = eye + powers[-2]
        pairs.append(lo + _mm(lo, stack(powers[-1])))
        tinv = pairs[0]
        for pr in pairs[1:-1]:
            tinv = _mm(tinv, stack(pr))
        yield
        tinv = _mm(tinv, stack(pairs[-1]))
        yield
        pm = _mm(tinv, kap_w)
        qm = _mm(tinv, stack(akv))
        yield
        u = -(_mm_nt(pm, s_w) + qm)
        yield
        o = _mm_nt(rt, s_w) + _mm(m_b, stack(u)) + _mm(m_k, v_w)
        upd = _mm_tn(jnp.concatenate([u, v], axis=0), end_w)
        yield
        o_ref[bi, sl, :] = o
        state[d, bi] = s_bd * e_tot + jnp.where(same_head, upd, 0.0)

    def all_scans(ci, carry):
        scans = [chunk(False, bi, ci) for bi in range(nb)] + [chunk(True, bi, nchunk - 1 - ci) for bi in range(nb)]
        while scans:
            scans = [g for g in scans if next(g, StopIteration) is not StopIteration]
        return carry

    lax.fori_loop(0, nchunk, all_scans, 0)


def _rwkv_scan(z, mu, w0, w2, a0, a2, g2, k_k, k_a, r_k, tb):
    bsz, seq, _ = z.shape
    nblk = seq // tb
    nb = 2 if bsz % 2 == 0 else 1
    z8 = z.reshape(bsz, seq // 8, 8, RW_IN)
    t8 = tb // 8

    def specs(pos):
        z_spec = pl.BlockSpec((nb, tb, RW_IN), lambda b, i: (b, pos(i), 0))
        zp_spec = pl.BlockSpec((nb, None, 8, RW_IN), lambda b, i: (b, jnp.maximum(pos(i) * t8 - 1, 0), 0, 0))
        zn_spec = pl.BlockSpec((nb, None, 8, RW_IN),
                               lambda b, i: (b, jnp.minimum((pos(i) + 1) * t8, seq // 8 - 1), 0, 0))
        o_spec = pl.BlockSpec((nb, tb, RW_W), lambda b, i: (b, pos(i), 0))
        return [z_spec, zp_spec, zn_spec], o_spec

    in_f, o_f = specs(lambda i: i)
    in_b, o_b = specs(lambda i: nblk - 1 - i)
    o_shape = jax.ShapeDtypeStruct((bsz, seq, RW_W), F32)
    weights = [mu, w0, w2, a0, a2, g2, k_k, k_a, r_k]
    return pl.pallas_call(
        functools.partial(_rwkv_kernel, nblk, tb, nb),
        grid=(bsz // nb, nblk),
        in_specs=in_f + in_b + [_full(wt) for wt in weights],
        out_specs=[o_f, o_f, o_f, o_b],
        out_shape=[o_shape] * 4,
        scratch_shapes=[pltpu.VMEM((2, nb, tb, RW_W), F32)] * 6 + [pltpu.VMEM((2, nb, RW_W, RW_W), F32)],
        compiler_params=_params("parallel", "arbitrary"),
        name="rwkv_scan",
    )(z, z8, z8, z, z8, z8, *weights)


def _rope_tables(seq):
    half = MLA_ROPE // 2
    inv = (ROPE_THETA ** (-np.arange(half, dtype=np.float32) / half)).astype(np.float32)
    ang = np.arange(seq, dtype=np.float32)[:, None] * inv[None, :]
    cos = np.cos(ang).astype(np.float32)
    sin = np.sin(ang).astype(np.float32)
    ct = np.zeros((seq, MLA_QK_PAD), np.float32)
    st = np.zeros((seq, MLA_QK_PAD), np.float32)
    ct[:, :MLA_NOPE] = 1.0
    ct[:, MLA_NOPE:MLA_NOPE + half] = cos
    ct[:, MLA_NOPE + half:MLA_NOPE + 2 * half] = cos
    st[:, MLA_NOPE:MLA_NOPE + half] = -sin
    st[:, MLA_NOPE + half:MLA_NOPE + 2 * half] = sin
    return jnp.asarray(ct), jnp.asarray(st), jnp.asarray(ct.T), jnp.asarray(st.T)


def _rope(x, cos_t, sin_t):
    half = MLA_ROPE // 2
    lane = lax.broadcasted_iota(jnp.int32, x.shape, 1)
    swapped = jnp.where(lane < MLA_NOPE + half,
                        pltpu.roll(x, MLA_QK_PAD - half, axis=1), pltpu.roll(x, half, axis=1))
    return x * cos_t + swapped * sin_t


def _rope_rows(x, cos_t, sin_t):
    half = MLA_ROPE // 2
    a, b = MLA_NOPE, MLA_NOPE + half
    swapped = jnp.concatenate([x[0:a], x[b:b + half], x[a:b], x[b + half:]], axis=0)
    return x * cos_t + swapped * sin_t


def _mla_qkv_kernel(z_ref, cos_ref, sin_ref, cosr_ref, sinr_ref, qn_ref, wuq_ref, kvn_ref, wuk_ref, wuv_ref,
                    q_ref, k_ref, v_ref):
    z = z_ref[...]
    cos_t = cos_ref[...]
    sin_t = sin_ref[...]
    cos_r = cosr_ref[...]
    sin_r = sinr_ref[...]
    c_q = z[:, :MLA_Q_LORA]
    c_kv = z[:, MLA_Q_LORA:MLA_Q_LORA + MLA_KV_LORA]
    k_rope = _rope(z[:, MLA_Q_LORA + MLA_KV_LORA:], cos_t, sin_t)
    c_q = (c_q * lax.rsqrt(jnp.mean(c_q * c_q, -1, keepdims=True) + RMS_EPS) * qn_ref[...]).astype(BF16)
    c_kv = (c_kv * lax.rsqrt(jnp.mean(c_kv * c_kv, -1, keepdims=True) + RMS_EPS) * kvn_ref[...]).astype(BF16)
    nt = (((1,), (1,)), ((), ()))
    for h in range(MLA_HEADS):
        q_t = lax.dot_general(wuq_ref[h], c_q, nt, preferred_element_type=F32)
        q_ref[h] = (_rope_rows(q_t, cos_r, sin_r) * (MLA_SCALE * LOG2_E)).astype(BF16)
        kh = jnp.dot(c_kv, wuk_ref[h], preferred_element_type=F32)
        k_ref[h] = (kh + k_rope).astype(BF16)
        v_t = lax.dot_general(wuv_ref[h], c_kv, nt, preferred_element_type=F32)
        v_ref[h] = v_t.astype(BF16)


def _mla_qkv(z, rope, q_norm, w_uq, kv_norm, w_uk, w_uv, seq, tm):
    t = z.shape[0]
    nseq = seq // tm
    tab = pl.BlockSpec((tm, MLA_QK_PAD), lambda i: (i % nseq, 0))
    tab_r = pl.BlockSpec((MLA_QK_PAD, tm), lambda i: (0, i % nseq))
    cols = lambda n: pl.BlockSpec((MLA_HEADS, n, tm), lambda i: (0, 0, i))
    return pl.pallas_call(
        _mla_qkv_kernel,
        grid=(t // tm,),
        in_specs=[pl.BlockSpec((tm, MLA_IN_PAD), lambda i: (i, 0)), tab, tab, tab_r, tab_r,
                  _full(q_norm), _full(w_uq), _full(kv_norm), _full(w_uk), _full(w_uv)],
        out_specs=[cols(MLA_QK_PAD), pl.BlockSpec((MLA_HEADS, tm, MLA_QK_PAD), lambda i: (0, i, 0)),
                   cols(MLA_V)],
        out_shape=[jax.ShapeDtypeStruct((MLA_HEADS, MLA_QK_PAD, t), BF16),
                   jax.ShapeDtypeStruct((MLA_HEADS, t, MLA_QK_PAD), BF16),
                   jax.ShapeDtypeStruct((MLA_HEADS, MLA_V, t), BF16)],
        compiler_params=_params("parallel"),
        name="mla_qkv",
    )(z, *rope, q_norm, w_uq, kv_norm, w_uk, w_uv)


def _mla_attn_kernel(nk, q_ref, k_ref, v_ref, o_ref, m_s, l_s, acc_s, bm_s, bl_s, pv_s):
    j = pl.program_id(2)

    def scores(h):
        return jnp.dot(k_ref[h], q_ref[h], preferred_element_type=F32)

    @pl.when(j == 0)
    def _():
        m_s[...] = jnp.full(m_s.shape, -jnp.inf, F32)
        l_s[...] = jnp.zeros_like(l_s)
        acc_s[...] = jnp.zeros_like(acc_s)
        for h in range(MLA_HEADS):
            bm_s[h:h + 1, :] = jnp.max(scores(h), axis=0, keepdims=True)

    @pl.when(j > 0)
    def _():
        s_next = scores(0)
        for h in range(MLA_HEADS):
            s = s_next
            if h + 1 < MLA_HEADS:
                s_next = scores(h + 1)
            bm_s[h:h + 1, :] = jnp.max(s, axis=0, keepdims=True)
            p = jnp.exp2(s - m_s[h:h + 1, :])
            bl_s[h:h + 1, :] = jnp.sum(p, axis=0, keepdims=True)
            pv_s[h] = jnp.dot(v_ref[h], p.astype(BF16), preferred_element_type=F32)

    stale_shift_ok = jnp.max(bm_s[...] - m_s[...]) < MLA_SHIFT_SLACK

    @pl.when(stale_shift_ok)
    def _():
        for h in range(MLA_HEADS):
            m_prev = m_s[h:h + 1, :]
            m_new = jnp.maximum(m_prev, bm_s[h:h + 1, :])
            alpha = jnp.exp2(m_prev - m_new)
            acc_s[h] = (acc_s[h] + pv_s[h]) * alpha
            l_s[h:h + 1, :] = (l_s[h:h + 1, :] + bl_s[h:h + 1, :]) * alpha
            m_s[h:h + 1, :] = m_new

    @pl.when(jnp.logical_not(stale_shift_ok))
    def _():
        for h in range(MLA_HEADS):
            s = scores(h)
            m_prev = m_s[h:h + 1, :]
            m_new = jnp.maximum(m_prev, bm_s[h:h + 1, :])
            alpha = jnp.exp2(m_prev - m_new)
            p = jnp.exp2(s - m_new)
            l_s[h:h + 1, :] = alpha * l_s[h:h + 1, :] + jnp.sum(p, axis=0, keepdims=True)
            acc_s[h] = alpha * acc_s[h] + jnp.dot(v_ref[h], p.astype(BF16), preferred_element_type=F32)
            m_s[h:h + 1, :] = m_new

    @pl.when(j == nk - 1)
    def _():
        heads = [acc_s[h] / l_s[h:h + 1, :] for h in range(MLA_HEADS)]
        o_ref[...] = jnp.concatenate(heads, axis=0).T


def _mla_attn(q, k, v, bsz, seq, tq, tk):
    nq = seq // tq
    nk = seq // tk
    t = bsz * seq
    return pl.pallas_call(
        functools.partial(_mla_attn_kernel, nk),
        grid=(bsz, nq, nk),
        in_specs=[pl.BlockSpec((MLA_HEADS, MLA_QK_PAD, tq), lambda b, i, j: (0, 0, b * nq + i)),
                  pl.BlockSpec((MLA_HEADS, tk, MLA_QK_PAD), lambda b, i, j: (0, b * nk + j, 0)),
                  pl.BlockSpec((MLA_HEADS, MLA_V, tk), lambda b, i, j: (0, 0, b * nk + j))],
        out_specs=pl.BlockSpec((tq, MLA_W), lambda b, i, j: (b * nq + i, 0)),
        out_shape=jax.ShapeDtypeStruct((t, MLA_W), F32),
        scratch_shapes=[pltpu.VMEM((MLA_HEADS, tq), F32), pltpu.VMEM((MLA_HEADS, tq), F32),
                        pltpu.VMEM((MLA_HEADS, MLA_V, tq), F32),
                        pltpu.VMEM((MLA_HEADS, tq), F32), pltpu.VMEM((MLA_HEADS, tq), F32),
                        pltpu.VMEM((MLA_HEADS, MLA_V, tq), F32)],
        compiler_params=_params("parallel", "parallel", "arbitrary"),
        name="mla_attn",
    )(q, k, v)


def _s5_param_kernel(lre_ref, lim_ref, ldt_ref, btre_ref, btim_ref, cre_ref, cim_ref, ek_ref, ew_ref, ee_ref,
                     kt_ref, wre_ref, wim_ref, ere_ref, eim_ref, are_ref, aim_ref):
    lam_re = lre_ref[...]
    lam_im = lim_ref[...]
    dt = jnp.exp(ldt_ref[...])

    def power(e):
        mag = jnp.exp(lam_re * dt * e)
        ang = lam_im * dt * e
        return mag * jnp.cos(ang), mag * jnp.sin(ang)

    a_re, a_im = power(jnp.ones_like(lam_re))
    den = lam_re * lam_re + lam_im * lam_im
    nr = a_re - 1.0
    coef_re = (nr * lam_re + a_im * lam_im) / den
    coef_im = (a_im * lam_re - nr * lam_im) / den
    bt_re = btre_ref[...]
    bt_im = btim_ref[...]
    bb_re = coef_re * bt_re - coef_im * bt_im
    bb_im = coef_re * bt_im + coef_im * bt_re
    tile = lambda x: jnp.concatenate([x] * S5_CHUNK, axis=0)
    c_re = tile(cre_ref[...])
    c_im = tile(cim_ref[...])
    pk_re, pk_im = power(ek_ref[...])
    ck_re = c_re * pk_re - c_im * pk_im
    ck_im = c_re * pk_im + c_im * pk_re
    nt = (((1,), (1,)), ((), ()))
    kt_ref[...] = (lax.dot_general(bb_re, ck_re, nt, precision=HIGHEST, preferred_element_type=F32)
                   - lax.dot_general(bb_im, ck_im, nt, precision=HIGHEST, preferred_element_type=F32))
    pw_re, pw_im = power(ew_ref[...])
    bbt_re = tile(bb_re)
    bbt_im = tile(bb_im)
    wre_ref[...] = pw_re * bbt_re - pw_im * bbt_im
    wim_ref[...] = pw_re * bbt_im + pw_im * bbt_re
    pe_re, pe_im = power(ee_ref[...])
    ere_ref[...] = c_re * pe_re - c_im * pe_im
    eim_ref[...] = -(c_re * pe_im + c_im * pe_re)
    ac_re, ac_im = power(jnp.full_like(lam_re, float(S5_CHUNK)))
    are_ref[...] = ac_re
    aim_ref[...] = ac_im


def _s5_params(lam_re, lam_im, log_dt, b_re, b_im, c_re, c_im):
    n = 2 * S5_GROUPS
    p, c, cw = S5_STATE, S5_GROUP, S5_CW
    vec = lambda a: a.reshape(n, 1, p)
    ldt = jnp.broadcast_to(log_dt.reshape(n, 1, 1), (n, 1, p))
    bt = lambda a: a.transpose(0, 1, 3, 2).reshape(n, c, p)
    cc = lambda a: a.reshape(n, c, p)
    steps = np.repeat(np.arange(S5_CHUNK, dtype=np.float32), c)
    rows = lambda e: jnp.asarray(np.broadcast_to(e[:, :, None], (2, cw, p)).copy())
    e_k = rows(np.stack([steps, steps]))
    e_w = rows(np.stack([S5_CHUNK - 1 - steps, steps]))
    e_e = rows(np.stack([steps + 1, S5_CHUNK - steps]))
    per = lambda *shape: pl.BlockSpec((None,) + shape, lambda i: (i,) + (0,) * len(shape))
    per_dir = pl.BlockSpec((None, cw, p), lambda i: (i // S5_GROUPS, 0, 0))
    out = lambda *shape: jax.ShapeDtypeStruct((n,) + shape, F32)
    return pl.pallas_call(
        _s5_param_kernel,
        grid=(n,),
        in_specs=[per(1, p)] * 3 + [per(c, p)] * 4 + [per_dir] * 3,
        out_specs=[per(c, cw)] + [per(cw, p)] * 4 + [per(1, p)] * 2,
        out_shape=[out(c, cw)] + [out(cw, p)] * 4 + [out(1, p)] * 2,
        compiler_params=_params("parallel"),
        name="s5_params",
    )(vec(lam_re), vec(lam_im), ldt, bt(b_re), bt(b_im), cc(c_re), cc(c_im), e_k, e_w, e_e)


def _s5_toeplitz(kt):
    g, c, n = S5_GROUPS, S5_GROUP, S5_CHUNK
    k = kt.reshape(2, g, c, n, c)
    s_in = np.arange(n)[:, None]
    s_out = np.arange(n)[None, :]
    lag_f = np.clip(s_out - s_in, 0, n - 1)
    lag_b = np.clip(s_in - s_out, 0, n - 1)
    m_f = jnp.asarray((s_out >= s_in).astype(np.float32))[None, None, :, :, None]
    m_b = jnp.asarray((s_in >= s_out).astype(np.float32))[None, None, :, :, None]
    m = k[0][:, :, lag_f, :] * m_f + k[1][:, :, lag_b, :] * m_b
    return m.transpose(0, 2, 1, 3, 4).reshape(g, n * c, n * c)


def _s5_operators(lam_re, lam_im, log_dt, b_re, b_im, c_re, c_im):
    kt, w_re, w_im, e_re, e_im, a_re, a_im = _s5_params(lam_re, lam_im, log_dt, b_re, b_im, c_re, c_im)
    g = S5_GROUPS
    parts = lambda re, im: jnp.stack([re[:g], im[:g], re[g:], im[g:]], axis=1)
    return (_s5_toeplitz(kt).astype(BF16), parts(w_re, w_im).astype(BF16), parts(e_re, e_im).astype(BF16),
            parts(a_re[:, 0], a_im[:, 0]))


def _s5_chunk_kernel(nk, u_ref, m_ref, w_ref, e_ref, a_ref, y_ref, wfr_s, wfi_s, wbr_s, wbi_s):
    rows = S5_ROWS
    u = u_ref[...].astype(BF16)
    y_ref[...] = jnp.dot(u, m_ref[...], preferred_element_type=F32)
    for i, dst in enumerate((wfr_s, wfi_s, wbr_s, wbi_s)):
        dst[...] = jnp.dot(u, w_ref[i], preferred_element_type=F32)
    a = [jnp.broadcast_to(a_ref[i:i + 1, :], (rows, S5_STATE)) for i in range(4)]

    def step(k, carry):
        fr, fi, br, bi = carry
        sf = pl.ds(pl.multiple_of(k * rows, rows), rows)
        sb = pl.ds(pl.multiple_of((nk - 1 - k) * rows, rows), rows)
        wfr, wfi, wbr, wbi = wfr_s[sf, :], wfi_s[sf, :], wbr_s[sb, :], wbi_s[sb, :]
        wfr_s[sf, :] = fr
        wfi_s[sf, :] = fi
        wbr_s[sb, :] = br
        wbi_s[sb, :] = bi
        return (a[0] * fr - a[1] * fi + wfr, a[0] * fi + a[1] * fr + wfi,
                a[2] * br - a[3] * bi + wbr, a[2] * bi + a[3] * br + wbi)

    zero = jnp.zeros((rows, S5_STATE), F32)
    lax.fori_loop(0, nk, step, (zero, zero, zero, zero))
    carry_in = 0.0
    for i, src in enumerate((wfr_s, wfi_s, wbr_s, wbi_s)):
        carry_in = carry_in + _mm_nt(src[...], e_ref[i])
    y_ref[...] += carry_in


def _s5_scan(z_rows, bsz, seq, operators):
    m_mat, w_mat, e_mat, a_vec = operators
    nk = seq // S5_CHUNK
    r = nk * S5_ROWS
    g = S5_GROUPS
    u = z_rows.reshape(bsz, nk, g * S5_CW).transpose(1, 0, 2)
    u = jnp.pad(u, ((0, 0), (0, S5_ROWS - bsz), (0, 0))).reshape(r, g * S5_CW)
    grp = lambda *shape: pl.BlockSpec((None,) + shape, lambda i: (i,) + (0,) * len(shape))
    seq_rows = pl.BlockSpec((r, S5_CW), lambda i: (0, i))
    y = pl.pallas_call(
        functools.partial(_s5_chunk_kernel, nk),
        grid=(g,),
        in_specs=[seq_rows, grp(S5_CW, S5_CW), grp(4, S5_CW, S5_STATE), grp(4, S5_CW, S5_STATE), grp(4, S5_STATE)],
        out_specs=seq_rows,
        out_shape=jax.ShapeDtypeStruct((r, g * S5_CW), F32),
        scratch_shapes=[pltpu.VMEM((r, S5_STATE), F32)] * 4,
        compiler_params=_params("parallel"),
        name="s5_scan",
    )(u, m_mat, w_mat, e_mat, a_vec)
    y = y.reshape(nk, S5_ROWS, g * S5_CW)[:, :bsz].transpose(1, 0, 2)
    return y.reshape(bsz * seq, S5_W)


def _s5_post_kernel(y_ref, u_ref, d_ref, w_ref, b_ref, o_ref):
    tm = y_ref.shape[0]
    y = y_ref[...].reshape(tm // S5_GROUPS, S5_GROUPS, S5_CW)
    u = u_ref[...].reshape(tm // S5_GROUPS, S5_GROUPS, S5_CW)
    y = _to_chunk_rows((y + d_ref[...] * u).reshape(tm, S5_CW))
    y = jax.nn.gelu(y, approximate=True)
    o_ref[...] = y * _sigmoid(_mm(y, w_ref[...]) + b_ref[...])


def _s5_post(y, u, d_rows, glu_w, glu_b, tm):
    t = u.shape[0]
    row = pl.BlockSpec((tm, S5_W), lambda i: (i, 0))
    return pl.pallas_call(
        _s5_post_kernel,
        grid=(t // tm,),
        in_specs=[row, row, _full(d_rows), _full(glu_w), _full(glu_b)],
        out_specs=row,
        out_shape=jax.ShapeDtypeStruct((t, S5_W), F32),
        compiler_params=_params("parallel"),
        name="s5_post",
    )(y, u, d_rows, glu_w, glu_b)


def _merge_kernel(x_ref, of_ref, ob_ref, bonus_ref, rgate_ref, mla_ref, s5_ref, g_ref,
                  gng_ref, gnb_ref, wrw_ref, wmla_ref, ws5_ref, wout_ref, lng_ref, lnb_ref, o_ref):
    w = RW_W
    hr = lax.broadcasted_iota(jnp.int32, (w, w), 0) // RW_HEAD
    hc = lax.broadcasted_iota(jnp.int32, (w, w), 1) // RW_HEAD
    head_mean = (hr == hc).astype(F32) * (1.0 / RW_HEAD)
    o = of_ref[...] + ob_ref[...]
    mean = _mm_split(o, head_mean, 2, 1)
    oc = o - mean
    var = _mm_split(oc * oc, head_mean, 2, 1)
    o = oc * lax.rsqrt(var + RW_GN_EPS) * gng_ref[...] + gnb_ref[...]
    y_rw = _mm((o + bonus_ref[...]) * rgate_ref[...], wrw_ref[...])
    y_mla = _mm(mla_ref[...], wmla_ref[...])
    y_s5 = _mm(s5_ref[...], ws5_ref[...])
    d = D_MODEL
    merged = g_ref[:, 0:d] * y_rw + g_ref[:, d:2 * d] * y_mla + g_ref[:, 2 * d:3 * d] * y_s5
    o_ref[...] = _layer_norm(DN_ALPHA * x_ref[...] + _mm(merged, wout_ref[...]), lng_ref[...], lnb_ref[...])


def _merge(x, o_f, o_b, bonus, rgate, o_mla, y_s5, gates, gn_g, gn_b, w_rw, w_mla, w_s5, w_out, ln_g, ln_b, tm):
    t = x.shape[0]
    row = lambda n: pl.BlockSpec((tm, n), lambda i: (i, 0))
    weights = [gn_g, gn_b, w_rw, w_mla, w_s5, w_out, ln_g, ln_b]
    return pl.pallas_call(
        _merge_kernel,
        grid=(t // tm,),
        in_specs=[row(D_MODEL), row(RW_W), row(RW_W), row(RW_W), row(RW_W), row(MLA_W), row(S5_W),
                  row(N_BRANCH * D_MODEL)] + [_full(wt) for wt in weights],
        out_specs=row(D_MODEL),
        out_shape=jax.ShapeDtypeStruct((t, D_MODEL), F32),
        compiler_params=_params("parallel"),
        name="merge_ln1",
    )(x, o_f, o_b, bonus, rgate, o_mla, y_s5, gates, *weights)


def _mlp_kernel(nf, x_ref, w1_ref, w2_ref, lng_ref, lnb_ref, o_ref, acc_s):
    j = pl.program_id(1)

    @pl.when(j == 0)
    def _():
        acc_s[...] = jnp.zeros_like(acc_s)

    h = jnp.maximum(_mm(x_ref[...], w1_ref[...]), 0.0)
    acc_s[...] += _mm(h * h, w2_ref[...])

    @pl.when(j == nf - 1)
    def _():
        o_ref[...] = _layer_norm(DN_ALPHA * x_ref[...] + acc_s[...], lng_ref[...], lnb_ref[...])


def _mlp(x, w1, w2, ln_g, ln_b, tm, tf):
    t = x.shape[0]
    nf = D_FF // tf
    return pl.pallas_call(
        functools.partial(_mlp_kernel, nf),
        grid=(t // tm, nf),
        in_specs=[pl.BlockSpec((tm, D_MODEL), lambda i, j: (i, 0)),
                  pl.BlockSpec((D_MODEL, tf), lambda i, j: (0, j)),
                  pl.BlockSpec((tf, D_MODEL), lambda i, j: (j, 0)),
                  _full(ln_g), _full(ln_b)],
        out_specs=pl.BlockSpec((tm, D_MODEL), lambda i, j: (i, 0)),
        out_shape=jax.ShapeDtypeStruct((t, D_MODEL), F32),
        scratch_shapes=[pltpu.VMEM((tm, D_MODEL), F32)],
        compiler_params=_params("parallel", "arbitrary"),
        name="mlp_ln2",
    )(x, w1, w2, ln_g, ln_b)


def _tile(n, pref):
    t = min(n, pref)
    assert n % t == 0, (n, pref)
    return t


def _prep_layer(w_in, rw_mu, rw_w0, rw_w2, rw_a0, rw_a2, rw_g2, rw_k_k, rw_k_a, rw_r_k, rw_gn_g, rw_gn_b, rw_proj,
                mla_q_norm, mla_w_uq, mla_kv_norm, mla_w_ukv, mla_proj,
                s5_lam_re, s5_lam_im, s5_log_dt, s5_b_re, s5_b_im, s5_c_re, s5_c_im, s5_d, s5_glu_w, s5_glu_b,
                s5_proj, w_out, ln1_g, ln1_b, mlp_w1, mlp_w2, ln2_g, ln2_b):
    row = lambda a: a.reshape(1, -1)
    p = {}
    p["w_rw"] = w_in[:, :OFF_MLA].astype(BF16)
    w_mla = w_in[:, OFF_MLA:OFF_S5]
    zeros = lambda n: jnp.zeros((D_MODEL, n), F32)
    p["w_mla"] = jnp.concatenate([w_mla[:, :MLA_Q_LORA + MLA_KV_LORA], zeros(MLA_NOPE),
                                  w_mla[:, MLA_Q_LORA + MLA_KV_LORA:],
                                  zeros(MLA_QK_PAD - MLA_NOPE - MLA_ROPE)], axis=1).astype(BF16)
    p["w_s5"] = w_in[:, OFF_S5:OFF_GATE].astype(BF16)
    p["w_gate"] = w_in[:, OFF_GATE:].astype(BF16)
    p["rw"] = [row(rw_mu)]
    p["rw_dir"] = [rw_w0[:, None, :], rw_w2, rw_a0[:, None, :], rw_a2]
    p["rw_shared"] = [rw_g2, row(rw_k_k), row(rw_k_a), row(rw_r_k)]
    p["rw_gn"] = [row(rw_gn_g), row(rw_gn_b)]
    p["rw_proj"] = rw_proj.astype(BF16)
    uq = mla_w_uq.reshape(MLA_Q_LORA, MLA_HEADS, MLA_NOPE + MLA_ROPE)
    uq = jnp.pad(uq, ((0, 0), (0, 0), (0, MLA_QK_PAD - MLA_NOPE - MLA_ROPE)))
    p["w_uq"] = uq.transpose(1, 2, 0).astype(BF16)
    ukv = mla_w_ukv.reshape(MLA_KV_LORA, MLA_HEADS, MLA_NOPE + MLA_V)
    uk = jnp.pad(ukv[:, :, :MLA_NOPE], ((0, 0), (0, 0), (0, MLA_QK_PAD - MLA_NOPE)))
    p["w_uk"] = uk.transpose(1, 0, 2).astype(BF16)
    p["w_uv"] = ukv[:, :, MLA_NOPE:].transpose(1, 2, 0).astype(BF16)
    p["q_norm"] = row(mla_q_norm)
    p["kv_norm"] = row(mla_kv_norm)
    p["mla_proj"] = mla_proj.astype(BF16)
    p["s5_ops"] = _s5_operators(s5_lam_re, s5_lam_im, s5_log_dt, s5_b_re, s5_b_im, s5_c_re, s5_c_im)
    d_rows = jnp.broadcast_to(s5_d.reshape(S5_GROUPS, 1, S5_GROUP), (S5_GROUPS, S5_CHUNK, S5_GROUP))
    p["s5_post"] = [d_rows.reshape(S5_GROUPS, S5_CW), s5_glu_w.astype(BF16), row(s5_glu_b)]
    p["s5_proj"] = s5_proj.astype(BF16)
    p["w_out"] = w_out.astype(BF16)
    p["ln1"] = [row(ln1_g), row(ln1_b)]
    p["w1"] = mlp_w1.astype(BF16)
    p["w2"] = mlp_w2.astype(BF16)
    p["ln2"] = [row(ln2_g), row(ln2_b)]
    return p


def _layer(x, p, bsz, seq, rope):
    t = bsz * seq
    tm = _tile(t, 512)
    z_rw, z_mla, z_s5, gates = _inproj(x, p["w_rw"], p["w_mla"], p["w_s5"], p["w_gate"], _tile(t, 256))

    tb = _tile(seq, 512)
    z3 = z_rw.reshape(bsz, seq, RW_IN)
    o_f, bonus, rgate, o_b = _rwkv_scan(z3, *p["rw"], *p["rw_dir"], *p["rw_shared"], tb=tb)
    flat = lambda a: a.reshape(t, RW_W)

    q, k, v = _mla_qkv(z_mla, rope, p["q_norm"], p["w_uq"], p["kv_norm"], p["w_uk"], p["w_uv"],
                       seq, _tile(seq, 512))
    o_mla = _mla_attn(q, k, v, bsz, seq, _tile(seq, 1024), _tile(seq, 1024))

    y_s5 = _s5_post(_s5_scan(z_s5, bsz, seq, p["s5_ops"]), z_s5, *p["s5_post"], _tile(t, 1024))

    x1 = _merge(x, flat(o_f), flat(o_b), flat(bonus), flat(rgate), o_mla, y_s5, gates,
                *p["rw_gn"], p["rw_proj"], p["mla_proj"], p["s5_proj"], p["w_out"], *p["ln1"], tm)
    return _mlp(x1, p["w1"], p["w2"], *p["ln2"], _tile(t, 1024), 1024)


def _trunk(x, layers):
    bsz, seq, _ = x.shape
    assert bsz <= S5_ROWS and seq % RW_CHUNK == 0 and seq % S5_CHUNK == 0
    rope = _rope_tables(seq)
    h = x.reshape(bsz * seq, D_MODEL)
    for p in layers:
        h = _layer(h, p, bsz, seq, rope)
    return h.reshape(bsz, seq, D_MODEL)


def kernel(x_prompt, x_sample, w_in, rw_mu, rw_w0, rw_w2, rw_a0, rw_a2, rw_g2, rw_k_k, rw_k_a, rw_r_k, rw_gn_g, rw_gn_b, rw_proj, mla_q_norm, mla_w_uq, mla_kv_norm, mla_w_ukv, mla_proj, s5_lam_re, s5_lam_im, s5_log_dt, s5_b_re, s5_b_im, s5_c_re, s5_c_im, s5_d, s5_glu_w, s5_glu_b, s5_proj, w_out, ln1_g, ln1_b, mlp_w1, mlp_w2, ln2_g, ln2_b):
    weights = (w_in, rw_mu, rw_w0, rw_w2, rw_a0, rw_a2, rw_g2, rw_k_k, rw_k_a, rw_r_k, rw_gn_g, rw_gn_b, rw_proj,
               mla_q_norm, mla_w_uq, mla_kv_norm, mla_w_ukv, mla_proj,
               s5_lam_re, s5_lam_im, s5_log_dt, s5_b_re, s5_b_im, s5_c_re, s5_c_im, s5_d, s5_glu_w, s5_glu_b,
               s5_proj, w_out, ln1_g, ln1_b, mlp_w1, mlp_w2, ln2_g, ln2_b)
    layers = [_prep_layer(*[wt[l] for wt in weights]) for l in range(w_in.shape[0])]
    return _trunk(x_prompt, layers), _trunk(x_sample, layers)
```

```python
import functools
import math

import numpy as np
import jax
import jax.numpy as jnp
from jax import lax
from jax.experimental import pallas as pl
from jax.experimental.pallas import tpu as pltpu

F32 = jnp.float32
BF16 = jnp.bfloat16
HIGHEST = lax.Precision.HIGHEST

D_MODEL = 1024
DEPTH = 2
RW_HEADS = 4
RW_HEAD = 64
RW_W = RW_HEADS * RW_HEAD
RW_LORA_W = 32
RW_LORA_A = 32
RW_LORA_G = 64
RW_GN_EPS = 64e-5
RW_IN = 3 * RW_W + RW_LORA_W + RW_LORA_A + RW_LORA_G
MLA_HEADS = 8
MLA_NOPE = 64
MLA_ROPE = 32
MLA_V = 64
MLA_Q_LORA = 256
MLA_KV_LORA = 128
MLA_W = MLA_HEADS * MLA_V
MLA_QK_PAD = 128
MLA_IN_PAD = MLA_Q_LORA + MLA_KV_LORA + MLA_QK_PAD
MLA_SCALE = (MLA_NOPE + MLA_ROPE) ** -0.5
LOG2_E = math.log2(math.e)
MLA_SHIFT_SLACK = 64.0
ROPE_THETA = 10000.0
RMS_EPS = 1e-6
S5_W = 256
S5_GROUP = 16
S5_GROUPS = S5_W // S5_GROUP
S5_STATE = 64
S5_CHUNK = 16
S5_CW = S5_CHUNK * S5_GROUP
D_FF = 4 * D_MODEL
LN_EPS = 1e-5
N_BRANCH = 3
DN_ALPHA = (2 * DEPTH) ** 0.25
OFF_MLA = RW_IN
OFF_S5 = OFF_MLA + MLA_Q_LORA + MLA_KV_LORA + MLA_ROPE
OFF_GATE = OFF_S5 + S5_W
RW_CHUNK = 64
VMEM_LIMIT = 56 * 1024 * 1024


def _params(*sem):
    return pltpu.CompilerParams(dimension_semantics=sem, vmem_limit_bytes=VMEM_LIMIT)


def _mm(a, b):
    return jnp.dot(a.astype(BF16), b.astype(BF16), preferred_element_type=F32)


def _mm_nt(a, b):
    return lax.dot_general(a.astype(BF16), b.astype(BF16), (((1,), (1,)), ((), ())),
                           preferred_element_type=F32)


def _mm_tn(a, b):
    return lax.dot_general(a.astype(BF16), b.astype(BF16), (((0,), (0,)), ((), ())),
                           preferred_element_type=F32)


def _mm_f32(a, b):
    return jnp.dot(a, b, preferred_element_type=F32, precision=HIGHEST)


def _bf16_terms(x, n):
    terms = []
    for _ in range(n):
        t = x.astype(BF16)
        terms.append(t)
        x = x - t.astype(F32)
    return terms


def _mm_split(a, b, a_terms, b_terms):
    at = _bf16_terms(a, a_terms)
    bt = _bf16_terms(b, b_terms)
    out = None
    for i, x in enumerate(at):
        for j, y in enumerate(bt):
            if i + j < max(a_terms, b_terms):
                d = jnp.dot(x, y, preferred_element_type=F32)
                out = d if out is None else out + d
    return out


def _sigmoid(x):
    return 1.0 / (1.0 + jnp.exp(-x))


def _full(a):
    nd = a.ndim
    return pl.BlockSpec(a.shape, lambda *_: (0,) * nd)


def _layer_norm(x, g, b):
    mu = jnp.mean(x, -1, keepdims=True)
    xc = x - mu
    var = jnp.mean(xc * xc, -1, keepdims=True)
    return xc * lax.rsqrt(var + LN_EPS) * g + b


def _to_chunk_rows(x):
    n = S5_CHUNK
    assert n == S5_GROUPS and x.shape[1] == S5_CW and x.shape[0] % n == 0
    x = x.reshape(x.shape[0] // n, n, S5_CW)
    row = lax.broadcasted_iota(jnp.int32, (1, n, S5_CW), 1)
    pkt = lax.broadcasted_iota(jnp.int32, (1, n, S5_CW), 2) // S5_GROUP
    d = 1
    while d < n:
        row_bit = (row // d) % 2
        pkt_bit = (pkt // d) % 2
        up = pltpu.roll(pltpu.roll(x, n - d, axis=1), S5_GROUP * d, axis=2)
        down = pltpu.roll(pltpu.roll(x, d, axis=1), S5_CW - S5_GROUP * d, axis=2)
        x = jnp.where(row_bit == pkt_bit, x, jnp.where(row_bit == 0, up, down))
        d *= 2
    return x.reshape(x.shape[0] * n, S5_CW)


def _inproj_kernel(x_ref, wrw_ref, wmla_ref, ws5_ref, wg_ref, zrw_ref, zmla_ref, zs5_ref, g_ref):
    xb = x_ref[...].astype(BF16)
    zrw_ref[...] = jnp.dot(xb, wrw_ref[...], preferred_element_type=F32)
    zmla_ref[...] = jnp.dot(xb, wmla_ref[...], preferred_element_type=F32)
    z_s5 = jnp.dot(xb, ws5_ref[...], preferred_element_type=F32)
    zs5_ref[...] = _to_chunk_rows(z_s5)
    g_ref[...] = _sigmoid(jnp.dot(xb, wg_ref[...], preferred_element_type=F32)).astype(g_ref.dtype)


def _inproj(x, w_rw, w_mla, w_s5, w_g, tm):
    t = x.shape[0]
    row = lambda n: pl.BlockSpec((tm, n), lambda i: (i, 0))
    return pl.pallas_call(
        _inproj_kernel,
        grid=(t // tm,),
        in_specs=[row(D_MODEL), _full(w_rw), _full(w_mla), _full(w_s5), _full(w_g)],
        out_specs=[row(RW_IN), row(MLA_IN_PAD), row(S5_W), row(N_BRANCH * D_MODEL)],
        out_shape=[jax.ShapeDtypeStruct((t, RW_IN), F32), jax.ShapeDtypeStruct((t, MLA_IN_PAD), F32),
                   jax.ShapeDtypeStruct((t, S5_W), F32), jax.ShapeDtypeStruct((t, N_BRANCH * D_MODEL), BF16)],
        compiler_params=_params("parallel"),
        name="inproj",
    )(x, w_rw, w_mla, w_s5, w_g)


def _rwkv_kernel(nblk, tb, nb,
                 zf_ref, zfp_ref, zfn_ref, zb_ref, zbp_ref, zbn_ref,
                 mu_ref, w0_ref, w2_ref, a0_ref, a2_ref, g2_ref, kk_ref, ka_ref, rk_ref,
                 of_ref, bonus_ref, gate_ref, ob_ref, r_s, v_s, kn_s, lw_s, b_s, kd_s, state):
    step = pl.program_id(1)
    c = RW_CHUNK
    w = RW_W
    hr = lax.broadcasted_iota(jnp.int32, (w, w), 0) // RW_HEAD
    hc = lax.broadcasted_iota(jnp.int32, (w, w), 1) // RW_HEAD
    same_head = hr == hc
    head_ones = same_head.astype(F32)

    def prepare(d, bi, z_ref, zp_ref, zn_ref, blk):
        z = z_ref[bi]
        prev_row = jnp.where(blk == 0, 0.0, zp_ref[bi, 7:8, :])
        next_row = jnp.where(blk == nblk - 1, 0.0, zn_ref[bi, 0:1, :])
        rows = lax.broadcasted_iota(jnp.int32, z.shape, 0)
        z_prev = jnp.where(rows == 0, prev_row, pltpu.roll(z, 1, axis=0))
        z_next = jnp.where(rows == tb - 1, next_row, pltpu.roll(z, tb - 1, axis=0))
        z = z + mu_ref[...] * (0.5 * (z_prev + z_next) - z)
        r = z[:, 0:w]
        k = z[:, w:2 * w]
        v = z[:, 2 * w:3 * w]
        xw = z[:, 3 * w:3 * w + RW_LORA_W]
        xa = z[:, 3 * w + RW_LORA_W:3 * w + RW_LORA_W + RW_LORA_A]
        xg = z[:, 3 * w + RW_LORA_W + RW_LORA_A:]
        kk = k * kk_ref[...]
        kk_ss = _mm_split(kk * kk, head_ones, 2, 1)
        kk = kk * lax.rsqrt(jnp.maximum(kk_ss, 1e-12))
        y = w0_ref[d] + _mm_split(jnp.tanh(xw), w2_ref[d], 2, 2)
        lw = -math.exp(-0.5) * _sigmoid(y)
        a = _sigmoid(a0_ref[d] + _mm(xa, a2_ref[d]))
        r_s[d, bi] = r
        v_s[d, bi] = v
        kn_s[d, bi] = kk
        lw_s[d, bi] = lw
        b_s[d, bi] = kk * a
        kd_s[d, bi] = k * (1.0 + (a - 1.0) * ka_ref[...])
        if d == 0:
            rk = _mm_split(r * k * rk_ref[...], head_ones, 2, 1)
            bonus_ref[bi] = rk * v
            gate_ref[bi] = _mm(_sigmoid(xg), g2_ref[...])

    for bi in range(nb):
        prepare(0, bi, zf_ref, zfp_ref, zfn_ref, step)
        prepare(1, bi, zb_ref, zbp_ref, zbn_ref, nblk - 1 - step)

    @pl.when(step == 0)
    def _():
        state[...] = jnp.zeros_like(state)

    ti = lax.broadcasted_iota(jnp.int32, (c, c), 0)
    si = lax.broadcasted_iota(jnp.int32, (c, c), 1)
    tw = lax.broadcasted_iota(jnp.int32, (c, w), 0)
    sw = lax.broadcasted_iota(jnp.int32, (c, w), 1) % RW_HEAD
    eye = (tw == sw).astype(F32)

    def stack(x):
        return jnp.where(same_head, jnp.concatenate([x] * RW_HEADS, axis=0), 0.0).astype(BF16)

    nchunk = tb // c

    def chunk(reverse, bi, cpos):
        d = 1 if reverse else 0
        o_ref = ob_ref if reverse else of_ref
        cum_mat = ((si >= ti) if reverse else (si <= ti)).astype(F32)
        strict = (tw < sw) if reverse else (tw > sw)
        incl = (tw <= sw) if reverse else (tw >= sw)
        sl = pl.ds(pl.multiple_of(cpos * c, c), c)
        lwc = lw_s[d, bi, sl, :]
        l_in = _mm_split(cum_mat, lwc, 1, 3)
        l_tot = jnp.sum(lwc, axis=0, keepdims=True)
        e_in = jnp.exp(l_in)
        e_neg = jnp.exp(-l_in)
        e_tot = jnp.exp(l_tot)
        kap = kn_s[d, bi, sl, :] * jnp.exp(l_in - lwc)
        bt = b_s[d, bi, sl, :] * e_neg
        kt = kd_s[d, bi, sl, :] * e_neg
        rt = (r_s[d, bi, sl, :] * e_in).astype(BF16)
        v = v_s[d, bi, sl, :]
        kap_w, bt_w, kt_w, v_w = stack(kap), stack(bt), stack(kt), stack(v)
        kap = kap.astype(BF16)
        end_w = jnp.concatenate([bt * e_tot, kt * e_tot], axis=0).astype(BF16)
        s_bd = state[d, bi]
        s_w = s_bd.astype(BF16)
        yield
        kap_rt = jnp.concatenate([kap, rt], axis=0)
        am_b = _mm_nt(kap_rt, bt_w)
        a_b = jnp.where(strict, am_b[0:c], 0.0)
        m_b = jnp.where(incl, am_b[c:2 * c], 0.0)
        yield
        am_k = _mm_nt(kap_rt, kt_w)
        a_k = jnp.where(strict, am_k[0:c], 0.0)
        m_k = jnp.where(incl, am_k[c:2 * c], 0.0)
        yield
        assert c == 2 ** int(math.log2(c)) and int(math.log2(c)) % 2 == 0
        powers = [-a_b]
        pairs = []
        akv = _mm(a_k, v_w)
        for level in range(1, int(math.log2(c))):
            sq_w = stack(powers[-1])
            powers.append(_mm(powers[-1], sq_w))
            if level % 2 == 0:
                lo = eye + powers[level - 2]
                pairs.append(lo + _mm(lo, sq_w))
            yield
        lo = eye + powers[-2]
        pairs.append(lo + _mm(lo, stack(powers[-1])))
        tinv = pairs[0]
        for pr in pairs[1:-1]:
            tinv = _mm(tinv, stack(pr))
        yield
        tinv = _mm(tinv, stack(pairs[-1]))
        yield
        pm = _mm(tinv, kap_w)
        qm = _mm(tinv, stack(akv))
        yield
        u = -(_mm_nt(pm, s_w) + qm)
        yield
        o = _mm_nt(rt, s_w) + _mm(m_b, stack(u)) + _mm(m_k, v_w)
        upd = _mm_tn(jnp.concatenate([u, v], axis=0), end_w)
        yield
        o_ref[bi, sl, :] = o
        state[d, bi] = s_bd * e_tot + jnp.where(same_head, upd, 0.0)

    def all_scans(ci, carry):
        scans = [chunk(False, bi, ci) for bi in range(nb)] + [chunk(True, bi, nchunk - 1 - ci) for bi in range(nb)]
        while scans:
            scans = [g for g in scans if next(g, StopIteration) is not StopIteration]
        return carry

    lax.fori_loop(0, nchunk, all_scans, 0)


def _rwkv_scan(z, mu, w0, w2, a0, a2, g2, k_k, k_a, r_k, tb):
    bsz, seq, _ = z.shape
    nblk = seq // tb
    nb = 2 if bsz % 2 == 0 else 1
    z8 = z.reshape(bsz, seq // 8, 8, RW_IN)
    t8 = tb // 8

    def specs(pos):
        z_spec = pl.BlockSpec((nb, tb, RW_IN), lambda b, i: (b, pos(i), 0))
        zp_spec = pl.BlockSpec((nb, None, 8, RW_IN), lambda b, i: (b, jnp.maximum(pos(i) * t8 - 1, 0), 0, 0))
        zn_spec = pl.BlockSpec((nb, None, 8, RW_IN),
                               lambda b, i: (b, jnp.minimum((pos(i) + 1) * t8, seq // 8 - 1), 0, 0))
        o_spec = pl.BlockSpec((nb, tb, RW_W), lambda b, i: (b, pos(i), 0))
        return [z_spec, zp_spec, zn_spec], o_spec

    in_f, o_f = specs(lambda i: i)
    in_b, o_b = specs(lambda i: nblk - 1 - i)
    o_shape = jax.ShapeDtypeStruct((bsz, seq, RW_W), F32)
    weights = [mu, w0, w2, a0, a2, g2, k_k, k_a, r_k]
    return pl.pallas_call(
        functools.partial(_rwkv_kernel, nblk, tb, nb),
        grid=(bsz // nb, nblk),
        in_specs=in_f + in_b + [_full(wt) for wt in weights],
        out_specs=[o_f, o_f, o_f, o_b],
        out_shape=[o_shape] * 4,
        scratch_shapes=[pltpu.VMEM((2, nb, tb, RW_W), F32)] * 6 + [pltpu.VMEM((2, nb, RW_W, RW_W), F32)],
        compiler_params=_params("parallel", "arbitrary"),
        name="rwkv_scan",
    )(z, z8, z8, z, z8, z8, *weights)


def _rope_tables(seq):
    half = MLA_ROPE // 2
    inv = (ROPE_THETA ** (-np.arange(half, dtype=np.float32) / half)).astype(np.float32)
    ang = np.arange(seq, dtype=np.float32)[:, None] * inv[None, :]
    cos = np.cos(ang).astype(np.float32)
    sin = np.sin(ang).astype(np.float32)
    ct = np.zeros((seq, MLA_QK_PAD), np.float32)
    st = np.zeros((seq, MLA_QK_PAD), np.float32)
    ct[:, :MLA_NOPE] = 1.0
    ct[:, MLA_NOPE:MLA_NOPE + half] = cos
    ct[:, MLA_NOPE + half:MLA_NOPE + 2 * half] = cos
    st[:, MLA_NOPE:MLA_NOPE + half] = -sin
    st[:, MLA_NOPE + half:MLA_NOPE + 2 * half] = sin
    return jnp.asarray(ct), jnp.asarray(st), jnp.asarray(ct.T), jnp.asarray(st.T)


def _rope(x, cos_t, sin_t):
    half = MLA_ROPE // 2
    lane = lax.broadcasted_iota(jnp.int32, x.shape, 1)
    swapped = jnp.where(lane < MLA_NOPE + half,
                        pltpu.roll(x, MLA_QK_PAD - half, axis=1), pltpu.roll(x, half, axis=1))
    return x * cos_t + swapped * sin_t


def _rope_rows(x, cos_t, sin_t):
    half = MLA_ROPE // 2
    a, b = MLA_NOPE, MLA_NOPE + half
    swapped = jnp.concatenate([x[0:a], x[b:b + half], x[a:b], x[b + half:]], axis=0)
    return x * cos_t + swapped * sin_t


def _mla_qkv_kernel(z_ref, cos_ref, sin_ref, cosr_ref, sinr_ref, qn_ref, wuq_ref, kvn_ref, wuk_ref, wuv_ref,
                    q_ref, k_ref, v_ref):
    z = z_ref[...]
    cos_t = cos_ref[...]
    sin_t = sin_ref[...]
    cos_r = cosr_ref[...]
    sin_r = sinr_ref[...]
    c_q = z[:, :MLA_Q_LORA]
    c_kv = z[:, MLA_Q_LORA:MLA_Q_LORA + MLA_KV_LORA]
    k_rope = _rope(z[:, MLA_Q_LORA + MLA_KV_LORA:], cos_t, sin_t)
    c_q = (c_q * lax.rsqrt(jnp.mean(c_q * c_q, -1, keepdims=True) + RMS_EPS) * qn_ref[...]).astype(BF16)
    c_kv = (c_kv * lax.rsqrt(jnp.mean(c_kv * c_kv, -1, keepdims=True) + RMS_EPS) * kvn_ref[...]).astype(BF16)
    nt = (((1,), (1,)), ((), ()))
    for h in range(MLA_HEADS):
        q_t = lax.dot_general(wuq_ref[h], c_q, nt, preferred_element_type=F32)
        q_ref[h] = (_rope_rows(q_t, cos_r, sin_r) * (MLA_SCALE * LOG2_E)).astype(BF16)
        kh = jnp.dot(c_kv, wuk_ref[h], preferred_element_type=F32)
        k_ref[h] = (kh + k_rope).astype(BF16)
        v_t = lax.dot_general(wuv_ref[h], c_kv, nt, preferred_element_type=F32)
        v_ref[h] = v_t.astype(BF16)


def _mla_qkv(z, rope, q_norm, w_uq, kv_norm, w_uk, w_uv, seq, tm):
    t = z.shape[0]
    nseq = seq // tm
    tab = pl.BlockSpec((tm, MLA_QK_PAD), lambda i: (i % nseq, 0))
    tab_r = pl.BlockSpec((MLA_QK_PAD, tm), lambda i: (0, i % nseq))
    cols = lambda n: pl.BlockSpec((MLA_HEADS, n, tm), lambda i: (0, 0, i))
    return pl.pallas_call(
        _mla_qkv_kernel,
        grid=(t // tm,),
        in_specs=[pl.BlockSpec((tm, MLA_IN_PAD), lambda i: (i, 0)), tab, tab, tab_r, tab_r,
                  _full(q_norm), _full(w_uq), _full(kv_norm), _full(w_uk), _full(w_uv)],
        out_specs=[cols(MLA_QK_PAD), pl.BlockSpec((MLA_HEADS, tm, MLA_QK_PAD), lambda i: (0, i, 0)),
                   cols(MLA_V)],
        out_shape=[jax.ShapeDtypeStruct((MLA_HEADS, MLA_QK_PAD, t), BF16),
                   jax.ShapeDtypeStruct((MLA_HEADS, t, MLA_QK_PAD), BF16),
                   jax.ShapeDtypeStruct((MLA_HEADS, MLA_V, t), BF16)],
        compiler_params=_params("parallel"),
        name="mla_qkv",
    )(z, *rope, q_norm, w_uq, kv_norm, w_uk, w_uv)


def _mla_attn_kernel(nk, q_ref, k_ref, v_ref, o_ref, m_s, l_s, acc_s, bm_s, bl_s, pv_s):
    j = pl.program_id(2)

    def scores(h):
        return jnp.dot(k_ref[h], q_ref[h], preferred_element_type=F32)

    @pl.when(j == 0)
    def _():
        m_s[...] = jnp.full(m_s.shape, -jnp.inf, F32)
        l_s[...] = jnp.zeros_like(l_s)
        acc_s[...] = jnp.zeros_like(acc_s)
        for h in range(MLA_HEADS):
            bm_s[h:h + 1, :] = jnp.max(scores(h), axis=0, keepdims=True)

    @pl.when(j > 0)
    def _():
        s_next = scores(0)
        for h in range(MLA_HEADS):
            s = s_next
            if h + 1 < MLA_HEADS:
                s_next = scores(h + 1)
            bm_s[h:h + 1, :] = jnp.max(s, axis=0, keepdims=True)
            p = jnp.exp2(s - m_s[h:h + 1, :])
            bl_s[h:h + 1, :] = jnp.sum(p, axis=0, keepdims=True)
            pv_s[h] = jnp.dot(v_ref[h], p.astype(BF16), preferred_element_type=F32)

    stale_shift_ok = jnp.max(bm_s[...] - m_s[...]) < MLA_SHIFT_SLACK

    @pl.when(stale_shift_ok)
    def _():
        for h in range(MLA_HEADS):
            m_prev = m_s[h:h + 1, :]
            m_new = jnp.maximum(m_prev, bm_s[h:h + 1, :])
            alpha = jnp.exp2(m_prev - m_new)
            acc_s[h] = (acc_s[h] + pv_s[h]) * alpha
            l_s[h:h + 1, :] = (l_s[h:h + 1, :] + bl_s[h:h + 1, :]) * alpha
            m_s[h:h + 1, :] = m_new

    @pl.when(jnp.logical_not(stale_shift_ok))
    def _():
        for h in range(MLA_HEADS):
            s = scores(h)
            m_prev = m_s[h:h + 1, :]
            m_new = jnp.maximum(m_prev, bm_s[h:h + 1, :])
            alpha = jnp.exp2(m_prev - m_new)
            p = jnp.exp2(s - m_new)
            l_s[h:h + 1, :] = alpha * l_s[h:h + 1, :] + jnp.sum(p, axis=0, keepdims=True)
            acc_s[h] = alpha * acc_s[h] + jnp.dot(v_ref[h], p.astype(BF16), preferred_element_type=F32)
            m_s[h:h + 1, :] = m_new

    @pl.when(j == nk - 1)
    def _():
        heads = [acc_s[h] / l_s[h:h + 1, :] for h in range(MLA_HEADS)]
        o_ref[...] = jnp.concatenate(heads, axis=0).T


def _mla_attn(q, k, v, bsz, seq, tq, tk):
    nq = seq // tq
    nk = seq // tk
    t = bsz * seq
    return pl.pallas_call(
        functools.partial(_mla_attn_kernel, nk),
        grid=(bsz, nq, nk),
        in_specs=[pl.BlockSpec((MLA_HEADS, MLA_QK_PAD, tq), lambda b, i, j: (0, 0, b * nq + i)),
                  pl.BlockSpec((MLA_HEADS, tk, MLA_QK_PAD), lambda b, i, j: (0, b * nk + j, 0)),
                  pl.BlockSpec((MLA_HEADS, MLA_V, tk), lambda b, i, j: (0, 0, b * nk + j))],
        out_specs=pl.BlockSpec((tq, MLA_W), lambda b, i, j: (b * nq + i, 0)),
        out_shape=jax.ShapeDtypeStruct((t, MLA_W), F32),
        scratch_shapes=[pltpu.VMEM((MLA_HEADS, tq), F32), pltpu.VMEM((MLA_HEADS, tq), F32),
                        pltpu.VMEM((MLA_HEADS, MLA_V, tq), F32),
                        pltpu.VMEM((MLA_HEADS, tq), F32), pltpu.VMEM((MLA_HEADS, tq), F32),
                        pltpu.VMEM((MLA_HEADS, MLA_V, tq), F32)],
        compiler_params=_params("parallel", "parallel", "arbitrary"),
        name="mla_attn",
    )(q, k, v)


def _s5_param_kernel(lre_ref, lim_ref, ldt_ref, btre_ref, btim_ref, cre_ref, cim_ref, ek_ref, ew_ref, ee_ref,
                     kt_ref, wre_ref, wim_ref, ere_ref, eim_ref, are_ref, aim_ref):
    lam_re = lre_ref[...]
    lam_im = lim_ref[...]
    dt = jnp.exp(ldt_ref[...])

    def rows_of(x):
        return jnp.broadcast_to(x[:, None, :], (S5_CHUNK, S5_GROUP, S5_STATE)).reshape(S5_CW, S5_STATE)

    def power(e):
        mag = jnp.exp(lam_re * dt * e)
        ang = lam_im * dt * e
        return mag * jnp.cos(ang), mag * jnp.sin(ang)

    a_re, a_im = power(jnp.ones_like(lam_re))
    den = lam_re * lam_re + lam_im * lam_im
    nr = a_re - 1.0
    coef_re = (nr * lam_re + a_im * lam_im) / den
    coef_im = (a_im * lam_re - nr * lam_im) / den
    bt_re = btre_ref[...]
    bt_im = btim_ref[...]
    bb_re = coef_re * bt_re - coef_im * bt_im
    bb_im = coef_re * bt_im + coef_im * bt_re
    tile = lambda x: jnp.concatenate([x] * S5_CHUNK, axis=0)
    c_re = tile(cre_ref[...])
    c_im = tile(cim_ref[...])
    pk_re, pk_im = map(rows_of, power(ek_ref[...]))
    ck_re = c_re * pk_re - c_im * pk_im
    ck_im = c_re * pk_im + c_im * pk_re
    nt = (((1,), (1,)), ((), ()))
    kt_ref[...] = (lax.dot_general(bb_re, ck_re, nt, precision=HIGHEST, preferred_element_type=F32)
                   - lax.dot_general(bb_im, ck_im, nt, precision=HIGHEST, preferred_element_type=F32))
    pw_re, pw_im = map(rows_of, power(ew_ref[...]))
    bbt_re = tile(bb_re)
    bbt_im = tile(bb_im)
    wre_ref[...] = pw_re * bbt_re - pw_im * bbt_im
    wim_ref[...] = pw_re * bbt_im + pw_im * bbt_re
    pe_re, pe_im = map(rows_of, power(ee_ref[...]))
    ere_ref[...] = c_re * pe_re - c_im * pe_im
    eim_ref[...] = -(c_re * pe_im + c_im * pe_re)
    ac_re, ac_im = power(jnp.full_like(lam_re, float(S5_CHUNK)))
    are_ref[...] = ac_re
    aim_ref[...] = ac_im


def _s5_params(lam_re, lam_im, log_dt, b_re, b_im, c_re, c_im):
    n = 2 * S5_GROUPS
    p, c, cw = S5_STATE, S5_GROUP, S5_CW
    vec = lambda a: a.reshape(n, 1, p)
    ldt = jnp.broadcast_to(log_dt.reshape(n, 1, 1), (n, 1, p))
    bt = lambda a: a.transpose(0, 1, 3, 2).reshape(n, c, p)
    cc = lambda a: a.reshape(n, c, p)
    steps = np.arange(S5_CHUNK, dtype=np.float32)
    rows = lambda e: jnp.asarray(np.broadcast_to(e[:, :, None], (2, S5_CHUNK, p)).copy())
    e_k = rows(np.stack([steps, steps]))
    e_w = rows(np.stack([S5_CHUNK - 1 - steps, steps]))
    e_e = rows(np.stack([steps + 1, S5_CHUNK - steps]))
    per = lambda *shape: pl.BlockSpec((None,) + shape, lambda i: (i,) + (0,) * len(shape))
    per_dir = pl.BlockSpec((None, S5_CHUNK, p), lambda i: (i // S5_GROUPS, 0, 0))
    out = lambda *shape: jax.ShapeDtypeStruct((n,) + shape, F32)
    return pl.pallas_call(
        _s5_param_kernel,
        grid=(n,),
        in_specs=[per(1, p)] * 3 + [per(c, p)] * 4 + [per_dir] * 3,
        out_specs=[per(c, cw)] + [per(cw, p)] * 4 + [per(1, p)] * 2,
        out_shape=[out(c, cw)] + [out(cw, p)] * 4 + [out(1, p)] * 2,
        compiler_params=_params("parallel"),
        name="s5_params",
    )(vec(lam_re), vec(lam_im), ldt, bt(b_re), bt(b_im), cc(c_re), cc(c_im), e_k, e_w, e_e)


def _s5_toeplitz(kt):
    g, c, n = S5_GROUPS, S5_GROUP, S5_CHUNK
    k = kt.reshape(2, g, c, n, c)
    s_in = np.arange(n)[:, None]
    s_out = np.arange(n)[None, :]
    lag_f = np.clip(s_out - s_in, 0, n - 1)
    lag_b = np.clip(s_in - s_out, 0, n - 1)
    m_f = jnp.asarray((s_out >= s_in).astype(np.float32))[None, None, :, :, None]
    m_b = jnp.asarray((s_in >= s_out).astype(np.float32))[None, None, :, :, None]
    m = k[0][:, :, lag_f, :] * m_f + k[1][:, :, lag_b, :] * m_b
    return m.transpose(0, 2, 1, 3, 4).reshape(g, n * c, n * c)


def _s5_operators(lam_re, lam_im, log_dt, b_re, b_im, c_re, c_im):
    kt, w_re, w_im, e_re, e_im, a_re, a_im = _s5_params(lam_re, lam_im, log_dt, b_re, b_im, c_re, c_im)
    g = S5_GROUPS
    parts = lambda re, im: jnp.stack([re[:g], im[:g], re[g:], im[g:]], axis=1)
    return (_s5_toeplitz(kt).astype(BF16), parts(w_re, w_im).astype(BF16), parts(e_re, e_im).astype(BF16),
            parts(a_re[:, 0], a_im[:, 0]))


def _s5_chunk_kernel(nk, rows, u_ref, m_ref, w_ref, e_ref, a_ref, y_ref, wfr_s, wfi_s, wbr_s, wbi_s):
    u = u_ref[...].astype(BF16)
    y_ref[...] = jnp.dot(u, m_ref[...], preferred_element_type=F32)
    for i, dst in enumerate((wfr_s, wfi_s, wbr_s, wbi_s)):
        dst[...] = jnp.dot(u, w_ref[i], preferred_element_type=F32)
    a = [jnp.broadcast_to(a_ref[i:i + 1, :], (rows, S5_STATE)) for i in range(4)]

    def step(k, carry):
        fr, fi, br, bi = carry
        sf = pl.ds(k, rows, stride=nk)
        sb = pl.ds(nk - 1 - k, rows, stride=nk)
        wfr, wfi, wbr, wbi = wfr_s[sf, :], wfi_s[sf, :], wbr_s[sb, :], wbi_s[sb, :]
        wfr_s[sf, :] = fr
        wfi_s[sf, :] = fi
        wbr_s[sb, :] = br
        wbi_s[sb, :] = bi
        return (a[0] * fr - a[1] * fi + wfr, a[0] * fi + a[1] * fr + wfi,
                a[2] * br - a[3] * bi + wbr, a[2] * bi + a[3] * br + wbi)

    zero = jnp.zeros((rows, S5_STATE), F32)
    lax.fori_loop(0, nk, step, (zero, zero, zero, zero))
    carry_in = 0.0
    for i, src in enumerate((wfr_s, wfi_s, wbr_s, wbi_s)):
        carry_in = carry_in + _mm_nt(src[...], e_ref[i])
    y_ref[...] += carry_in


def _s5_scan(z_rows, bsz, seq, operators):
    m_mat, w_mat, e_mat, a_vec = operators
    nk = seq // S5_CHUNK
    r = bsz * nk
    g = S5_GROUPS
    u = z_rows.reshape(r, g * S5_CW)
    grp = lambda *shape: pl.BlockSpec((None,) + shape, lambda i: (i,) + (0,) * len(shape))
    seq_rows = pl.BlockSpec((r, S5_CW), lambda i: (0, i))
    y = pl.pallas_call(
        functools.partial(_s5_chunk_kernel, nk, bsz),
        grid=(g,),
        in_specs=[seq_rows, grp(S5_CW, S5_CW), grp(4, S5_CW, S5_STATE), grp(4, S5_CW, S5_STATE), grp(4, S5_STATE)],
        out_specs=seq_rows,
        out_shape=jax.ShapeDtypeStruct((r, g * S5_CW), F32),
        scratch_shapes=[pltpu.VMEM((r, S5_STATE), F32)] * 4,
        compiler_params=_params("parallel"),
        name="s5_scan",
    )(u, m_mat, w_mat, e_mat, a_vec)
    return y.reshape(bsz * seq, S5_W)


def _s5_post_kernel(y_ref, u_ref, d_ref, w_ref, b_ref, o_ref):
    tm = y_ref.shape[0]
    y = y_ref[...].reshape(tm // S5_GROUPS, S5_GROUPS, S5_CW)
    u = u_ref[...].reshape(tm // S5_GROUPS, S5_GROUPS, S5_CW)
    y = _to_chunk_rows((y + d_ref[...] * u).reshape(tm, S5_CW))
    y = jax.nn.gelu(y, approximate=True)
    o_ref[...] = y * _sigmoid(_mm(y, w_ref[...]) + b_ref[...])


def _s5_post(y, u, d_rows, glu_w, glu_b, tm):
    t = u.shape[0]
    row = pl.BlockSpec((tm, S5_W), lambda i: (i, 0))
    return pl.pallas_call(
        _s5_post_kernel,
        grid=(t // tm,),
        in_specs=[row, row, _full(d_rows), _full(glu_w), _full(glu_b)],
        out_specs=row,
        out_shape=jax.ShapeDtypeStruct((t, S5_W), F32),
        compiler_params=_params("parallel"),
        name="s5_post",
    )(y, u, d_rows, glu_w, glu_b)


def _merge_kernel(x_ref, of_ref, ob_ref, bonus_ref, rgate_ref, mla_ref, s5_ref, g_ref,
                  gng_ref, gnb_ref, wrw_ref, wmla_ref, ws5_ref, wout_ref, lng_ref, lnb_ref, o_ref):
    w = RW_W
    hr = lax.broadcasted_iota(jnp.int32, (w, w), 0) // RW_HEAD
    hc = lax.broadcasted_iota(jnp.int32, (w, w), 1) // RW_HEAD
    head_mean = (hr == hc).astype(F32) * (1.0 / RW_HEAD)
    o = of_ref[...] + ob_ref[...]
    mean = _mm_split(o, head_mean, 2, 1)
    oc = o - mean
    var = _mm_split(oc * oc, head_mean, 2, 1)
    o = oc * lax.rsqrt(var + RW_GN_EPS) * gng_ref[...] + gnb_ref[...]
    y_rw = _mm((o + bonus_ref[...]) * rgate_ref[...], wrw_ref[...])
    y_mla = _mm(mla_ref[...], wmla_ref[...])
    y_s5 = _mm(s5_ref[...], ws5_ref[...])
    d = D_MODEL
    merged = g_ref[:, 0:d] * y_rw + g_ref[:, d:2 * d] * y_mla + g_ref[:, 2 * d:3 * d] * y_s5
    o_ref[...] = _layer_norm(DN_ALPHA * x_ref[...] + _mm(merged, wout_ref[...]), lng_ref[...], lnb_ref[...])


def _merge(x, o_f, o_b, bonus, rgate, o_mla, y_s5, gates, gn_g, gn_b, w_rw, w_mla, w_s5, w_out, ln_g, ln_b, tm):
    t = x.shape[0]
    row = lambda n: pl.BlockSpec((tm, n), lambda i: (i, 0))
    weights = [gn_g, gn_b, w_rw, w_mla, w_s5, w_out, ln_g, ln_b]
    return pl.pallas_call(
        _merge_kernel,
        grid=(t // tm,),
        in_specs=[row(D_MODEL), row(RW_W), row(RW_W), row(RW_W), row(RW_W), row(MLA_W), row(S5_W),
                  row(N_BRANCH * D_MODEL)] + [_full(wt) for wt in weights],
        out_specs=row(D_MODEL),
        out_shape=jax.ShapeDtypeStruct((t, D_MODEL), F32),
        compiler_params=_params("parallel"),
        name="merge_ln1",
    )(x, o_f, o_b, bonus, rgate, o_mla, y_s5, gates, *weights)


def _mlp_kernel(nf, x_ref, w1_ref, w2_ref, lng_ref, lnb_ref, o_ref, acc_s):
    j = pl.program_id(1)

    @pl.when(j == 0)
    def _():
        acc_s[...] = jnp.zeros_like(acc_s)

    h = jnp.maximum(_mm(x_ref[...], w1_ref[...]), 0.0)
    acc_s[...] += _mm(h * h, w2_ref[...])

    @pl.when(j == nf - 1)
    def _():
        o_ref[...] = _layer_norm(DN_ALPHA * x_ref[...] + acc_s[...], lng_ref[...], lnb_ref[...])


def _mlp(x, w1, w2, ln_g, ln_b, tm, tf):
    t = x.shape[0]
    nf = D_FF // tf
    return pl.pallas_call(
        functools.partial(_mlp_kernel, nf),
        grid=(t // tm, nf),
        in_specs=[pl.BlockSpec((tm, D_MODEL), lambda i, j: (i, 0)),
                  pl.BlockSpec((D_MODEL, tf), lambda i, j: (0, j)),
                  pl.BlockSpec((tf, D_MODEL), lambda i, j: (j, 0)),
                  _full(ln_g), _full(ln_b)],
        out_specs=pl.BlockSpec((tm, D_MODEL), lambda i, j: (i, 0)),
        out_shape=jax.ShapeDtypeStruct((t, D_MODEL), F32),
        scratch_shapes=[pltpu.VMEM((tm, D_MODEL), F32)],
        compiler_params=_params("parallel", "arbitrary"),
        name="mlp_ln2",
    )(x, w1, w2, ln_g, ln_b)


def _tile(n, pref):
    t = min(n, pref)
    assert n % t == 0, (n, pref)
    return t


def _prep_layer(w_in, rw_mu, rw_w0, rw_w2, rw_a0, rw_a2, rw_g2, rw_k_k, rw_k_a, rw_r_k, rw_gn_g, rw_gn_b, rw_proj,
                mla_q_norm, mla_w_uq, mla_kv_norm, mla_w_ukv, mla_proj,
                s5_lam_re, s5_lam_im, s5_log_dt, s5_b_re, s5_b_im, s5_c_re, s5_c_im, s5_d, s5_glu_w, s5_glu_b,
                s5_proj, w_out, ln1_g, ln1_b, mlp_w1, mlp_w2, ln2_g, ln2_b):
    row = lambda a: a.reshape(1, -1)
    p = {}
    p["w_rw"] = w_in[:, :OFF_MLA].astype(BF16)
    w_mla = w_in[:, OFF_MLA:OFF_S5]
    zeros = lambda n: jnp.zeros((D_MODEL, n), F32)
    p["w_mla"] = jnp.concatenate([w_mla[:, :MLA_Q_LORA + MLA_KV_LORA], zeros(MLA_NOPE),
                                  w_mla[:, MLA_Q_LORA + MLA_KV_LORA:],
                                  zeros(MLA_QK_PAD - MLA_NOPE - MLA_ROPE)], axis=1).astype(BF16)
    p["w_s5"] = w_in[:, OFF_S5:OFF_GATE].astype(BF16)
    p["w_gate"] = w_in[:, OFF_GATE:].astype(BF16)
    p["rw"] = [row(rw_mu)]
    p["rw_dir"] = [rw_w0[:, None, :], rw_w2, rw_a0[:, None, :], rw_a2]
    p["rw_shared"] = [rw_g2, row(rw_k_k), row(rw_k_a), row(rw_r_k)]
    p["rw_gn"] = [row(rw_gn_g), row(rw_gn_b)]
    p["rw_proj"] = rw_proj.astype(BF16)
    uq = mla_w_uq.reshape(MLA_Q_LORA, MLA_HEADS, MLA_NOPE + MLA_ROPE)
    uq = jnp.pad(uq, ((0, 0), (0, 0), (0, MLA_QK_PAD - MLA_NOPE - MLA_ROPE)))
    p["w_uq"] = uq.transpose(1, 2, 0).astype(BF16)
    ukv = mla_w_ukv.reshape(MLA_KV_LORA, MLA_HEADS, MLA_NOPE + MLA_V)
    uk = jnp.pad(ukv[:, :, :MLA_NOPE], ((0, 0), (0, 0), (0, MLA_QK_PAD - MLA_NOPE)))
    p["w_uk"] = uk.transpose(1, 0, 2).astype(BF16)
    p["w_uv"] = ukv[:, :, MLA_NOPE:].transpose(1, 2, 0).astype(BF16)
    p["q_norm"] = row(mla_q_norm)
    p["kv_norm"] = row(mla_kv_norm)
    p["mla_proj"] = mla_proj.astype(BF16)
    p["s5_ops"] = _s5_operators(s5_lam_re, s5_lam_im, s5_log_dt, s5_b_re, s5_b_im, s5_c_re, s5_c_im)
    d_rows = jnp.broadcast_to(s5_d.reshape(S5_GROUPS, 1, S5_GROUP), (S5_GROUPS, S5_CHUNK, S5_GROUP))
    p["s5_post"] = [d_rows.reshape(S5_GROUPS, S5_CW), s5_glu_w.astype(BF16), row(s5_glu_b)]
    p["s5_proj"] = s5_proj.astype(BF16)
    p["w_out"] = w_out.astype(BF16)
    p["ln1"] = [row(ln1_g), row(ln1_b)]
    p["w1"] = mlp_w1.astype(BF16)
    p["w2"] = mlp_w2.astype(BF16)
    p["ln2"] = [row(ln2_g), row(ln2_b)]
    return p


def _layer(x, p, bsz, seq, rope):
    t = bsz * seq
    tm = _tile(t, 512)
    z_rw, z_mla, z_s5, gates = _inproj(x, p["w_rw"], p["w_mla"], p["w_s5"], p["w_gate"], _tile(t, 256))

    tb = _tile(seq, 512)
    z3 = z_rw.reshape(bsz, seq, RW_IN)
    o_f, bonus, rgate, o_b = _rwkv_scan(z3, *p["rw"], *p["rw_dir"], *p["rw_shared"], tb=tb)
    flat = lambda a: a.reshape(t, RW_W)

    q, k, v = _mla_qkv(z_mla, rope, p["q_norm"], p["w_uq"], p["kv_norm"], p["w_uk"], p["w_uv"],
                       seq, _tile(seq, 512))
    o_mla = _mla_attn(q, k, v, bsz, seq, _tile(seq, 1024), _tile(seq, 1024))

    y_s5 = _s5_post(_s5_scan(z_s5, bsz, seq, p["s5_ops"]), z_s5, *p["s5_post"], _tile(t, 1024))

    x1 = _merge(x, flat(o_f), flat(o_b), flat(bonus), flat(rgate), o_mla, y_s5, gates,
                *p["rw_gn"], p["rw_proj"], p["mla_proj"], p["s5_proj"], p["w_out"], *p["ln1"], tm)
    return _mlp(x1, p["w1"], p["w2"], *p["ln2"], _tile(t, 1024), 1024)


def _trunk(x, layers):
    bsz, seq, _ = x.shape
    assert seq % RW_CHUNK == 0 and seq % S5_CHUNK == 0
    rope = _rope_tables(seq)
    h = x.reshape(bsz * seq, D_MODEL)
    for p in layers:
        h = _layer(h, p, bsz, seq, rope)
    return h.reshape(bsz, seq, D_MODEL)


def kernel(x_prompt, x_sample, w_in, rw_mu, rw_w0, rw_w2, rw_a0, rw_a2, rw_g2, rw_k_k, rw_k_a, rw_r_k, rw_gn_g, rw_gn_b, rw_proj, mla_q_norm, mla_w_uq, mla_kv_norm, mla_w_ukv, mla_proj, s5_lam_re, s5_lam_im, s5_log_dt, s5_b_re, s5_b_im, s5_c_re, s5_c_im, s5_d, s5_glu_w, s5_glu_b, s5_proj, w_out, ln1_g, ln1_b, mlp_w1, mlp_w2, ln2_g, ln2_b):
    weights = (w_in, rw_mu, rw_w0, rw_w2, rw_a0, rw_a2, rw_g2, rw_k_k, rw_k_a, rw_r_k, rw_gn_g, rw_gn_b, rw_proj,
               mla_q_norm, mla_w_uq, mla_kv_norm, mla_w_ukv, mla_proj,
               s5_lam_re, s5_lam_im, s5_log_dt, s5_b_re, s5_b_im, s5_c_re, s5_c_im, s5_d, s5_glu_w, s5_glu_b,
               s5_proj, w_out, ln1_g, ln1_b, mlp_w1, mlp_w2, ln2_g, ln2_b)
    layers = [_prep_layer(*[wt[l] for wt in weights]) for l in range(w_in.shape[0])]
    return _trunk(x_prompt, layers), _trunk(x_sample, layers)
```

```python
import functools
import math

import numpy as np
import jax
import jax.numpy as jnp
from jax import lax
from jax.experimental import pallas as pl
from jax.experimental.pallas import tpu as pltpu

F32 = jnp.float32
BF16 = jnp.bfloat16
HIGHEST = lax.Precision.HIGHEST

D_MODEL = 1024
DEPTH = 2
RW_HEADS = 4
RW_HEAD = 64
RW_W = RW_HEADS * RW_HEAD
RW_LORA_W = 32
RW_LORA_A = 32
RW_LORA_G = 64
RW_GN_EPS = 64e-5
RW_IN = 3 * RW_W + RW_LORA_W + RW_LORA_A + RW_LORA_G
MLA_HEADS = 8
MLA_NOPE = 64
MLA_ROPE = 32
MLA_V = 64
MLA_Q_LORA = 256
MLA_KV_LORA = 128
MLA_W = MLA_HEADS * MLA_V
MLA_QK_PAD = 128
MLA_IN_PAD = MLA_Q_LORA + MLA_KV_LORA + MLA_QK_PAD
MLA_SCALE = (MLA_NOPE + MLA_ROPE) ** -0.5
LOG2_E = math.log2(math.e)
MLA_SHIFT_SLACK = 64.0
ROPE_THETA = 10000.0
RMS_EPS = 1e-6
S5_W = 256
S5_GROUP = 16
S5_GROUPS = S5_W // S5_GROUP
S5_STATE = 64
S5_CHUNK = 16
S5_CW = S5_CHUNK * S5_GROUP
LANES = 128
D_FF = 4 * D_MODEL
LN_EPS = 1e-5
N_BRANCH = 3
DN_ALPHA = (2 * DEPTH) ** 0.25
OFF_MLA = RW_IN
OFF_S5 = OFF_MLA + MLA_Q_LORA + MLA_KV_LORA + MLA_ROPE
OFF_GATE = OFF_S5 + S5_W
RW_CHUNK = 64
VMEM_LIMIT = 56 * 1024 * 1024


def _params(*sem):
    return pltpu.CompilerParams(dimension_semantics=sem, vmem_limit_bytes=VMEM_LIMIT)


def _mm(a, b):
    return jnp.dot(a.astype(BF16), b.astype(BF16), preferred_element_type=F32)


def _mm_nt(a, b):
    return lax.dot_general(a.astype(BF16), b.astype(BF16), (((1,), (1,)), ((), ())),
                           preferred_element_type=F32)


def _mm_tn(a, b):
    return lax.dot_general(a.astype(BF16), b.astype(BF16), (((0,), (0,)), ((), ())),
                           preferred_element_type=F32)


def _mm_f32(a, b):
    return jnp.dot(a, b, preferred_element_type=F32, precision=HIGHEST)


def _bf16_terms(x, n):
    terms = []
    for _ in range(n):
        t = x.astype(BF16)
        terms.append(t)
        x = x - t.astype(F32)
    return terms


def _mm_split(a, b, a_terms, b_terms):
    at = _bf16_terms(a, a_terms)
    bt = _bf16_terms(b, b_terms)
    out = None
    for i, x in enumerate(at):
        for j, y in enumerate(bt):
            if i + j < max(a_terms, b_terms):
                d = jnp.dot(x, y, preferred_element_type=F32)
                out = d if out is None else out + d
    return out


def _sigmoid(x):
    return 1.0 / (1.0 + jnp.exp(-x))


def _full(a):
    nd = a.ndim
    return pl.BlockSpec(a.shape, lambda *_: (0,) * nd)


def _layer_norm(x, g, b):
    mu = jnp.mean(x, -1, keepdims=True)
    xc = x - mu
    var = jnp.mean(xc * xc, -1, keepdims=True)
    return xc * lax.rsqrt(var + LN_EPS) * g + b


def _to_chunk_rows(x):
    n = S5_CHUNK
    assert n == S5_GROUPS and x.shape[1] == S5_CW and x.shape[0] % n == 0
    x = x.reshape(x.shape[0] // n, n, S5_CW)
    row = lax.broadcasted_iota(jnp.int32, (1, n, S5_CW), 1)
    pkt = lax.broadcasted_iota(jnp.int32, (1, n, S5_CW), 2) // S5_GROUP
    d = 1
    while d < n:
        row_bit = (row // d) % 2
        pkt_bit = (pkt // d) % 2
        up = pltpu.roll(pltpu.roll(x, n - d, axis=1), S5_GROUP * d, axis=2)
        down = pltpu.roll(pltpu.roll(x, d, axis=1), S5_CW - S5_GROUP * d, axis=2)
        x = jnp.where(row_bit == pkt_bit, x, jnp.where(row_bit == 0, up, down))
        d *= 2
    return x.reshape(x.shape[0] * n, S5_CW)


def _inproj_kernel(x_ref, wrw_ref, wmla_ref, ws5_ref, wg_ref, zrw_ref, zmla_ref, zs5_ref, g_ref, rows_s):
    xb = x_ref[...].astype(BF16)
    zrw_ref[...] = jnp.dot(xb, wrw_ref[...], preferred_element_type=F32)
    zmla_ref[...] = jnp.dot(xb, wmla_ref[...], preferred_element_type=F32)
    z_rows = _to_chunk_rows(jnp.dot(xb, ws5_ref[...], preferred_element_type=F32))
    chunks = z_rows.shape[0] // S5_GROUPS
    for half in range(S5_CW // LANES):
        lanes = slice(half * LANES, (half + 1) * LANES)
        rows_s[half] = z_rows[:, lanes]
        for g in range(S5_GROUPS):
            zs5_ref[g, :, lanes] = rows_s[half, pl.ds(g, chunks, stride=S5_GROUPS), :]
    g_ref[...] = _sigmoid(jnp.dot(xb, wg_ref[...], preferred_element_type=F32)).astype(g_ref.dtype)


def _inproj(x, w_rw, w_mla, w_s5, w_g, tm):
    t = x.shape[0]
    row = lambda n: pl.BlockSpec((tm, n), lambda i: (i, 0))
    return pl.pallas_call(
        _inproj_kernel,
        grid=(t // tm,),
        in_specs=[row(D_MODEL), _full(w_rw), _full(w_mla), _full(w_s5), _full(w_g)],
        out_specs=[row(RW_IN), row(MLA_IN_PAD),
                   pl.BlockSpec((S5_GROUPS, tm // S5_CHUNK, S5_CW), lambda i: (0, i, 0)), row(N_BRANCH * D_MODEL)],
        out_shape=[jax.ShapeDtypeStruct((t, RW_IN), F32), jax.ShapeDtypeStruct((t, MLA_IN_PAD), F32),
                   jax.ShapeDtypeStruct((S5_GROUPS, t // S5_CHUNK, S5_CW), F32),
                   jax.ShapeDtypeStruct((t, N_BRANCH * D_MODEL), BF16)],
        scratch_shapes=[pltpu.VMEM((S5_CW // LANES, tm, LANES), F32)],
        compiler_params=_params("parallel"),
        name="inproj",
    )(x, w_rw, w_mla, w_s5, w_g)


def _rwkv_kernel(nblk, tb, nb,
                 zf_ref, zfp_ref, zfn_ref, zb_ref, zbp_ref, zbn_ref,
                 mu_ref, w0_ref, w2_ref, a0_ref, a2_ref, g2_ref, kk_ref, ka_ref, rk_ref,
                 of_ref, bonus_ref, gate_ref, ob_ref, r_s, v_s, kn_s, lw_s, b_s, kd_s, state):
    step = pl.program_id(1)
    c = RW_CHUNK
    w = RW_W
    hr = lax.broadcasted_iota(jnp.int32, (w, w), 0) // RW_HEAD
    hc = lax.broadcasted_iota(jnp.int32, (w, w), 1) // RW_HEAD
    same_head = hr == hc
    head_ones = same_head.astype(F32)

    def prepare(d, bi, z_ref, zp_ref, zn_ref, blk):
        z = z_ref[bi]
        prev_row = jnp.where(blk == 0, 0.0, zp_ref[bi, 7:8, :])
        next_row = jnp.where(blk == nblk - 1, 0.0, zn_ref[bi, 0:1, :])
        rows = lax.broadcasted_iota(jnp.int32, z.shape, 0)
        z_prev = jnp.where(rows == 0, prev_row, pltpu.roll(z, 1, axis=0))
        z_next = jnp.where(rows == tb - 1, next_row, pltpu.roll(z, tb - 1, axis=0))
        z = z + mu_ref[...] * (0.5 * (z_prev + z_next) - z)
        r = z[:, 0:w]
        k = z[:, w:2 * w]
        v = z[:, 2 * w:3 * w]
        xw = z[:, 3 * w:3 * w + RW_LORA_W]
        xa = z[:, 3 * w + RW_LORA_W:3 * w + RW_LORA_W + RW_LORA_A]
        xg = z[:, 3 * w + RW_LORA_W + RW_LORA_A:]
        kk = k * kk_ref[...]
        kk_ss = _mm_split(kk * kk, head_ones, 2, 1)
        kk = kk * lax.rsqrt(jnp.maximum(kk_ss, 1e-12))
        y = w0_ref[d] + _mm_split(jnp.tanh(xw), w2_ref[d], 2, 2)
        lw = -math.exp(-0.5) * _sigmoid(y)
        a = _sigmoid(a0_ref[d] + _mm(xa, a2_ref[d]))
        r_s[d, bi] = r
        v_s[d, bi] = v
        kn_s[d, bi] = kk
        lw_s[d, bi] = lw
        b_s[d, bi] = kk * a
        kd_s[d, bi] = k * (1.0 + (a - 1.0) * ka_ref[...])
        if d == 0:
            rk = _mm_split(r * k * rk_ref[...], head_ones, 2, 1)
            bonus_ref[bi] = rk * v
            gate_ref[bi] = _mm(_sigmoid(xg), g2_ref[...])

    for bi in range(nb):
        prepare(0, bi, zf_ref, zfp_ref, zfn_ref, step)
        prepare(1, bi, zb_ref, zbp_ref, zbn_ref, nblk - 1 - step)

    @pl.when(step == 0)
    def _():
        state[...] = jnp.zeros_like(state)

    ti = lax.broadcasted_iota(jnp.int32, (c, c), 0)
    si = lax.broadcasted_iota(jnp.int32, (c, c), 1)
    tw = lax.broadcasted_iota(jnp.int32, (c, w), 0)
    sw = lax.broadcasted_iota(jnp.int32, (c, w), 1) % RW_HEAD
    eye = (tw == sw).astype(F32)

    def stack(x):
        return jnp.where(same_head, jnp.concatenate([x] * RW_HEADS, axis=0), 0.0).astype(BF16)

    nchunk = tb // c

    def chunk(reverse, bi, cpos):
        d = 1 if reverse else 0
        o_ref = ob_ref if reverse else of_ref
        cum_mat = ((si >= ti) if reverse else (si <= ti)).astype(F32)
        strict = (tw < sw) if reverse else (tw > sw)
        incl = (tw <= sw) if reverse else (tw >= sw)
        sl = pl.ds(pl.multiple_of(cpos * c, c), c)
        lwc = lw_s[d, bi, sl, :]
        l_in = _mm_split(cum_mat, lwc, 1, 3)
        l_tot = jnp.sum(lwc, axis=0, keepdims=True)
        e_in = jnp.exp(l_in)
        e_neg = jnp.exp(-l_in)
        e_tot = jnp.exp(l_tot)
        kap = kn_s[d, bi, sl, :] * jnp.exp(l_in - lwc)
        bt = b_s[d, bi, sl, :] * e_neg
        kt = kd_s[d, bi, sl, :] * e_neg
        rt = (r_s[d, bi, sl, :] * e_in).astype(BF16)
        v = v_s[d, bi, sl, :]
        kap_w, bt_w, kt_w, v_w = stack(kap), stack(bt), stack(kt), stack(v)
        kap = kap.astype(BF16)
        end_w = jnp.concatenate([bt * e_tot, kt * e_tot], axis=0).astype(BF16)
        s_bd = state[d, bi]
        s_w = s_bd.astype(BF16)
        yield
        kap_rt = jnp.concatenate([kap, rt], axis=0)
        am_b = _mm_nt(kap_rt, bt_w)
        a_b = jnp.where(strict, am_b[0:c], 0.0)
        m_b = jnp.where(incl, am_b[c:2 * c], 0.0)
        yield
        am_k = _mm_nt(kap_rt, kt_w)
        a_k = jnp.where(strict, am_k[0:c], 0.0)
        m_k = jnp.where(incl, am_k[c:2 * c], 0.0)
        yield
        assert c == 2 ** int(math.log2(c)) and int(math.log2(c)) % 2 == 0
        powers = [-a_b]
        pairs = []
        akv = _mm(a_k, v_w)
        for level in range(1, int(math.log2(c))):
            sq_w = stack(powers[-1])
            powers.append(_mm(powers[-1], sq_w))
            if level % 2 == 0:
                lo = eye + powers[level - 2]
                pairs.append(lo + _mm(lo, sq_w))
            yield
        lo = eye + powers[-2]
        pairs.append(lo + _mm(lo, stack(powers[-1])))
        tinv = pairs[0]
        for pr in pairs[1:-1]:
            tinv = _mm(tinv, stack(pr))
        yield
        tinv = _mm(tinv, stack(pairs[-1]))
        yield
        pm = _mm(tinv, kap_w)
        qm = _mm(tinv, stack(akv))
        yield
        u = -(_mm_nt(pm, s_w) + qm)
        yield
        o = _mm_nt(rt, s_w) + _mm(m_b, stack(u)) + _mm(m_k, v_w)
        upd = _mm_tn(jnp.concatenate([u, v], axis=0), end_w)
        yield
        o_ref[bi, sl, :] = o
        state[d, bi] = s_bd * e_tot + jnp.where(same_head, upd, 0.0)

    def all_scans(ci, carry):
        scans = [chunk(False, bi, ci) for bi in range(nb)] + [chunk(True, bi, nchunk - 1 - ci) for bi in range(nb)]
        while scans:
            scans = [g for g in scans if next(g, StopIteration) is not StopIteration]
        return carry

    lax.fori_loop(0, nchunk, all_scans, 0)


def _rwkv_scan(z, mu, w0, w2, a0, a2, g2, k_k, k_a, r_k, tb):
    bsz, seq, _ = z.shape
    nblk = seq // tb
    nb = 2 if bsz % 2 == 0 else 1
    z8 = z.reshape(bsz, seq // 8, 8, RW_IN)
    t8 = tb // 8

    def specs(pos):
        z_spec = pl.BlockSpec((nb, tb, RW_IN), lambda b, i: (b, pos(i), 0))
        zp_spec = pl.BlockSpec((nb, None, 8, RW_IN), lambda b, i: (b, jnp.maximum(pos(i) * t8 - 1, 0), 0, 0))
        zn_spec = pl.BlockSpec((nb, None, 8, RW_IN),
                               lambda b, i: (b, jnp.minimum((pos(i) + 1) * t8, seq // 8 - 1), 0, 0))
        o_spec = pl.BlockSpec((nb, tb, RW_W), lambda b, i: (b, pos(i), 0))
        return [z_spec, zp_spec, zn_spec], o_spec

    in_f, o_f = specs(lambda i: i)
    in_b, o_b = specs(lambda i: nblk - 1 - i)
    o_shape = jax.ShapeDtypeStruct((bsz, seq, RW_W), F32)
    weights = [mu, w0, w2, a0, a2, g2, k_k, k_a, r_k]
    return pl.pallas_call(
        functools.partial(_rwkv_kernel, nblk, tb, nb),
        grid=(bsz // nb, nblk),
        in_specs=in_f + in_b + [_full(wt) for wt in weights],
        out_specs=[o_f, o_f, o_f, o_b],
        out_shape=[o_shape] * 4,
        scratch_shapes=[pltpu.VMEM((2, nb, tb, RW_W), F32)] * 6 + [pltpu.VMEM((2, nb, RW_W, RW_W), F32)],
        compiler_params=_params("parallel", "arbitrary"),
        name="rwkv_scan",
    )(z, z8, z8, z, z8, z8, *weights)


def _rope_tables(seq):
    half = MLA_ROPE // 2
    inv = (ROPE_THETA ** (-np.arange(half, dtype=np.float32) / half)).astype(np.float32)
    ang = np.arange(seq, dtype=np.float32)[:, None] * inv[None, :]
    cos = np.cos(ang).astype(np.float32)
    sin = np.sin(ang).astype(np.float32)
    ct = np.zeros((seq, MLA_QK_PAD), np.float32)
    st = np.zeros((seq, MLA_QK_PAD), np.float32)
    ct[:, :MLA_NOPE] = 1.0
    ct[:, MLA_NOPE:MLA_NOPE + half] = cos
    ct[:, MLA_NOPE + half:MLA_NOPE + 2 * half] = cos
    st[:, MLA_NOPE:MLA_NOPE + half] = -sin
    st[:, MLA_NOPE + half:MLA_NOPE + 2 * half] = sin
    return jnp.asarray(ct), jnp.asarray(st), jnp.asarray(ct.T), jnp.asarray(st.T)


def _rope(x, cos_t, sin_t):
    half = MLA_ROPE // 2
    lane = lax.broadcasted_iota(jnp.int32, x.shape, 1)
    swapped = jnp.where(lane < MLA_NOPE + half,
                        pltpu.roll(x, MLA_QK_PAD - half, axis=1), pltpu.roll(x, half, axis=1))
    return x * cos_t + swapped * sin_t


def _rope_rows(x, cos_t, sin_t):
    half = MLA_ROPE // 2
    a, b = MLA_NOPE, MLA_NOPE + half
    swapped = jnp.concatenate([x[0:a], x[b:b + half], x[a:b], x[b + half:]], axis=0)
    return x * cos_t + swapped * sin_t


def _mla_qkv_kernel(z_ref, cos_ref, sin_ref, cosr_ref, sinr_ref, qn_ref, wuq_ref, kvn_ref, wuk_ref, wuv_ref,
                    q_ref, k_ref, v_ref):
    z = z_ref[...]
    cos_t = cos_ref[...]
    sin_t = sin_ref[...]
    cos_r = cosr_ref[...]
    sin_r = sinr_ref[...]
    c_q = z[:, :MLA_Q_LORA]
    c_kv = z[:, MLA_Q_LORA:MLA_Q_LORA + MLA_KV_LORA]
    k_rope = _rope(z[:, MLA_Q_LORA + MLA_KV_LORA:], cos_t, sin_t)
    c_q = (c_q * lax.rsqrt(jnp.mean(c_q * c_q, -1, keepdims=True) + RMS_EPS) * qn_ref[...]).astype(BF16)
    c_kv = (c_kv * lax.rsqrt(jnp.mean(c_kv * c_kv, -1, keepdims=True) + RMS_EPS) * kvn_ref[...]).astype(BF16)
    nt = (((1,), (1,)), ((), ()))
    for h in range(MLA_HEADS):
        q_t = lax.dot_general(wuq_ref[h], c_q, nt, preferred_element_type=F32)
        q_ref[h] = (_rope_rows(q_t, cos_r, sin_r) * (MLA_SCALE * LOG2_E)).astype(BF16)
        kh = jnp.dot(c_kv, wuk_ref[h], preferred_element_type=F32)
        k_ref[h] = (kh + k_rope).astype(BF16)
        v_t = lax.dot_general(wuv_ref[h], c_kv, nt, preferred_element_type=F32)
        v_ref[h] = v_t.astype(BF16)


def _mla_qkv(z, rope, q_norm, w_uq, kv_norm, w_uk, w_uv, seq, tm):
    t = z.shape[0]
    nseq = seq // tm
    tab = pl.BlockSpec((tm, MLA_QK_PAD), lambda i: (i % nseq, 0))
    tab_r = pl.BlockSpec((MLA_QK_PAD, tm), lambda i: (0, i % nseq))
    cols = lambda n: pl.BlockSpec((MLA_HEADS, n, tm), lambda i: (0, 0, i))
    return pl.pallas_call(
        _mla_qkv_kernel,
        grid=(t // tm,),
        in_specs=[pl.BlockSpec((tm, MLA_IN_PAD), lambda i: (i, 0)), tab, tab, tab_r, tab_r,
                  _full(q_norm), _full(w_uq), _full(kv_norm), _full(w_uk), _full(w_uv)],
        out_specs=[cols(MLA_QK_PAD), pl.BlockSpec((MLA_HEADS, tm, MLA_QK_PAD), lambda i: (0, i, 0)),
                   cols(MLA_V)],
        out_shape=[jax.ShapeDtypeStruct((MLA_HEADS, MLA_QK_PAD, t), BF16),
                   jax.ShapeDtypeStruct((MLA_HEADS, t, MLA_QK_PAD), BF16),
                   jax.ShapeDtypeStruct((MLA_HEADS, MLA_V, t), BF16)],
        compiler_params=_params("parallel"),
        name="mla_qkv",
    )(z, *rope, q_norm, w_uq, kv_norm, w_uk, w_uv)


def _mla_attn_kernel(nk, q_ref, k_ref, v_ref, o_ref, m_s, l_s, acc_s, bm_s, bl_s, pv_s):
    j = pl.program_id(2)

    def scores(h):
        return jnp.dot(k_ref[h], q_ref[h], preferred_element_type=F32)

    @pl.when(j == 0)
    def _():
        m_s[...] = jnp.full(m_s.shape, -jnp.inf, F32)
        l_s[...] = jnp.zeros_like(l_s)
        acc_s[...] = jnp.zeros_like(acc_s)
        for h in range(MLA_HEADS):
            bm_s[h:h + 1, :] = jnp.max(scores(h), axis=0, keepdims=True)

    @pl.when(j > 0)
    def _():
        s_next = scores(0)
        for h in range(MLA_HEADS):
            s = s_next
            if h + 1 < MLA_HEADS:
                s_next = scores(h + 1)
            bm_s[h:h + 1, :] = jnp.max(s, axis=0, keepdims=True)
            p = jnp.exp2(s - m_s[h:h + 1, :])
            bl_s[h:h + 1, :] = jnp.sum(p, axis=0, keepdims=True)
            pv_s[h] = jnp.dot(v_ref[h], p.astype(BF16), preferred_element_type=F32)

    stale_shift_ok = jnp.max(bm_s[...] - m_s[...]) < MLA_SHIFT_SLACK

    @pl.when(stale_shift_ok)
    def _():
        for h in range(MLA_HEADS):
            m_prev = m_s[h:h + 1, :]
            m_new = jnp.maximum(m_prev, bm_s[h:h + 1, :])
            alpha = jnp.exp2(m_prev - m_new)
            acc_s[h] = (acc_s[h] + pv_s[h]) * alpha
            l_s[h:h + 1, :] = (l_s[h:h + 1, :] + bl_s[h:h + 1, :]) * alpha
            m_s[h:h + 1, :] = m_new

    @pl.when(jnp.logical_not(stale_shift_ok))
    def _():
        for h in range(MLA_HEADS):
            s = scores(h)
            m_prev = m_s[h:h + 1, :]
            m_new = jnp.maximum(m_prev, bm_s[h:h + 1, :])
            alpha = jnp.exp2(m_prev - m_new)
            p = jnp.exp2(s - m_new)
            l_s[h:h + 1, :] = alpha * l_s[h:h + 1, :] + jnp.sum(p, axis=0, keepdims=True)
            acc_s[h] = alpha * acc_s[h] + jnp.dot(v_ref[h], p.astype(BF16), preferred_element_type=F32)
            m_s[h:h + 1, :] = m_new

    @pl.when(j == nk - 1)
    def _():
        heads = [acc_s[h] / l_s[h:h + 1, :] for h in range(MLA_HEADS)]
        o_ref[...] = jnp.concatenate(heads, axis=0).T


def _mla_attn(q, k, v, bsz, seq, tq, tk):
    nq = seq // tq
    nk = seq // tk
    t = bsz * seq
    return pl.pallas_call(
        functools.partial(_mla_attn_kernel, nk),
        grid=(bsz, nq, nk),
        in_specs=[pl.BlockSpec((MLA_HEADS, MLA_QK_PAD, tq), lambda b, i, j: (0, 0, b * nq + i)),
                  pl.BlockSpec((MLA_HEADS, tk, MLA_QK_PAD), lambda b, i, j: (0, b * nk + j, 0)),
                  pl.BlockSpec((MLA_HEADS, MLA_V, tk), lambda b, i, j: (0, 0, b * nk + j))],
        out_specs=pl.BlockSpec((tq, MLA_W), lambda b, i, j: (b * nq + i, 0)),
        out_shape=jax.ShapeDtypeStruct((t, MLA_W), F32),
        scratch_shapes=[pltpu.VMEM((MLA_HEADS, tq), F32), pltpu.VMEM((MLA_HEADS, tq), F32),
                        pltpu.VMEM((MLA_HEADS, MLA_V, tq), F32),
                        pltpu.VMEM((MLA_HEADS, tq), F32), pltpu.VMEM((MLA_HEADS, tq), F32),
                        pltpu.VMEM((MLA_HEADS, MLA_V, tq), F32)],
        compiler_params=_params("parallel", "parallel", "arbitrary"),
        name="mla_attn",
    )(q, k, v)


def _s5_param_kernel(lre_ref, lim_ref, ldt_ref, btre_ref, btim_ref, cre_ref, cim_ref, ek_ref, ew_ref, ee_ref,
                     kt_ref, wre_ref, wim_ref, ere_ref, eim_ref, are_ref, aim_ref):
    lam_re = lre_ref[...]
    lam_im = lim_ref[...]
    dt = jnp.exp(ldt_ref[...])

    def rows_of(x):
        return jnp.broadcast_to(x[:, None, :], (S5_CHUNK, S5_GROUP, S5_STATE)).reshape(S5_CW, S5_STATE)

    def power(e):
        mag = jnp.exp(lam_re * dt * e)
        ang = lam_im * dt * e
        return mag * jnp.cos(ang), mag * jnp.sin(ang)

    a_re, a_im = power(jnp.ones_like(lam_re))
    den = lam_re * lam_re + lam_im * lam_im
    nr = a_re - 1.0
    coef_re = (nr * lam_re + a_im * lam_im) / den
    coef_im = (a_im * lam_re - nr * lam_im) / den
    bt_re = btre_ref[...]
    bt_im = btim_ref[...]
    bb_re = coef_re * bt_re - coef_im * bt_im
    bb_im = coef_re * bt_im + coef_im * bt_re
    tile = lambda x: jnp.concatenate([x] * S5_CHUNK, axis=0)
    c_re = tile(cre_ref[...])
    c_im = tile(cim_ref[...])
    pk_re, pk_im = map(rows_of, power(ek_ref[...]))
    ck_re = c_re * pk_re - c_im * pk_im
    ck_im = c_re * pk_im + c_im * pk_re
    nt = (((1,), (1,)), ((), ()))
    kt_ref[...] = (lax.dot_general(bb_re, ck_re, nt, precision=HIGHEST, preferred_element_type=F32)
                   - lax.dot_general(bb_im, ck_im, nt, precision=HIGHEST, preferred_element_type=F32))
    pw_re, pw_im = map(rows_of, power(ew_ref[...]))
    bbt_re = tile(bb_re)
    bbt_im = tile(bb_im)
    wre_ref[...] = pw_re * bbt_re - pw_im * bbt_im
    wim_ref[...] = pw_re * bbt_im + pw_im * bbt_re
    pe_re, pe_im = map(rows_of, power(ee_ref[...]))
    ere_ref[...] = c_re * pe_re - c_im * pe_im
    eim_ref[...] = -(c_re * pe_im + c_im * pe_re)
    ac_re, ac_im = power(jnp.full_like(lam_re, float(S5_CHUNK)))
    are_ref[...] = ac_re
    aim_ref[...] = ac_im


def _s5_params(lam_re, lam_im, log_dt, b_re, b_im, c_re, c_im):
    n = 2 * S5_GROUPS
    p, c, cw = S5_STATE, S5_GROUP, S5_CW
    vec = lambda a: a.reshape(n, 1, p)
    ldt = jnp.broadcast_to(log_dt.reshape(n, 1, 1), (n, 1, p))
    bt = lambda a: a.transpose(0, 1, 3, 2).reshape(n, c, p)
    cc = lambda a: a.reshape(n, c, p)
    steps = np.arange(S5_CHUNK, dtype=np.float32)
    rows = lambda e: jnp.asarray(np.broadcast_to(e[:, :, None], (2, S5_CHUNK, p)).copy())
    e_k = rows(np.stack([steps, steps]))
    e_w = rows(np.stack([S5_CHUNK - 1 - steps, steps]))
    e_e = rows(np.stack([steps + 1, S5_CHUNK - steps]))
    per = lambda *shape: pl.BlockSpec((None,) + shape, lambda i: (i,) + (0,) * len(shape))
    per_dir = pl.BlockSpec((None, S5_CHUNK, p), lambda i: (i // S5_GROUPS, 0, 0))
    out = lambda *shape: jax.ShapeDtypeStruct((n,) + shape, F32)
    return pl.pallas_call(
        _s5_param_kernel,
        grid=(n,),
        in_specs=[per(1, p)] * 3 + [per(c, p)] * 4 + [per_dir] * 3,
        out_specs=[per(c, cw)] + [per(cw, p)] * 4 + [per(1, p)] * 2,
        out_shape=[out(c, cw)] + [out(cw, p)] * 4 + [out(1, p)] * 2,
        compiler_params=_params("parallel"),
        name="s5_params",
    )(vec(lam_re), vec(lam_im), ldt, bt(b_re), bt(b_im), cc(c_re), cc(c_im), e_k, e_w, e_e)


def _s5_toeplitz(kt):
    g, c, n = S5_GROUPS, S5_GROUP, S5_CHUNK
    k = kt.reshape(2, g, c, n, c)
    s_in = np.arange(n)[:, None]
    s_out = np.arange(n)[None, :]
    lag_f = np.clip(s_out - s_in, 0, n - 1)
    lag_b = np.clip(s_in - s_out, 0, n - 1)
    m_f = jnp.asarray((s_out >= s_in).astype(np.float32))[None, None, :, :, None]
    m_b = jnp.asarray((s_in >= s_out).astype(np.float32))[None, None, :, :, None]
    m = k[0][:, :, lag_f, :] * m_f + k[1][:, :, lag_b, :] * m_b
    return m.transpose(0, 2, 1, 3, 4).reshape(g, n * c, n * c)


def _s5_operators(lam_re, lam_im, log_dt, b_re, b_im, c_re, c_im):
    kt, w_re, w_im, e_re, e_im, a_re, a_im = _s5_params(lam_re, lam_im, log_dt, b_re, b_im, c_re, c_im)
    g = S5_GROUPS
    parts = lambda re, im: jnp.stack([re[:g], im[:g], re[g:], im[g:]], axis=1)
    return (_s5_toeplitz(kt).astype(BF16), parts(w_re, w_im).astype(BF16), parts(e_re, e_im).astype(BF16),
            parts(a_re[:, 0], a_im[:, 0]))


def _s5_chunk_kernel(nk, rows, u_ref, m_ref, w_ref, e_ref, a_ref, y_ref, wfr_s, wfi_s, wbr_s, wbi_s):
    u = u_ref[...].astype(BF16)
    y_ref[...] = jnp.dot(u, m_ref[...], preferred_element_type=F32)
    for i, dst in enumerate((wfr_s, wfi_s, wbr_s, wbi_s)):
        dst[...] = jnp.dot(u, w_ref[i], preferred_element_type=F32)
    a = [jnp.broadcast_to(a_ref[i:i + 1, :], (rows, S5_STATE)) for i in range(4)]

    def step(k, carry):
        fr, fi, br, bi = carry
        sf = pl.ds(k, rows, stride=nk)
        sb = pl.ds(nk - 1 - k, rows, stride=nk)
        wfr, wfi, wbr, wbi = wfr_s[sf, :], wfi_s[sf, :], wbr_s[sb, :], wbi_s[sb, :]
        wfr_s[sf, :] = fr
        wfi_s[sf, :] = fi
        wbr_s[sb, :] = br
        wbi_s[sb, :] = bi
        return (a[0] * fr - a[1] * fi + wfr, a[0] * fi + a[1] * fr + wfi,
                a[2] * br - a[3] * bi + wbr, a[2] * bi + a[3] * br + wbi)

    zero = jnp.zeros((rows, S5_STATE), F32)
    lax.fori_loop(0, nk, step, (zero, zero, zero, zero))
    carry_in = 0.0
    for i, src in enumerate((wfr_s, wfi_s, wbr_s, wbi_s)):
        carry_in = carry_in + _mm_nt(src[...], e_ref[i])
    y_ref[...] += carry_in


def _s5_scan(z_groups, bsz, seq, operators):
    m_mat, w_mat, e_mat, a_vec = operators
    nk = seq // S5_CHUNK
    r = bsz * nk
    grp = lambda *shape: pl.BlockSpec((None,) + shape, lambda i: (i,) + (0,) * len(shape))
    return pl.pallas_call(
        functools.partial(_s5_chunk_kernel, nk, bsz),
        grid=(S5_GROUPS,),
        in_specs=[grp(r, S5_CW), grp(S5_CW, S5_CW), grp(4, S5_CW, S5_STATE), grp(4, S5_CW, S5_STATE),
                  grp(4, S5_STATE)],
        out_specs=grp(r, S5_CW),
        out_shape=jax.ShapeDtypeStruct((S5_GROUPS, r, S5_CW), F32),
        scratch_shapes=[pltpu.VMEM((r, S5_STATE), F32)] * 4,
        compiler_params=_params("parallel"),
        name="s5_scan",
    )(z_groups, m_mat, w_mat, e_mat, a_vec)


def _s5_post_kernel(y_ref, u_ref, d_ref, w_ref, b_ref, o_ref, rows_s):
    chunks = y_ref.shape[1]
    for g in range(S5_GROUPS):
        y_g = y_ref[g] + d_ref[g:g + 1, :] * u_ref[g]
        for half in range(S5_CW // LANES):
            rows_s[half, pl.ds(g, chunks, stride=S5_GROUPS), :] = y_g[:, half * LANES:(half + 1) * LANES]
    y = jnp.concatenate([rows_s[half] for half in range(S5_CW // LANES)], axis=1)
    y = _to_chunk_rows(y)
    y = jax.nn.gelu(y, approximate=True)
    o_ref[...] = y * _sigmoid(_mm(y, w_ref[...]) + b_ref[...])


def _s5_post(y, u, d_rows, glu_w, glu_b, tm):
    t = u.shape[1] * S5_CHUNK
    grp = pl.BlockSpec((S5_GROUPS, tm // S5_CHUNK, S5_CW), lambda i: (0, i, 0))
    return pl.pallas_call(
        _s5_post_kernel,
        grid=(t // tm,),
        in_specs=[grp, grp, _full(d_rows), _full(glu_w), _full(glu_b)],
        out_specs=pl.BlockSpec((tm, S5_W), lambda i: (i, 0)),
        out_shape=jax.ShapeDtypeStruct((t, S5_W), F32),
        scratch_shapes=[pltpu.VMEM((S5_CW // LANES, tm, LANES), F32)],
        compiler_params=_params("parallel"),
        name="s5_post",
    )(y, u, d_rows, glu_w, glu_b)


def _merge_kernel(x_ref, of_ref, ob_ref, bonus_ref, rgate_ref, mla_ref, s5_ref, g_ref,
                  gng_ref, gnb_ref, wrw_ref, wmla_ref, ws5_ref, wout_ref, lng_ref, lnb_ref, o_ref):
    w = RW_W
    hr = lax.broadcasted_iota(jnp.int32, (w, w), 0) // RW_HEAD
    hc = lax.broadcasted_iota(jnp.int32, (w, w), 1) // RW_HEAD
    head_mean = (hr == hc).astype(F32) * (1.0 / RW_HEAD)
    o = of_ref[...] + ob_ref[...]
    mean = _mm_split(o, head_mean, 2, 1)
    oc = o - mean
    var = _mm_split(oc * oc, head_mean, 2, 1)
    o = oc * lax.rsqrt(var + RW_GN_EPS) * gng_ref[...] + gnb_ref[...]
    y_rw = _mm((o + bonus_ref[...]) * rgate_ref[...], wrw_ref[...])
    y_mla = _mm(mla_ref[...], wmla_ref[...])
    y_s5 = _mm(s5_ref[...], ws5_ref[...])
    d = D_MODEL
    merged = g_ref[:, 0:d] * y_rw + g_ref[:, d:2 * d] * y_mla + g_ref[:, 2 * d:3 * d] * y_s5
    o_ref[...] = _layer_norm(DN_ALPHA * x_ref[...] + _mm(merged, wout_ref[...]), lng_ref[...], lnb_ref[...])


def _merge(x, o_f, o_b, bonus, rgate, o_mla, y_s5, gates, gn_g, gn_b, w_rw, w_mla, w_s5, w_out, ln_g, ln_b, tm):
    t = x.shape[0]
    row = lambda n: pl.BlockSpec((tm, n), lambda i: (i, 0))
    weights = [gn_g, gn_b, w_rw, w_mla, w_s5, w_out, ln_g, ln_b]
    return pl.pallas_call(
        _merge_kernel,
        grid=(t // tm,),
        in_specs=[row(D_MODEL), row(RW_W), row(RW_W), row(RW_W), row(RW_W), row(MLA_W), row(S5_W),
                  row(N_BRANCH * D_MODEL)] + [_full(wt) for wt in weights],
        out_specs=row(D_MODEL),
        out_shape=jax.ShapeDtypeStruct((t, D_MODEL), F32),
        compiler_params=_params("parallel"),
        name="merge_ln1",
    )(x, o_f, o_b, bonus, rgate, o_mla, y_s5, gates, *weights)


def _mlp_kernel(nf, x_ref, w1_ref, w2_ref, lng_ref, lnb_ref, o_ref, acc_s):
    j = pl.program_id(1)

    @pl.when(j == 0)
    def _():
        acc_s[...] = jnp.zeros_like(acc_s)

    h = jnp.maximum(_mm(x_ref[...], w1_ref[...]), 0.0)
    acc_s[...] += _mm(h * h, w2_ref[...])

    @pl.when(j == nf - 1)
    def _():
        o_ref[...] = _layer_norm(DN_ALPHA * x_ref[...] + acc_s[...], lng_ref[...], lnb_ref[...])


def _mlp(x, w1, w2, ln_g, ln_b, tm, tf):
    t = x.shape[0]
    nf = D_FF // tf
    return pl.pallas_call(
        functools.partial(_mlp_kernel, nf),
        grid=(t // tm, nf),
        in_specs=[pl.BlockSpec((tm, D_MODEL), lambda i, j: (i, 0)),
                  pl.BlockSpec((D_MODEL, tf), lambda i, j: (0, j)),
                  pl.BlockSpec((tf, D_MODEL), lambda i, j: (j, 0)),
                  _full(ln_g), _full(ln_b)],
        out_specs=pl.BlockSpec((tm, D_MODEL), lambda i, j: (i, 0)),
        out_shape=jax.ShapeDtypeStruct((t, D_MODEL), F32),
        scratch_shapes=[pltpu.VMEM((tm, D_MODEL), F32)],
        compiler_params=_params("parallel", "arbitrary"),
        name="mlp_ln2",
    )(x, w1, w2, ln_g, ln_b)


def _tile(n, pref):
    t = min(n, pref)
    assert n % t == 0, (n, pref)
    return t


def _prep_layer(w_in, rw_mu, rw_w0, rw_w2, rw_a0, rw_a2, rw_g2, rw_k_k, rw_k_a, rw_r_k, rw_gn_g, rw_gn_b, rw_proj,
                mla_q_norm, mla_w_uq, mla_kv_norm, mla_w_ukv, mla_proj,
                s5_lam_re, s5_lam_im, s5_log_dt, s5_b_re, s5_b_im, s5_c_re, s5_c_im, s5_d, s5_glu_w, s5_glu_b,
                s5_proj, w_out, ln1_g, ln1_b, mlp_w1, mlp_w2, ln2_g, ln2_b):
    row = lambda a: a.reshape(1, -1)
    p = {}
    p["w_rw"] = w_in[:, :OFF_MLA].astype(BF16)
    w_mla = w_in[:, OFF_MLA:OFF_S5]
    zeros = lambda n: jnp.zeros((D_MODEL, n), F32)
    p["w_mla"] = jnp.concatenate([w_mla[:, :MLA_Q_LORA + MLA_KV_LORA], zeros(MLA_NOPE),
                                  w_mla[:, MLA_Q_LORA + MLA_KV_LORA:],
                                  zeros(MLA_QK_PAD - MLA_NOPE - MLA_ROPE)], axis=1).astype(BF16)
    p["w_s5"] = w_in[:, OFF_S5:OFF_GATE].astype(BF16)
    p["w_gate"] = w_in[:, OFF_GATE:].astype(BF16)
    p["rw"] = [row(rw_mu)]
    p["rw_dir"] = [rw_w0[:, None, :], rw_w2, rw_a0[:, None, :], rw_a2]
    p["rw_shared"] = [rw_g2, row(rw_k_k), row(rw_k_a), row(rw_r_k)]
    p["rw_gn"] = [row(rw_gn_g), row(rw_gn_b)]
    p["rw_proj"] = rw_proj.astype(BF16)
    uq = mla_w_uq.reshape(MLA_Q_LORA, MLA_HEADS, MLA_NOPE + MLA_ROPE)
    uq = jnp.pad(uq, ((0, 0), (0, 0), (0, MLA_QK_PAD - MLA_NOPE - MLA_ROPE)))
    p["w_uq"] = uq.transpose(1, 2, 0).astype(BF16)
    ukv = mla_w_ukv.reshape(MLA_KV_LORA, MLA_HEADS, MLA_NOPE + MLA_V)
    uk = jnp.pad(ukv[:, :, :MLA_NOPE], ((0, 0), (0, 0), (0, MLA_QK_PAD - MLA_NOPE)))
    p["w_uk"] = uk.transpose(1, 0, 2).astype(BF16)
    p["w_uv"] = ukv[:, :, MLA_NOPE:].transpose(1, 2, 0).astype(BF16)
    p["q_norm"] = row(mla_q_norm)
    p["kv_norm"] = row(mla_kv_norm)
    p["mla_proj"] = mla_proj.astype(BF16)
    p["s5_ops"] = _s5_operators(s5_lam_re, s5_lam_im, s5_log_dt, s5_b_re, s5_b_im, s5_c_re, s5_c_im)
    d_rows = jnp.broadcast_to(s5_d.reshape(S5_GROUPS, 1, S5_GROUP), (S5_GROUPS, S5_CHUNK, S5_GROUP))
    p["s5_post"] = [d_rows.reshape(S5_GROUPS, S5_CW), s5_glu_w.astype(BF16), row(s5_glu_b)]
    p["s5_proj"] = s5_proj.astype(BF16)
    p["w_out"] = w_out.astype(BF16)
    p["ln1"] = [row(ln1_g), row(ln1_b)]
    p["w1"] = mlp_w1.astype(BF16)
    p["w2"] = mlp_w2.astype(BF16)
    p["ln2"] = [row(ln2_g), row(ln2_b)]
    return p


def _layer(x, p, bsz, seq, rope):
    t = bsz * seq
    tm = _tile(t, 512)
    z_rw, z_mla, z_s5, gates = _inproj(x, p["w_rw"], p["w_mla"], p["w_s5"], p["w_gate"], _tile(t, 256))

    tb = _tile(seq, 512)
    z3 = z_rw.reshape(bsz, seq, RW_IN)
    o_f, bonus, rgate, o_b = _rwkv_scan(z3, *p["rw"], *p["rw_dir"], *p["rw_shared"], tb=tb)
    flat = lambda a: a.reshape(t, RW_W)

    q, k, v = _mla_qkv(z_mla, rope, p["q_norm"], p["w_uq"], p["kv_norm"], p["w_uk"], p["w_uv"],
                       seq, _tile(seq, 512))
    o_mla = _mla_attn(q, k, v, bsz, seq, _tile(seq, 512), _tile(seq, 2048))

    y_s5 = _s5_post(_s5_scan(z_s5, bsz, seq, p["s5_ops"]), z_s5, *p["s5_post"], _tile(t, 1024))

    x1 = _merge(x, flat(o_f), flat(o_b), flat(bonus), flat(rgate), o_mla, y_s5, gates,
                *p["rw_gn"], p["rw_proj"], p["mla_proj"], p["s5_proj"], p["w_out"], *p["ln1"], tm)
    return _mlp(x1, p["w1"], p["w2"], *p["ln2"], _tile(t, 1024), 1024)


def _trunk(x, layers):
    bsz, seq, _ = x.shape
    assert seq % RW_CHUNK == 0 and seq % S5_CHUNK == 0
    rope = _rope_tables(seq)
    h = x.reshape(bsz * seq, D_MODEL)
    for p in layers:
        h = _layer(h, p, bsz, seq, rope)
    return h.reshape(bsz, seq, D_MODEL)


def kernel(x_prompt, x_sample, w_in, rw_mu, rw_w0, rw_w2, rw_a0, rw_a2, rw_g2, rw_k_k, rw_k_a, rw_r_k, rw_gn_g, rw_gn_b, rw_proj, mla_q_norm, mla_w_uq, mla_kv_norm, mla_w_ukv, mla_proj, s5_lam_re, s5_lam_im, s5_log_dt, s5_b_re, s5_b_im, s5_c_re, s5_c_im, s5_d, s5_glu_w, s5_glu_b, s5_proj, w_out, ln1_g, ln1_b, mlp_w1, mlp_w2, ln2_g, ln2_b):
    weights = (w_in, rw_mu, rw_w0, rw_w2, rw_a0, rw_a2, rw_g2, rw_k_k, rw_k_a, rw_r_k, rw_gn_g, rw_gn_b, rw_proj,
               mla_q_norm, mla_w_uq, mla_kv_norm, mla_w_ukv, mla_proj,
               s5_lam_re, s5_lam_im, s5_log_dt, s5_b_re, s5_b_im, s5_c_re, s5_c_im, s5_d, s5_glu_w, s5_glu_b,
               s5_proj, w_out, ln1_g, ln1_b, mlp_w1, mlp_w2, ln2_g, ln2_b)
    layers = [_prep_layer(*[wt[l] for wt in weights]) for l in range(w_in.shape[0])]
    return _trunk(x_prompt, layers), _trunk(x_sample, layers)
```

```python
import functools
import math

import numpy as np
import jax
import jax.numpy as jnp
from jax import lax
from jax.experimental import pallas as pl
from jax.experimental.pallas import tpu as pltpu

F32 = jnp.float32
BF16 = jnp.bfloat16
HIGHEST = lax.Precision.HIGHEST

D_MODEL = 1024
DEPTH = 2
RW_HEADS = 4
RW_HEAD = 64
RW_W = RW_HEADS * RW_HEAD
RW_LORA_W = 32
RW_LORA_A = 32
RW_LORA_G = 64
RW_GN_EPS = 64e-5
RW_IN = 3 * RW_W + RW_LORA_W + RW_LORA_A + RW_LORA_G
MLA_HEADS = 8
MLA_NOPE = 64
MLA_ROPE = 32
MLA_V = 64
MLA_Q_LORA = 256
MLA_KV_LORA = 128
MLA_W = MLA_HEADS * MLA_V
MLA_QK_PAD = 128
MLA_IN_PAD = MLA_Q_LORA + MLA_KV_LORA + MLA_QK_PAD
MLA_SCALE = (MLA_NOPE + MLA_ROPE) ** -0.5
LOG2_E = math.log2(math.e)
MLA_SHIFT_SLACK = 64.0
ROPE_THETA = 10000.0
RMS_EPS = 1e-6
S5_W = 256
S5_GROUP = 16
S5_GROUPS = S5_W // S5_GROUP
S5_STATE = 64
S5_CHUNK = 16
S5_CW = S5_CHUNK * S5_GROUP
LANES = 128
D_FF = 4 * D_MODEL
LN_EPS = 1e-5
N_BRANCH = 3
DN_ALPHA = (2 * DEPTH) ** 0.25
OFF_MLA = RW_IN
OFF_S5 = OFF_MLA + MLA_Q_LORA + MLA_KV_LORA + MLA_ROPE
OFF_GATE = OFF_S5 + S5_W
RW_CHUNK = 64
VMEM_LIMIT = 56 * 1024 * 1024


def _params(*sem):
    return pltpu.CompilerParams(dimension_semantics=sem, vmem_limit_bytes=VMEM_LIMIT)


def _mm(a, b):
    return jnp.dot(a.astype(BF16), b.astype(BF16), preferred_element_type=F32)


def _mm_nt(a, b):
    return lax.dot_general(a.astype(BF16), b.astype(BF16), (((1,), (1,)), ((), ())),
                           preferred_element_type=F32)


def _mm_tn(a, b):
    return lax.dot_general(a.astype(BF16), b.astype(BF16), (((0,), (0,)), ((), ())),
                           preferred_element_type=F32)


def _mm_f32(a, b):
    return jnp.dot(a, b, preferred_element_type=F32, precision=HIGHEST)


def _bf16_terms(x, n):
    terms = []
    for _ in range(n):
        t = x.astype(BF16)
        terms.append(t)
        x = x - t.astype(F32)
    return terms


def _mm_split(a, b, a_terms, b_terms):
    at = _bf16_terms(a, a_terms)
    bt = _bf16_terms(b, b_terms)
    out = None
    for i, x in enumerate(at):
        for j, y in enumerate(bt):
            if i + j < max(a_terms, b_terms):
                d = jnp.dot(x, y, preferred_element_type=F32)
                out = d if out is None else out + d
    return out


def _sigmoid(x):
    return 1.0 / (1.0 + jnp.exp(-x))


def _full(a):
    nd = a.ndim
    return pl.BlockSpec(a.shape, lambda *_: (0,) * nd)


def _layer_norm(x, g, b):
    mu = jnp.mean(x, -1, keepdims=True)
    xc = x - mu
    var = jnp.mean(xc * xc, -1, keepdims=True)
    return xc * lax.rsqrt(var + LN_EPS) * g + b


def _to_chunk_rows(x):
    n = S5_CHUNK
    assert n == S5_GROUPS and x.shape[1] == S5_CW and x.shape[0] % n == 0
    x = x.reshape(x.shape[0] // n, n, S5_CW)
    row = lax.broadcasted_iota(jnp.int32, (1, n, S5_CW), 1)
    pkt = lax.broadcasted_iota(jnp.int32, (1, n, S5_CW), 2) // S5_GROUP
    d = 1
    while d < n:
        row_bit = (row // d) % 2
        pkt_bit = (pkt // d) % 2
        up = pltpu.roll(pltpu.roll(x, n - d, axis=1), S5_GROUP * d, axis=2)
        down = pltpu.roll(pltpu.roll(x, d, axis=1), S5_CW - S5_GROUP * d, axis=2)
        x = jnp.where(row_bit == pkt_bit, x, jnp.where(row_bit == 0, up, down))
        d *= 2
    return x.reshape(x.shape[0] * n, S5_CW)


def _inproj_kernel(x_ref, wrw_ref, wmla_ref, ws5_ref, wg_ref, zrw_ref, zmla_ref, zs5_ref, g_ref, rows_s):
    xb = x_ref[...].astype(BF16)
    zrw_ref[...] = jnp.dot(xb, wrw_ref[...], preferred_element_type=F32)
    zmla_ref[...] = jnp.dot(xb, wmla_ref[...], preferred_element_type=F32)
    z_rows = _to_chunk_rows(jnp.dot(xb, ws5_ref[...], preferred_element_type=F32))
    chunks = z_rows.shape[0] // S5_GROUPS
    for half in range(S5_CW // LANES):
        lanes = slice(half * LANES, (half + 1) * LANES)
        rows_s[half] = z_rows[:, lanes]
        for g in range(S5_GROUPS):
            zs5_ref[g, :, lanes] = rows_s[half, pl.ds(g, chunks, stride=S5_GROUPS), :]
    g_ref[...] = _sigmoid(jnp.dot(xb, wg_ref[...], preferred_element_type=F32)).astype(g_ref.dtype)


def _inproj(x, w_rw, w_mla, w_s5, w_g, tm):
    t = x.shape[0]
    row = lambda n: pl.BlockSpec((tm, n), lambda i: (i, 0))
    return pl.pallas_call(
        _inproj_kernel,
        grid=(t // tm,),
        in_specs=[row(D_MODEL), _full(w_rw), _full(w_mla), _full(w_s5), _full(w_g)],
        out_specs=[row(RW_IN), row(MLA_IN_PAD),
                   pl.BlockSpec((S5_GROUPS, tm // S5_CHUNK, S5_CW), lambda i: (0, i, 0)), row(N_BRANCH * D_MODEL)],
        out_shape=[jax.ShapeDtypeStruct((t, RW_IN), F32), jax.ShapeDtypeStruct((t, MLA_IN_PAD), F32),
                   jax.ShapeDtypeStruct((S5_GROUPS, t // S5_CHUNK, S5_CW), F32),
                   jax.ShapeDtypeStruct((t, N_BRANCH * D_MODEL), BF16)],
        scratch_shapes=[pltpu.VMEM((S5_CW // LANES, tm, LANES), F32)],
        compiler_params=_params("parallel"),
        name="inproj",
    )(x, w_rw, w_mla, w_s5, w_g)


def _rwkv_kernel(nblk, tb, nb,
                 zf_ref, zfp_ref, zfn_ref, zb_ref, zbp_ref, zbn_ref,
                 mu_ref, w0_ref, w2_ref, a0_ref, a2_ref, g2_ref, kk_ref, ka_ref, rk_ref,
                 of_ref, bonus_ref, gate_ref, ob_ref, r_s, v_s, kn_s, lw_s, b_s, kd_s, state):
    step = pl.program_id(1)
    c = RW_CHUNK
    w = RW_W
    hr = lax.broadcasted_iota(jnp.int32, (w, w), 0) // RW_HEAD
    hc = lax.broadcasted_iota(jnp.int32, (w, w), 1) // RW_HEAD
    same_head = hr == hc
    head_ones = same_head.astype(F32)

    def prepare(d, bi, z_ref, zp_ref, zn_ref, blk):
        z = z_ref[bi]
        prev_row = jnp.where(blk == 0, 0.0, zp_ref[bi, 7:8, :])
        next_row = jnp.where(blk == nblk - 1, 0.0, zn_ref[bi, 0:1, :])
        tile_row = lax.broadcasted_iota(jnp.int32, (8, RW_IN), 0)
        z_prev = pltpu.roll(z, 1, axis=0)
        z_prev = jnp.concatenate([jnp.where(tile_row == 0, prev_row, z_prev[0:8]), z_prev[8:]], axis=0)
        z_next = pltpu.roll(z, tb - 1, axis=0)
        z_next = jnp.concatenate([z_next[:tb - 8], jnp.where(tile_row == 7, next_row, z_next[tb - 8:])], axis=0)
        mu = mu_ref[...]
        z = (1.0 - mu) * z + (0.5 * mu) * (z_prev + z_next)
        r = z[:, 0:w]
        k = z[:, w:2 * w]
        v = z[:, 2 * w:3 * w]
        xw = z[:, 3 * w:3 * w + RW_LORA_W]
        xa = z[:, 3 * w + RW_LORA_W:3 * w + RW_LORA_W + RW_LORA_A]
        xg = z[:, 3 * w + RW_LORA_W + RW_LORA_A:]
        kk = k * kk_ref[...]
        kk_ss = _mm_split(kk * kk, head_ones, 2, 1)
        kk = kk * lax.rsqrt(jnp.maximum(kk_ss, 1e-12))
        y = w0_ref[d] + _mm_split(jnp.tanh(xw), w2_ref[d], 2, 2)
        lw = -math.exp(-0.5) * _sigmoid(y)
        a = _sigmoid(a0_ref[d] + _mm(xa, a2_ref[d]))
        r_s[d, bi] = r
        v_s[d, bi] = v
        kn_s[d, bi] = kk
        lw_s[d, bi] = lw
        b_s[d, bi] = kk * a
        kd_s[d, bi] = k * (1.0 + (a - 1.0) * ka_ref[...])
        if d == 0:
            rk = _mm_split(r * k * rk_ref[...], head_ones, 2, 1)
            bonus_ref[bi] = rk * v
            gate_ref[bi] = _mm(_sigmoid(xg), g2_ref[...])

    for bi in range(nb):
        prepare(0, bi, zf_ref, zfp_ref, zfn_ref, step)
        prepare(1, bi, zb_ref, zbp_ref, zbn_ref, nblk - 1 - step)

    @pl.when(step == 0)
    def _():
        state[...] = jnp.zeros_like(state)

    ti = lax.broadcasted_iota(jnp.int32, (c, c), 0)
    si = lax.broadcasted_iota(jnp.int32, (c, c), 1)
    tw = lax.broadcasted_iota(jnp.int32, (c, w), 0)
    sw = lax.broadcasted_iota(jnp.int32, (c, w), 1) % RW_HEAD
    eye = (tw == sw).astype(F32)

    def stack(x):
        return jnp.where(same_head, jnp.concatenate([x] * RW_HEADS, axis=0), 0.0).astype(BF16)

    nchunk = tb // c

    def chunk(reverse, bi, cpos):
        d = 1 if reverse else 0
        o_ref = ob_ref if reverse else of_ref
        cum_mat = ((si >= ti) if reverse else (si <= ti)).astype(F32)
        strict = (tw < sw) if reverse else (tw > sw)
        incl = (tw <= sw) if reverse else (tw >= sw)
        sl = pl.ds(pl.multiple_of(cpos * c, c), c)
        lwc = lw_s[d, bi, sl, :]
        l_in = _mm_split(cum_mat, lwc, 1, 3)
        l_tot = jnp.sum(lwc, axis=0, keepdims=True)
        e_in = jnp.exp(l_in)
        e_neg = jnp.exp(-l_in)
        e_tot = jnp.exp(l_tot)
        kap = kn_s[d, bi, sl, :] * jnp.exp(l_in - lwc)
        bt = b_s[d, bi, sl, :] * e_neg
        kt = kd_s[d, bi, sl, :] * e_neg
        rt = (r_s[d, bi, sl, :] * e_in).astype(BF16)
        v = v_s[d, bi, sl, :]
        kap_w, bt_w, kt_w, v_w = stack(kap), stack(bt), stack(kt), stack(v)
        kap = kap.astype(BF16)
        end_w = jnp.concatenate([bt * e_tot, kt * e_tot], axis=0).astype(BF16)
        s_bd = state[d, bi]
        s_w = s_bd.astype(BF16)
        yield
        kap_rt = jnp.concatenate([kap, rt], axis=0)
        am_b = _mm_nt(kap_rt, bt_w)
        a_b = jnp.where(strict, am_b[0:c], 0.0)
        m_b = jnp.where(incl, am_b[c:2 * c], 0.0)
        yield
        am_k = _mm_nt(kap_rt, kt_w)
        a_k = jnp.where(strict, am_k[0:c], 0.0)
        m_k = jnp.where(incl, am_k[c:2 * c], 0.0)
        yield
        assert c == 2 ** int(math.log2(c)) and int(math.log2(c)) % 2 == 0
        powers = [-a_b]
        pairs = []
        akv = _mm(a_k, v_w)
        for level in range(1, int(math.log2(c))):
            sq_w = stack(powers[-1])
            powers.append(_mm(powers[-1], sq_w))
            if level % 2 == 0:
                lo = eye + powers[level - 2]
                pairs.append(lo + _mm(lo, sq_w))
            yield
        lo = eye + powers[-2]
        pairs.append(lo + _mm(lo, stack(powers[-1])))
        tinv = pairs[0]
        for pr in pairs[1:-1]:
            tinv = _mm(tinv, stack(pr))
        yield
        tinv = _mm(tinv, stack(pairs[-1]))
        yield
        pm = _mm(tinv, kap_w)
        qm = _mm(tinv, stack(akv))
        yield
        u = -(_mm_nt(pm, s_w) + qm)
        yield
        o = _mm_nt(rt, s_w) + _mm(m_b, stack(u)) + _mm(m_k, v_w)
        upd = _mm_tn(jnp.concatenate([u, v], axis=0), end_w)
        yield
        o_ref[bi, sl, :] = o
        state[d, bi] = s_bd * e_tot + jnp.where(same_head, upd, 0.0)

    def all_scans(ci, carry):
        scans = [chunk(False, bi, ci) for bi in range(nb)] + [chunk(True, bi, nchunk - 1 - ci) for bi in range(nb)]
        while scans:
            scans = [g for g in scans if next(g, StopIteration) is not StopIteration]
        return carry

    lax.fori_loop(0, nchunk, all_scans, 0)


def _rwkv_scan(z, mu, w0, w2, a0, a2, g2, k_k, k_a, r_k, tb):
    bsz, seq, _ = z.shape
    nblk = seq // tb
    nb = 2 if bsz % 2 == 0 else 1
    z8 = z.reshape(bsz, seq // 8, 8, RW_IN)
    t8 = tb // 8

    def specs(pos):
        z_spec = pl.BlockSpec((nb, tb, RW_IN), lambda b, i: (b, pos(i), 0))
        zp_spec = pl.BlockSpec((nb, None, 8, RW_IN), lambda b, i: (b, jnp.maximum(pos(i) * t8 - 1, 0), 0, 0))
        zn_spec = pl.BlockSpec((nb, None, 8, RW_IN),
                               lambda b, i: (b, jnp.minimum((pos(i) + 1) * t8, seq // 8 - 1), 0, 0))
        o_spec = pl.BlockSpec((nb, tb, RW_W), lambda b, i: (b, pos(i), 0))
        return [z_spec, zp_spec, zn_spec], o_spec

    in_f, o_f = specs(lambda i: i)
    in_b, o_b = specs(lambda i: nblk - 1 - i)
    o_shape = jax.ShapeDtypeStruct((bsz, seq, RW_W), F32)
    weights = [mu, w0, w2, a0, a2, g2, k_k, k_a, r_k]
    return pl.pallas_call(
        functools.partial(_rwkv_kernel, nblk, tb, nb),
        grid=(bsz // nb, nblk),
        in_specs=in_f + in_b + [_full(wt) for wt in weights],
        out_specs=[o_f, o_f, o_f, o_b],
        out_shape=[o_shape] * 4,
        scratch_shapes=[pltpu.VMEM((2, nb, tb, RW_W), F32)] * 6 + [pltpu.VMEM((2, nb, RW_W, RW_W), F32)],
        compiler_params=_params("parallel", "arbitrary"),
        name="rwkv_scan",
    )(z, z8, z8, z, z8, z8, *weights)


def _rope_tables(seq):
    half = MLA_ROPE // 2
    inv = (ROPE_THETA ** (-np.arange(half, dtype=np.float32) / half)).astype(np.float32)
    ang = np.arange(seq, dtype=np.float32)[:, None] * inv[None, :]
    cos = np.cos(ang).astype(np.float32)
    sin = np.sin(ang).astype(np.float32)
    ct = np.zeros((seq, MLA_QK_PAD), np.float32)
    st = np.zeros((seq, MLA_QK_PAD), np.float32)
    ct[:, :MLA_NOPE] = 1.0
    ct[:, MLA_NOPE:MLA_NOPE + half] = cos
    ct[:, MLA_NOPE + half:MLA_NOPE + 2 * half] = cos
    st[:, MLA_NOPE:MLA_NOPE + half] = -sin
    st[:, MLA_NOPE + half:MLA_NOPE + 2 * half] = sin
    return jnp.asarray(ct), jnp.asarray(st), jnp.asarray(ct.T), jnp.asarray(st.T)


def _rope(x, cos_t, sin_t):
    half = MLA_ROPE // 2
    lane = lax.broadcasted_iota(jnp.int32, x.shape, 1)
    swapped = jnp.where(lane < MLA_NOPE + half,
                        pltpu.roll(x, MLA_QK_PAD - half, axis=1), pltpu.roll(x, half, axis=1))
    return x * cos_t + swapped * sin_t


def _rope_rows(x, cos_t, sin_t):
    half = MLA_ROPE // 2
    a, b = MLA_NOPE, MLA_NOPE + half
    swapped = jnp.concatenate([x[0:a], x[b:b + half], x[a:b], x[b + half:]], axis=0)
    return x * cos_t + swapped * sin_t


def _mla_qkv_kernel(z_ref, cos_ref, sin_ref, cosr_ref, sinr_ref, qn_ref, wuq_ref, kvn_ref, wuk_ref, wuv_ref,
                    q_ref, k_ref, v_ref):
    z = z_ref[...]
    cos_t = cos_ref[...]
    sin_t = sin_ref[...]
    cos_r = cosr_ref[...]
    sin_r = sinr_ref[...]
    c_q = z[:, :MLA_Q_LORA]
    c_kv = z[:, MLA_Q_LORA:MLA_Q_LORA + MLA_KV_LORA]
    k_rope = _rope(z[:, MLA_Q_LORA + MLA_KV_LORA:], cos_t, sin_t)
    c_q = (c_q * lax.rsqrt(jnp.mean(c_q * c_q, -1, keepdims=True) + RMS_EPS) * qn_ref[...]).astype(BF16)
    c_kv = (c_kv * lax.rsqrt(jnp.mean(c_kv * c_kv, -1, keepdims=True) + RMS_EPS) * kvn_ref[...]).astype(BF16)
    nt = (((1,), (1,)), ((), ()))
    for h in range(MLA_HEADS):
        q_t = lax.dot_general(wuq_ref[h], c_q, nt, preferred_element_type=F32)
        q_ref[h] = (_rope_rows(q_t, cos_r, sin_r) * (MLA_SCALE * LOG2_E)).astype(BF16)
        kh = jnp.dot(c_kv, wuk_ref[h], preferred_element_type=F32)
        k_ref[h] = (kh + k_rope).astype(BF16)
        v_t = lax.dot_general(wuv_ref[h], c_kv, nt, preferred_element_type=F32)
        v_ref[h] = v_t.astype(BF16)


def _mla_qkv(z, rope, q_norm, w_uq, kv_norm, w_uk, w_uv, seq, tm):
    t = z.shape[0]
    nseq = seq // tm
    tab = pl.BlockSpec((tm, MLA_QK_PAD), lambda i: (i % nseq, 0))
    tab_r = pl.BlockSpec((MLA_QK_PAD, tm), lambda i: (0, i % nseq))
    cols = lambda n: pl.BlockSpec((MLA_HEADS, n, tm), lambda i: (0, 0, i))
    return pl.pallas_call(
        _mla_qkv_kernel,
        grid=(t // tm,),
        in_specs=[pl.BlockSpec((tm, MLA_IN_PAD), lambda i: (i, 0)), tab, tab, tab_r, tab_r,
                  _full(q_norm), _full(w_uq), _full(kv_norm), _full(w_uk), _full(w_uv)],
        out_specs=[cols(MLA_QK_PAD), pl.BlockSpec((MLA_HEADS, tm, MLA_QK_PAD), lambda i: (0, i, 0)),
                   cols(MLA_V)],
        out_shape=[jax.ShapeDtypeStruct((MLA_HEADS, MLA_QK_PAD, t), BF16),
                   jax.ShapeDtypeStruct((MLA_HEADS, t, MLA_QK_PAD), BF16),
                   jax.ShapeDtypeStruct((MLA_HEADS, MLA_V, t), BF16)],
        compiler_params=_params("parallel"),
        name="mla_qkv",
    )(z, *rope, q_norm, w_uq, kv_norm, w_uk, w_uv)


def _mla_attn_kernel(nk, q_ref, k_ref, v_ref, o_ref, m_s, l_s, acc_s, bm_s, bl_s, pv_s):
    j = pl.program_id(2)

    def scores(h):
        return jnp.dot(k_ref[h], q_ref[h], preferred_element_type=F32)

    @pl.when(j == 0)
    def _():
        m_s[...] = jnp.full(m_s.shape, -jnp.inf, F32)
        l_s[...] = jnp.zeros_like(l_s)
        acc_s[...] = jnp.zeros_like(acc_s)
        for h in range(MLA_HEADS):
            bm_s[h:h + 1, :] = jnp.max(scores(h), axis=0, keepdims=True)

    @pl.when(j > 0)
    def _():
        s_next = scores(0)
        for h in range(MLA_HEADS):
            s = s_next
            if h + 1 < MLA_HEADS:
                s_next = scores(h + 1)
            bm_s[h:h + 1, :] = jnp.max(s, axis=0, keepdims=True)
            p = jnp.exp2(s - m_s[h:h + 1, :])
            bl_s[h:h + 1, :] = jnp.sum(p, axis=0, keepdims=True)
            pv_s[h] = jnp.dot(v_ref[h], p.astype(BF16), preferred_element_type=F32)

    stale_shift_ok = jnp.max(bm_s[...] - m_s[...]) < MLA_SHIFT_SLACK

    @pl.when(stale_shift_ok)
    def _():
        for h in range(MLA_HEADS):
            m_prev = m_s[h:h + 1, :]
            m_new = jnp.maximum(m_prev, bm_s[h:h + 1, :])
            alpha = jnp.exp2(m_prev - m_new)
            acc_s[h] = (acc_s[h] + pv_s[h]) * alpha
            l_s[h:h + 1, :] = (l_s[h:h + 1, :] + bl_s[h:h + 1, :]) * alpha
            m_s[h:h + 1, :] = m_new

    @pl.when(jnp.logical_not(stale_shift_ok))
    def _():
        for h in range(MLA_HEADS):
            s = scores(h)
            m_prev = m_s[h:h + 1, :]
            m_new = jnp.maximum(m_prev, bm_s[h:h + 1, :])
            alpha = jnp.exp2(m_prev - m_new)
            p = jnp.exp2(s - m_new)
            l_s[h:h + 1, :] = alpha * l_s[h:h + 1, :] + jnp.sum(p, axis=0, keepdims=True)
            acc_s[h] = alpha * acc_s[h] + jnp.dot(v_ref[h], p.astype(BF16), preferred_element_type=F32)
            m_s[h:h + 1, :] = m_new

    @pl.when(j == nk - 1)
    def _():
        heads = [acc_s[h] / l_s[h:h + 1, :] for h in range(MLA_HEADS)]
        o_ref[...] = jnp.concatenate(heads, axis=0).T


def _mla_attn(q, k, v, bsz, seq, tq, tk):
    nq = seq // tq
    nk = seq // tk
    t = bsz * seq
    return pl.pallas_call(
        functools.partial(_mla_attn_kernel, nk),
        grid=(bsz, nq, nk),
        in_specs=[pl.BlockSpec((MLA_HEADS, MLA_QK_PAD, tq), lambda b, i, j: (0, 0, b * nq + i)),
                  pl.BlockSpec((MLA_HEADS, tk, MLA_QK_PAD), lambda b, i, j: (0, b * nk + j, 0)),
                  pl.BlockSpec((MLA_HEADS, MLA_V, tk), lambda b, i, j: (0, 0, b * nk + j))],
        out_specs=pl.BlockSpec((tq, MLA_W), lambda b, i, j: (b * nq + i, 0)),
        out_shape=jax.ShapeDtypeStruct((t, MLA_W), F32),
        scratch_shapes=[pltpu.VMEM((MLA_HEADS, tq), F32), pltpu.VMEM((MLA_HEADS, tq), F32),
                        pltpu.VMEM((MLA_HEADS, MLA_V, tq), F32),
                        pltpu.VMEM((MLA_HEADS, tq), F32), pltpu.VMEM((MLA_HEADS, tq), F32),
                        pltpu.VMEM((MLA_HEADS, MLA_V, tq), F32)],
        compiler_params=_params("parallel", "parallel", "arbitrary"),
        name="mla_attn",
    )(q, k, v)


def _s5_param_kernel(lre_ref, lim_ref, ldt_ref, btre_ref, btim_ref, cre_ref, cim_ref, ek_ref, ew_ref, ee_ref,
                     kt_ref, wre_ref, wim_ref, ere_ref, eim_ref, are_ref, aim_ref):
    lam_re = lre_ref[...]
    lam_im = lim_ref[...]
    dt = jnp.exp(ldt_ref[...])

    def rows_of(x):
        return jnp.broadcast_to(x[:, None, :], (S5_CHUNK, S5_GROUP, S5_STATE)).reshape(S5_CW, S5_STATE)

    def power(e):
        mag = jnp.exp(lam_re * dt * e)
        ang = lam_im * dt * e
        return mag * jnp.cos(ang), mag * jnp.sin(ang)

    a_re, a_im = power(jnp.ones_like(lam_re))
    den = lam_re * lam_re + lam_im * lam_im
    nr = a_re - 1.0
    coef_re = (nr * lam_re + a_im * lam_im) / den
    coef_im = (a_im * lam_re - nr * lam_im) / den
    bt_re = btre_ref[...]
    bt_im = btim_ref[...]
    bb_re = coef_re * bt_re - coef_im * bt_im
    bb_im = coef_re * bt_im + coef_im * bt_re
    tile = lambda x: jnp.concatenate([x] * S5_CHUNK, axis=0)
    c_re = tile(cre_ref[...])
    c_im = tile(cim_ref[...])
    pk_re, pk_im = map(rows_of, power(ek_ref[...]))
    ck_re = c_re * pk_re - c_im * pk_im
    ck_im = c_re * pk_im + c_im * pk_re
    nt = (((1,), (1,)), ((), ()))
    kt_ref[...] = (lax.dot_general(bb_re, ck_re, nt, precision=HIGHEST, preferred_element_type=F32)
                   - lax.dot_general(bb_im, ck_im, nt, precision=HIGHEST, preferred_element_type=F32))
    pw_re, pw_im = map(rows_of, power(ew_ref[...]))
    bbt_re = tile(bb_re)
    bbt_im = tile(bb_im)
    wre_ref[...] = pw_re * bbt_re - pw_im * bbt_im
    wim_ref[...] = pw_re * bbt_im + pw_im * bbt_re
    pe_re, pe_im = map(rows_of, power(ee_ref[...]))
    ere_ref[...] = c_re * pe_re - c_im * pe_im
    eim_ref[...] = -(c_re * pe_im + c_im * pe_re)
    ac_re, ac_im = power(jnp.full_like(lam_re, float(S5_CHUNK)))
    are_ref[...] = ac_re
    aim_ref[...] = ac_im


def _s5_params(lam_re, lam_im, log_dt, b_re, b_im, c_re, c_im):
    n = 2 * S5_GROUPS
    p, c, cw = S5_STATE, S5_GROUP, S5_CW
    vec = lambda a: a.reshape(n, 1, p)
    ldt = jnp.broadcast_to(log_dt.reshape(n, 1, 1), (n, 1, p))
    bt = lambda a: a.transpose(0, 1, 3, 2).reshape(n, c, p)
    cc = lambda a: a.reshape(n, c, p)
    steps = np.arange(S5_CHUNK, dtype=np.float32)
    rows = lambda e: jnp.asarray(np.broadcast_to(e[:, :, None], (2, S5_CHUNK, p)).copy())
    e_k = rows(np.stack([steps, steps]))
    e_w = rows(np.stack([S5_CHUNK - 1 - steps, steps]))
    e_e = rows(np.stack([steps + 1, S5_CHUNK - steps]))
    per = lambda *shape: pl.BlockSpec((None,) + shape, lambda i: (i,) + (0,) * len(shape))
    per_dir = pl.BlockSpec((None, S5_CHUNK, p), lambda i: (i // S5_GROUPS, 0, 0))
    out = lambda *shape: jax.ShapeDtypeStruct((n,) + shape, F32)
    return pl.pallas_call(
        _s5_param_kernel,
        grid=(n,),
        in_specs=[per(1, p)] * 3 + [per(c, p)] * 4 + [per_dir] * 3,
        out_specs=[per(c, cw)] + [per(cw, p)] * 4 + [per(1, p)] * 2,
        out_shape=[out(c, cw)] + [out(cw, p)] * 4 + [out(1, p)] * 2,
        compiler_params=_params("parallel"),
        name="s5_params",
    )(vec(lam_re), vec(lam_im), ldt, bt(b_re), bt(b_im), cc(c_re), cc(c_im), e_k, e_w, e_e)


def _s5_toeplitz(kt):
    g, c, n = S5_GROUPS, S5_GROUP, S5_CHUNK
    k = kt.reshape(2, g, c, n, c)
    s_in = np.arange(n)[:, None]
    s_out = np.arange(n)[None, :]
    lag_f = np.clip(s_out - s_in, 0, n - 1)
    lag_b = np.clip(s_in - s_out, 0, n - 1)
    m_f = jnp.asarray((s_out >= s_in).astype(np.float32))[None, None, :, :, None]
    m_b = jnp.asarray((s_in >= s_out).astype(np.float32))[None, None, :, :, None]
    m = k[0][:, :, lag_f, :] * m_f + k[1][:, :, lag_b, :] * m_b
    return m.transpose(0, 2, 1, 3, 4).reshape(g, n * c, n * c)


def _s5_operators(lam_re, lam_im, log_dt, b_re, b_im, c_re, c_im):
    kt, w_re, w_im, e_re, e_im, a_re, a_im = _s5_params(lam_re, lam_im, log_dt, b_re, b_im, c_re, c_im)
    g = S5_GROUPS
    parts = lambda re, im: jnp.stack([re[:g], im[:g], re[g:], im[g:]], axis=1)
    return (_s5_toeplitz(kt).astype(BF16), parts(w_re, w_im).astype(BF16), parts(e_re, e_im).astype(BF16),
            parts(a_re[:, 0], a_im[:, 0]))


def _s5_chunk_kernel(nk, rows, u_ref, m_ref, w_ref, e_ref, a_ref, y_ref, wfr_s, wfi_s, wbr_s, wbi_s):
    u = u_ref[...].astype(BF16)
    y_ref[...] = jnp.dot(u, m_ref[...], preferred_element_type=F32)
    for i, dst in enumerate((wfr_s, wfi_s, wbr_s, wbi_s)):
        dst[...] = jnp.dot(u, w_ref[i], preferred_element_type=F32)
    a = [jnp.broadcast_to(a_ref[i:i + 1, :], (rows, S5_STATE)) for i in range(4)]

    def step(k, carry):
        fr, fi, br, bi = carry
        sf = pl.ds(k, rows, stride=nk)
        sb = pl.ds(nk - 1 - k, rows, stride=nk)
        wfr, wfi, wbr, wbi = wfr_s[sf, :], wfi_s[sf, :], wbr_s[sb, :], wbi_s[sb, :]
        wfr_s[sf, :] = fr
        wfi_s[sf, :] = fi
        wbr_s[sb, :] = br
        wbi_s[sb, :] = bi
        return (a[0] * fr - a[1] * fi + wfr, a[0] * fi + a[1] * fr + wfi,
                a[2] * br - a[3] * bi + wbr, a[2] * bi + a[3] * br + wbi)

    zero = jnp.zeros((rows, S5_STATE), F32)
    lax.fori_loop(0, nk, step, (zero, zero, zero, zero))
    carry_in = 0.0
    for i, src in enumerate((wfr_s, wfi_s, wbr_s, wbi_s)):
        carry_in = carry_in + _mm_nt(src[...], e_ref[i])
    y_ref[...] += carry_in


def _s5_scan(z_groups, bsz, seq, operators):
    m_mat, w_mat, e_mat, a_vec = operators
    nk = seq // S5_CHUNK
    r = bsz * nk
    grp = lambda *shape: pl.BlockSpec((None,) + shape, lambda i: (i,) + (0,) * len(shape))
    return pl.pallas_call(
        functools.partial(_s5_chunk_kernel, nk, bsz),
        grid=(S5_GROUPS,),
        in_specs=[grp(r, S5_CW), grp(S5_CW, S5_CW), grp(4, S5_CW, S5_STATE), grp(4, S5_CW, S5_STATE),
                  grp(4, S5_STATE)],
        out_specs=grp(r, S5_CW),
        out_shape=jax.ShapeDtypeStruct((S5_GROUPS, r, S5_CW), F32),
        scratch_shapes=[pltpu.VMEM((r, S5_STATE), F32)] * 4,
        compiler_params=_params("parallel"),
        name="s5_scan",
    )(z_groups, m_mat, w_mat, e_mat, a_vec)


def _s5_post_kernel(y_ref, u_ref, d_ref, w_ref, b_ref, o_ref, rows_s):
    chunks = y_ref.shape[1]
    for g in range(S5_GROUPS):
        y_g = y_ref[g] + d_ref[g:g + 1, :] * u_ref[g]
        for half in range(S5_CW // LANES):
            rows_s[half, pl.ds(g, chunks, stride=S5_GROUPS), :] = y_g[:, half * LANES:(half + 1) * LANES]
    y = jnp.concatenate([rows_s[half] for half in range(S5_CW // LANES)], axis=1)
    y = _to_chunk_rows(y)
    y = jax.nn.gelu(y, approximate=True)
    o_ref[...] = y * _sigmoid(_mm(y, w_ref[...]) + b_ref[...])


def _s5_post(y, u, d_rows, glu_w, glu_b, tm):
    t = u.shape[1] * S5_CHUNK
    grp = pl.BlockSpec((S5_GROUPS, tm // S5_CHUNK, S5_CW), lambda i: (0, i, 0))
    return pl.pallas_call(
        _s5_post_kernel,
        grid=(t // tm,),
        in_specs=[grp, grp, _full(d_rows), _full(glu_w), _full(glu_b)],
        out_specs=pl.BlockSpec((tm, S5_W), lambda i: (i, 0)),
        out_shape=jax.ShapeDtypeStruct((t, S5_W), F32),
        scratch_shapes=[pltpu.VMEM((S5_CW // LANES, tm, LANES), F32)],
        compiler_params=_params("parallel"),
        name="s5_post",
    )(y, u, d_rows, glu_w, glu_b)


def _merge_kernel(x_ref, of_ref, ob_ref, bonus_ref, rgate_ref, mla_ref, s5_ref, g_ref,
                  gng_ref, gnb_ref, wrw_ref, wmla_ref, ws5_ref, wout_ref, lng_ref, lnb_ref, o_ref):
    w = RW_W
    hr = lax.broadcasted_iota(jnp.int32, (w, w), 0) // RW_HEAD
    hc = lax.broadcasted_iota(jnp.int32, (w, w), 1) // RW_HEAD
    head_mean = (hr == hc).astype(F32) * (1.0 / RW_HEAD)
    o = of_ref[...] + ob_ref[...]
    mean = _mm_split(o, head_mean, 2, 1)
    oc = o - mean
    var = _mm_split(oc * oc, head_mean, 2, 1)
    o = oc * lax.rsqrt(var + RW_GN_EPS) * gng_ref[...] + gnb_ref[...]
    y_rw = _mm((o + bonus_ref[...]) * rgate_ref[...], wrw_ref[...])
    y_mla = _mm(mla_ref[...], wmla_ref[...])
    y_s5 = _mm(s5_ref[...], ws5_ref[...])
    d = D_MODEL
    merged = g_ref[:, 0:d] * y_rw + g_ref[:, d:2 * d] * y_mla + g_ref[:, 2 * d:3 * d] * y_s5
    o_ref[...] = _layer_norm(DN_ALPHA * x_ref[...] + _mm(merged, wout_ref[...]), lng_ref[...], lnb_ref[...])


def _merge(x, o_f, o_b, bonus, rgate, o_mla, y_s5, gates, gn_g, gn_b, w_rw, w_mla, w_s5, w_out, ln_g, ln_b, tm):
    t = x.shape[0]
    row = lambda n: pl.BlockSpec((tm, n), lambda i: (i, 0))
    weights = [gn_g, gn_b, w_rw, w_mla, w_s5, w_out, ln_g, ln_b]
    return pl.pallas_call(
        _merge_kernel,
        grid=(t // tm,),
        in_specs=[row(D_MODEL), row(RW_W), row(RW_W), row(RW_W), row(RW_W), row(MLA_W), row(S5_W),
                  row(N_BRANCH * D_MODEL)] + [_full(wt) for wt in weights],
        out_specs=row(D_MODEL),
        out_shape=jax.ShapeDtypeStruct((t, D_MODEL), F32),
        compiler_params=_params("parallel"),
        name="merge_ln1",
    )(x, o_f, o_b, bonus, rgate, o_mla, y_s5, gates, *weights)


def _mlp_kernel(nf, x_ref, w1_ref, w2_ref, lng_ref, lnb_ref, o_ref, acc_s):
    j = pl.program_id(1)

    @pl.when(j == 0)
    def _():
        acc_s[...] = jnp.zeros_like(acc_s)

    h = jnp.maximum(_mm(x_ref[...], w1_ref[...]), 0.0)
    acc_s[...] += _mm(h * h, w2_ref[...])

    @pl.when(j == nf - 1)
    def _():
        o_ref[...] = _layer_norm(DN_ALPHA * x_ref[...] + acc_s[...], lng_ref[...], lnb_ref[...])


def _mlp(x, w1, w2, ln_g, ln_b, tm, tf):
    t = x.shape[0]
    nf = D_FF // tf
    return pl.pallas_call(
        functools.partial(_mlp_kernel, nf),
        grid=(t // tm, nf),
        in_specs=[pl.BlockSpec((tm, D_MODEL), lambda i, j: (i, 0)),
                  pl.BlockSpec((D_MODEL, tf), lambda i, j: (0, j)),
                  pl.BlockSpec((tf, D_MODEL), lambda i, j: (j, 0)),
                  _full(ln_g), _full(ln_b)],
        out_specs=pl.BlockSpec((tm, D_MODEL), lambda i, j: (i, 0)),
        out_shape=jax.ShapeDtypeStruct((t, D_MODEL), F32),
        scratch_shapes=[pltpu.VMEM((tm, D_MODEL), F32)],
        compiler_params=_params("parallel", "arbitrary"),
        name="mlp_ln2",
    )(x, w1, w2, ln_g, ln_b)


def _tile(n, pref):
    t = min(n, pref)
    assert n % t == 0, (n, pref)
    return t


def _prep_layer(w_in, rw_mu, rw_w0, rw_w2, rw_a0, rw_a2, rw_g2, rw_k_k, rw_k_a, rw_r_k, rw_gn_g, rw_gn_b, rw_proj,
                mla_q_norm, mla_w_uq, mla_kv_norm, mla_w_ukv, mla_proj,
                s5_lam_re, s5_lam_im, s5_log_dt, s5_b_re, s5_b_im, s5_c_re, s5_c_im, s5_d, s5_glu_w, s5_glu_b,
                s5_proj, w_out, ln1_g, ln1_b, mlp_w1, mlp_w2, ln2_g, ln2_b):
    row = lambda a: a.reshape(1, -1)
    p = {}
    p["w_rw"] = w_in[:, :OFF_MLA].astype(BF16)
    w_mla = w_in[:, OFF_MLA:OFF_S5]
    zeros = lambda n: jnp.zeros((D_MODEL, n), F32)
    p["w_mla"] = jnp.concatenate([w_mla[:, :MLA_Q_LORA + MLA_KV_LORA], zeros(MLA_NOPE),
                                  w_mla[:, MLA_Q_LORA + MLA_KV_LORA:],
                                  zeros(MLA_QK_PAD - MLA_NOPE - MLA_ROPE)], axis=1).astype(BF16)
    p["w_s5"] = w_in[:, OFF_S5:OFF_GATE].astype(BF16)
    p["w_gate"] = w_in[:, OFF_GATE:].astype(BF16)
    p["rw"] = [row(rw_mu)]
    p["rw_dir"] = [rw_w0[:, None, :], rw_w2, rw_a0[:, None, :], rw_a2]
    p["rw_shared"] = [rw_g2, row(rw_k_k), row(rw_k_a), row(rw_r_k)]
    p["rw_gn"] = [row(rw_gn_g), row(rw_gn_b)]
    p["rw_proj"] = rw_proj.astype(BF16)
    uq = mla_w_uq.reshape(MLA_Q_LORA, MLA_HEADS, MLA_NOPE + MLA_ROPE)
    uq = jnp.pad(uq, ((0, 0), (0, 0), (0, MLA_QK_PAD - MLA_NOPE - MLA_ROPE)))
    p["w_uq"] = uq.transpose(1, 2, 0).astype(BF16)
    ukv = mla_w_ukv.reshape(MLA_KV_LORA, MLA_HEADS, MLA_NOPE + MLA_V)
    uk = jnp.pad(ukv[:, :, :MLA_NOPE], ((0, 0), (0, 0), (0, MLA_QK_PAD - MLA_NOPE)))
    p["w_uk"] = uk.transpose(1, 0, 2).astype(BF16)
    p["w_uv"] = ukv[:, :, MLA_NOPE:].transpose(1, 2, 0).astype(BF16)
    p["q_norm"] = row(mla_q_norm)
    p["kv_norm"] = row(mla_kv_norm)
    p["mla_proj"] = mla_proj.astype(BF16)
    p["s5_ops"] = _s5_operators(s5_lam_re, s5_lam_im, s5_log_dt, s5_b_re, s5_b_im, s5_c_re, s5_c_im)
    d_rows = jnp.broadcast_to(s5_d.reshape(S5_GROUPS, 1, S5_GROUP), (S5_GROUPS, S5_CHUNK, S5_GROUP))
    p["s5_post"] = [d_rows.reshape(S5_GROUPS, S5_CW), s5_glu_w.astype(BF16), row(s5_glu_b)]
    p["s5_proj"] = s5_proj.astype(BF16)
    p["w_out"] = w_out.astype(BF16)
    p["ln1"] = [row(ln1_g), row(ln1_b)]
    p["w1"] = mlp_w1.astype(BF16)
    p["w2"] = mlp_w2.astype(BF16)
    p["ln2"] = [row(ln2_g), row(ln2_b)]
    return p


def _layer(x, p, bsz, seq, rope):
    t = bsz * seq
    tm = _tile(t, 512)
    z_rw, z_mla, z_s5, gates = _inproj(x, p["w_rw"], p["w_mla"], p["w_s5"], p["w_gate"], _tile(t, 256))

    tb = _tile(seq, 512)
    z3 = z_rw.reshape(bsz, seq, RW_IN)
    o_f, bonus, rgate, o_b = _rwkv_scan(z3, *p["rw"], *p["rw_dir"], *p["rw_shared"], tb=tb)
    flat = lambda a: a.reshape(t, RW_W)

    q, k, v = _mla_qkv(z_mla, rope, p["q_norm"], p["w_uq"], p["kv_norm"], p["w_uk"], p["w_uv"],
                       seq, _tile(seq, 512))
    o_mla = _mla_attn(q, k, v, bsz, seq, _tile(seq, 1024), _tile(seq, 1024))

    y_s5 = _s5_post(_s5_scan(z_s5, bsz, seq, p["s5_ops"]), z_s5, *p["s5_post"], _tile(t, 1024))

    x1 = _merge(x, flat(o_f), flat(o_b), flat(bonus), flat(rgate), o_mla, y_s5, gates,
                *p["rw_gn"], p["rw_proj"], p["mla_proj"], p["s5_proj"], p["w_out"], *p["ln1"], tm)
    return _mlp(x1, p["w1"], p["w2"], *p["ln2"], _tile(t, 1024), 1024)


def _trunk(x, layers):
    bsz, seq, _ = x.shape
    assert seq % RW_CHUNK == 0 and seq % S5_CHUNK == 0
    rope = _rope_tables(seq)
    h = x.reshape(bsz * seq, D_MODEL)
    for p in layers:
        h = _layer(h, p, bsz, seq, rope)
    return h.reshape(bsz, seq, D_MODEL)


def kernel(x_prompt, x_sample, w_in, rw_mu, rw_w0, rw_w2, rw_a0, rw_a2, rw_g2, rw_k_k, rw_k_a, rw_r_k, rw_gn_g, rw_gn_b, rw_proj, mla_q_norm, mla_w_uq, mla_kv_norm, mla_w_ukv, mla_proj, s5_lam_re, s5_lam_im, s5_log_dt, s5_b_re, s5_b_im, s5_c_re, s5_c_im, s5_d, s5_glu_w, s5_glu_b, s5_proj, w_out, ln1_g, ln1_b, mlp_w1, mlp_w2, ln2_g, ln2_b):
    weights = (w_in, rw_mu, rw_w0, rw_w2, rw_a0, rw_a2, rw_g2, rw_k_k, rw_k_a, rw_r_k, rw_gn_g, rw_gn_b, rw_proj,
               mla_q_norm, mla_w_uq, mla_kv_norm, mla_w_ukv, mla_proj,
               s5_lam_re, s5_lam_im, s5_log_dt, s5_b_re, s5_b_im, s5_c_re, s5_c_im, s5_d, s5_glu_w, s5_glu_b,
               s5_proj, w_out, ln1_g, ln1_b, mlp_w1, mlp_w2, ln2_g, ln2_b)
    layers = [_prep_layer(*[wt[l] for wt in weights]) for l in range(w_in.shape[0])]
    return _trunk(x_prompt, layers), _trunk(x_sample, layers)
```

```python
import functools
import math

import numpy as np
import jax
import jax.numpy as jnp
from jax import lax
from jax.experimental import pallas as pl
from jax.experimental.pallas import tpu as pltpu

F32 = jnp.float32
BF16 = jnp.bfloat16
HIGHEST = lax.Precision.HIGHEST

D_MODEL = 1024
DEPTH = 2
RW_HEADS = 4
RW_HEAD = 64
RW_W = RW_HEADS * RW_HEAD
RW_LORA_W = 32
RW_LORA_A = 32
RW_LORA_G = 64
RW_GN_EPS = 64e-5
RW_IN = 3 * RW_W + RW_LORA_W + RW_LORA_A + RW_LORA_G
MLA_HEADS = 8
MLA_NOPE = 64
MLA_ROPE = 32
MLA_V = 64
MLA_Q_LORA = 256
MLA_KV_LORA = 128
MLA_W = MLA_HEADS * MLA_V
MLA_QK_PAD = 128
MLA_IN_PAD = MLA_Q_LORA + MLA_KV_LORA + MLA_QK_PAD
MLA_SCALE = (MLA_NOPE + MLA_ROPE) ** -0.5
LOG2_E = math.log2(math.e)
MLA_SHIFT_SLACK = 64.0
ROPE_THETA = 10000.0
RMS_EPS = 1e-6
S5_W = 256
S5_GROUP = 16
S5_GROUPS = S5_W // S5_GROUP
S5_STATE = 64
S5_CHUNK = 16
S5_CW = S5_CHUNK * S5_GROUP
LANES = 128
D_FF = 4 * D_MODEL
LN_EPS = 1e-5
N_BRANCH = 3
DN_ALPHA = (2 * DEPTH) ** 0.25
OFF_MLA = RW_IN
OFF_S5 = OFF_MLA + MLA_Q_LORA + MLA_KV_LORA + MLA_ROPE
OFF_GATE = OFF_S5 + S5_W
RW_CHUNK = 64
VMEM_LIMIT = 56 * 1024 * 1024


def _params(*sem):
    return pltpu.CompilerParams(dimension_semantics=sem, vmem_limit_bytes=VMEM_LIMIT)


def _mm(a, b):
    return jnp.dot(a.astype(BF16), b.astype(BF16), preferred_element_type=F32)


def _mm_nt(a, b):
    return lax.dot_general(a.astype(BF16), b.astype(BF16), (((1,), (1,)), ((), ())),
                           preferred_element_type=F32)


def _mm_tn(a, b):
    return lax.dot_general(a.astype(BF16), b.astype(BF16), (((0,), (0,)), ((), ())),
                           preferred_element_type=F32)


def _mm_f32(a, b):
    return jnp.dot(a, b, preferred_element_type=F32, precision=HIGHEST)


def _bf16_terms(x, n):
    terms = []
    for _ in range(n):
        t = x.astype(BF16)
        terms.append(t)
        x = x - t.astype(F32)
    return terms


def _mm_split(a, b, a_terms, b_terms):
    at = _bf16_terms(a, a_terms)
    bt = _bf16_terms(b, b_terms)
    out = None
    for i, x in enumerate(at):
        for j, y in enumerate(bt):
            if i + j < max(a_terms, b_terms):
                d = jnp.dot(x, y, preferred_element_type=F32)
                out = d if out is None else out + d
    return out


def _sigmoid(x):
    return 1.0 / (1.0 + jnp.exp(-x))


def _full(a):
    nd = a.ndim
    return pl.BlockSpec(a.shape, lambda *_: (0,) * nd)


def _layer_norm(x, g, b):
    mu = jnp.mean(x, -1, keepdims=True)
    xc = x - mu
    var = jnp.mean(xc * xc, -1, keepdims=True)
    return xc * lax.rsqrt(var + LN_EPS) * g + b


def _to_chunk_rows(x):
    n = S5_CHUNK
    assert n == S5_GROUPS and x.shape[1] == S5_CW and x.shape[0] % n == 0
    x = x.reshape(x.shape[0] // n, n, S5_CW)
    row = lax.broadcasted_iota(jnp.int32, (1, n, S5_CW), 1)
    pkt = lax.broadcasted_iota(jnp.int32, (1, n, S5_CW), 2) // S5_GROUP
    d = 1
    while d < n:
        row_bit = (row // d) % 2
        pkt_bit = (pkt // d) % 2
        up = pltpu.roll(pltpu.roll(x, n - d, axis=1), S5_GROUP * d, axis=2)
        down = pltpu.roll(pltpu.roll(x, d, axis=1), S5_CW - S5_GROUP * d, axis=2)
        x = jnp.where(row_bit == pkt_bit, x, jnp.where(row_bit == 0, up, down))
        d *= 2
    return x.reshape(x.shape[0] * n, S5_CW)


def _inproj_kernel(x_ref, wrw_ref, wmla_ref, ws5_ref, wg_ref, zrw_ref, zmla_ref, zs5_ref, g_ref, rows_s):
    xb = x_ref[...].astype(BF16)
    zrw_ref[...] = jnp.dot(xb, wrw_ref[...], preferred_element_type=F32)
    zmla_ref[...] = jnp.dot(xb, wmla_ref[...], preferred_element_type=F32)
    z_rows = _to_chunk_rows(jnp.dot(xb, ws5_ref[...], preferred_element_type=F32))
    chunks = z_rows.shape[0] // S5_GROUPS
    for half in range(S5_CW // LANES):
        lanes = slice(half * LANES, (half + 1) * LANES)
        rows_s[half] = z_rows[:, lanes]
        for g in range(S5_GROUPS):
            zs5_ref[g, :, lanes] = rows_s[half, pl.ds(g, chunks, stride=S5_GROUPS), :]
    g_ref[...] = _sigmoid(jnp.dot(xb, wg_ref[...], preferred_element_type=F32)).astype(g_ref.dtype)


def _inproj(x, w_rw, w_mla, w_s5, w_g, tm):
    t = x.shape[0]
    row = lambda n: pl.BlockSpec((tm, n), lambda i: (i, 0))
    return pl.pallas_call(
        _inproj_kernel,
        grid=(t // tm,),
        in_specs=[row(D_MODEL), _full(w_rw), _full(w_mla), _full(w_s5), _full(w_g)],
        out_specs=[row(RW_IN), row(MLA_IN_PAD),
                   pl.BlockSpec((S5_GROUPS, tm // S5_CHUNK, S5_CW), lambda i: (0, i, 0)), row(N_BRANCH * D_MODEL)],
        out_shape=[jax.ShapeDtypeStruct((t, RW_IN), F32), jax.ShapeDtypeStruct((t, MLA_IN_PAD), F32),
                   jax.ShapeDtypeStruct((S5_GROUPS, t // S5_CHUNK, S5_CW), F32),
                   jax.ShapeDtypeStruct((t, N_BRANCH * D_MODEL), BF16)],
        scratch_shapes=[pltpu.VMEM((S5_CW // LANES, tm, LANES), F32)],
        compiler_params=_params("parallel"),
        name="inproj",
    )(x, w_rw, w_mla, w_s5, w_g)


def _rwkv_kernel(nblk, tb, nb,
                 zf_ref, zfp_ref, zfn_ref, zb_ref, zbp_ref, zbn_ref,
                 mu_ref, w0_ref, w2_ref, a0_ref, a2_ref, g2_ref, kk_ref, ka_ref, rk_ref,
                 of_ref, bonus_ref, gate_ref, ob_ref, r_s, v_s, kn_s, lw_s, b_s, kd_s, state):
    step = pl.program_id(1)
    c = RW_CHUNK
    w = RW_W
    hr = lax.broadcasted_iota(jnp.int32, (w, w), 0) // RW_HEAD
    hc = lax.broadcasted_iota(jnp.int32, (w, w), 1) // RW_HEAD
    same_head = hr == hc
    head_ones = same_head.astype(F32)

    def prepare(d, bi, z_ref, zp_ref, zn_ref, blk):
        z = z_ref[bi]
        prev_row = jnp.where(blk == 0, 0.0, zp_ref[bi, 7:8, :])
        next_row = jnp.where(blk == nblk - 1, 0.0, zn_ref[bi, 0:1, :])
        tile_row = lax.broadcasted_iota(jnp.int32, (8, RW_IN), 0)
        z_prev = pltpu.roll(z, 1, axis=0)
        z_prev = jnp.concatenate([jnp.where(tile_row == 0, prev_row, z_prev[0:8]), z_prev[8:]], axis=0)
        z_next = pltpu.roll(z, tb - 1, axis=0)
        z_next = jnp.concatenate([z_next[:tb - 8], jnp.where(tile_row == 7, next_row, z_next[tb - 8:])], axis=0)
        mu = mu_ref[...]
        z = (1.0 - mu) * z + (0.5 * mu) * (z_prev + z_next)
        r = z[:, 0:w]
        k = z[:, w:2 * w]
        v = z[:, 2 * w:3 * w]
        xw = z[:, 3 * w:3 * w + RW_LORA_W]
        xa = z[:, 3 * w + RW_LORA_W:3 * w + RW_LORA_W + RW_LORA_A]
        xg = z[:, 3 * w + RW_LORA_W + RW_LORA_A:]
        kk = k * kk_ref[...]
        kk_ss = _mm_split(kk * kk, head_ones, 2, 1)
        kk = kk * lax.rsqrt(jnp.maximum(kk_ss, 1e-12))
        y = w0_ref[d] + _mm_split(jnp.tanh(xw), w2_ref[d], 2, 2)
        lw = -math.exp(-0.5) * _sigmoid(y)
        a = _sigmoid(a0_ref[d] + _mm(xa, a2_ref[d]))
        r_s[d, bi] = r
        v_s[d, bi] = v
        kn_s[d, bi] = kk
        lw_s[d, bi] = lw
        b_s[d, bi] = kk * a
        kd_s[d, bi] = k * (1.0 + (a - 1.0) * ka_ref[...])
        if d == 0:
            rk = _mm_split(r * k * rk_ref[...], head_ones, 2, 1)
            bonus_ref[bi] = rk * v
            gate_ref[bi] = _mm(_sigmoid(xg), g2_ref[...])

    for bi in range(nb):
        prepare(0, bi, zf_ref, zfp_ref, zfn_ref, step)
        prepare(1, bi, zb_ref, zbp_ref, zbn_ref, nblk - 1 - step)

    @pl.when(step == 0)
    def _():
        state[...] = jnp.zeros_like(state)

    ti = lax.broadcasted_iota(jnp.int32, (c, c), 0)
    si = lax.broadcasted_iota(jnp.int32, (c, c), 1)
    tw = lax.broadcasted_iota(jnp.int32, (c, w), 0)
    sw = lax.broadcasted_iota(jnp.int32, (c, w), 1) % RW_HEAD
    eye = (tw == sw).astype(F32)

    def stack(x):
        return jnp.where(same_head, jnp.concatenate([x] * RW_HEADS, axis=0), 0.0).astype(BF16)

    nchunk = tb // c

    def chunk(reverse, bi, cpos):
        d = 1 if reverse else 0
        o_ref = ob_ref if reverse else of_ref
        cum_mat = ((si >= ti) if reverse else (si <= ti)).astype(F32)
        strict = (tw < sw) if reverse else (tw > sw)
        incl = (tw <= sw) if reverse else (tw >= sw)
        sl = pl.ds(pl.multiple_of(cpos * c, c), c)
        lwc = lw_s[d, bi, sl, :]
        l_in = _mm_split(cum_mat, lwc, 1, 3)
        l_tot = jnp.sum(lwc, axis=0, keepdims=True)
        e_in = jnp.exp(l_in)
        e_neg = jnp.exp(-l_in)
        e_tot = jnp.exp(l_tot)
        kap = kn_s[d, bi, sl, :] * jnp.exp(l_in - lwc)
        bt = b_s[d, bi, sl, :] * e_neg
        kt = kd_s[d, bi, sl, :] * e_neg
        rt = (r_s[d, bi, sl, :] * e_in).astype(BF16)
        v = v_s[d, bi, sl, :]
        kap_w, bt_w, kt_w, v_w = stack(kap), stack(bt), stack(kt), stack(v)
        kap = kap.astype(BF16)
        end_w = jnp.concatenate([bt * e_tot, kt * e_tot], axis=0).astype(BF16)
        s_bd = state[d, bi]
        s_w = s_bd.astype(BF16)
        yield
        kap_rt = jnp.concatenate([kap, rt], axis=0)
        am_b = _mm_nt(kap_rt, bt_w)
        a_b = jnp.where(strict, am_b[0:c], 0.0)
        m_b = jnp.where(incl, am_b[c:2 * c], 0.0)
        yield
        am_k = _mm_nt(kap_rt, kt_w)
        a_k = jnp.where(strict, am_k[0:c], 0.0)
        m_k = jnp.where(incl, am_k[c:2 * c], 0.0)
        yield
        assert c == 2 ** int(math.log2(c)) and int(math.log2(c)) % 2 == 0
        powers = [-a_b]
        pairs = []
        akv = _mm(a_k, v_w)
        for level in range(1, int(math.log2(c))):
            sq_w = stack(powers[-1])
            powers.append(_mm(powers[-1], sq_w))
            if level % 2 == 0:
                lo = eye + powers[level - 2]
                pairs.append(lo + _mm(lo, sq_w))
            yield
        lo = eye + powers[-2]
        pairs.append(lo + _mm(lo, stack(powers[-1])))
        tinv = pairs[0]
        for pr in pairs[1:-1]:
            tinv = _mm(tinv, stack(pr))
        yield
        tinv = _mm(tinv, stack(pairs[-1]))
        yield
        pm = _mm(tinv, kap_w)
        qm = _mm(tinv, stack(akv))
        yield
        u = -(_mm_nt(pm, s_w) + qm)
        yield
        o = _mm_nt(rt, s_w) + _mm(m_b, stack(u)) + _mm(m_k, v_w)
        upd = _mm_tn(jnp.concatenate([u, v], axis=0), end_w)
        yield
        o_ref[bi, sl, :] = o
        state[d, bi] = s_bd * e_tot + jnp.where(same_head, upd, 0.0)

    def all_scans(ci, carry):
        scans = [chunk(False, bi, ci) for bi in range(nb)] + [chunk(True, bi, nchunk - 1 - ci) for bi in range(nb)]
        while scans:
            scans = [g for g in scans if next(g, StopIteration) is not StopIteration]
        return carry

    lax.fori_loop(0, nchunk, all_scans, 0)


def _rwkv_scan(z, mu, w0, w2, a0, a2, g2, k_k, k_a, r_k, tb):
    bsz, seq, _ = z.shape
    nblk = seq // tb
    nb = 2 if bsz % 2 == 0 else 1
    z8 = z.reshape(bsz, seq // 8, 8, RW_IN)
    t8 = tb // 8

    def specs(pos):
        z_spec = pl.BlockSpec((nb, tb, RW_IN), lambda b, i: (b, pos(i), 0))
        zp_spec = pl.BlockSpec((nb, None, 8, RW_IN), lambda b, i: (b, jnp.maximum(pos(i) * t8 - 1, 0), 0, 0))
        zn_spec = pl.BlockSpec((nb, None, 8, RW_IN),
                               lambda b, i: (b, jnp.minimum((pos(i) + 1) * t8, seq // 8 - 1), 0, 0))
        o_spec = pl.BlockSpec((nb, tb, RW_W), lambda b, i: (b, pos(i), 0))
        return [z_spec, zp_spec, zn_spec], o_spec

    in_f, o_f = specs(lambda i: i)
    in_b, o_b = specs(lambda i: nblk - 1 - i)
    o_shape = jax.ShapeDtypeStruct((bsz, seq, RW_W), F32)
    weights = [mu, w0, w2, a0, a2, g2, k_k, k_a, r_k]
    return pl.pallas_call(
        functools.partial(_rwkv_kernel, nblk, tb, nb),
        grid=(bsz // nb, nblk),
        in_specs=in_f + in_b + [_full(wt) for wt in weights],
        out_specs=[o_f, o_f, o_f, o_b],
        out_shape=[o_shape] * 4,
        scratch_shapes=[pltpu.VMEM((2, nb, tb, RW_W), F32)] * 6 + [pltpu.VMEM((2, nb, RW_W, RW_W), F32)],
        compiler_params=_params("parallel", "arbitrary"),
        name="rwkv_scan",
    )(z, z8, z8, z, z8, z8, *weights)


def _rope_tables(seq):
    half = MLA_ROPE // 2
    inv = (ROPE_THETA ** (-np.arange(half, dtype=np.float32) / half)).astype(np.float32)
    ang = np.arange(seq, dtype=np.float32)[:, None] * inv[None, :]
    cos = np.cos(ang).astype(np.float32)
    sin = np.sin(ang).astype(np.float32)
    ct = np.zeros((seq, MLA_QK_PAD), np.float32)
    st = np.zeros((seq, MLA_QK_PAD), np.float32)
    ct[:, :MLA_NOPE] = 1.0
    ct[:, MLA_NOPE:MLA_NOPE + half] = cos
    ct[:, MLA_NOPE + half:MLA_NOPE + 2 * half] = cos
    st[:, MLA_NOPE:MLA_NOPE + half] = -sin
    st[:, MLA_NOPE + half:MLA_NOPE + 2 * half] = sin
    return jnp.asarray(ct), jnp.asarray(st), jnp.asarray(ct.T), jnp.asarray(st.T)


def _rope(x, cos_t, sin_t):
    half = MLA_ROPE // 2
    lane = lax.broadcasted_iota(jnp.int32, x.shape, 1)
    swapped = jnp.where(lane < MLA_NOPE + half,
                        pltpu.roll(x, MLA_QK_PAD - half, axis=1), pltpu.roll(x, half, axis=1))
    return x * cos_t + swapped * sin_t


def _rope_rows(x, cos_t, sin_t):
    half = MLA_ROPE // 2
    a, b = MLA_NOPE, MLA_NOPE + half
    swapped = jnp.concatenate([x[0:a], x[b:b + half], x[a:b], x[b + half:]], axis=0)
    return x * cos_t + swapped * sin_t


def _mla_qkv_kernel(z_ref, cos_ref, sin_ref, cosr_ref, sinr_ref, qn_ref, wuq_ref, kvn_ref, wuk_ref, wuv_ref,
                    q_ref, k_ref, v_ref):
    z = z_ref[...]
    cos_t = cos_ref[...]
    sin_t = sin_ref[...]
    cos_r = cosr_ref[...]
    sin_r = sinr_ref[...]
    c_q = z[:, :MLA_Q_LORA]
    c_kv = z[:, MLA_Q_LORA:MLA_Q_LORA + MLA_KV_LORA]
    k_rope = _rope(z[:, MLA_Q_LORA + MLA_KV_LORA:], cos_t, sin_t)
    c_q = (c_q * lax.rsqrt(jnp.mean(c_q * c_q, -1, keepdims=True) + RMS_EPS) * qn_ref[...]).astype(BF16)
    c_kv = (c_kv * lax.rsqrt(jnp.mean(c_kv * c_kv, -1, keepdims=True) + RMS_EPS) * kvn_ref[...]).astype(BF16)
    nt = (((1,), (1,)), ((), ()))
    for h in range(MLA_HEADS):
        q_t = lax.dot_general(wuq_ref[h], c_q, nt, preferred_element_type=F32)
        q_ref[h] = (_rope_rows(q_t, cos_r, sin_r) * (MLA_SCALE * LOG2_E)).astype(BF16)
        kh = jnp.dot(c_kv, wuk_ref[h], preferred_element_type=F32)
        k_ref[h] = (kh + k_rope).astype(BF16)
        v_t = lax.dot_general(wuv_ref[h], c_kv, nt, preferred_element_type=F32)
        v_ref[h] = v_t.astype(BF16)


def _mla_qkv(z, rope, q_norm, w_uq, kv_norm, w_uk, w_uv, seq, tm):
    t = z.shape[0]
    nseq = seq // tm
    tab = pl.BlockSpec((tm, MLA_QK_PAD), lambda i: (i % nseq, 0))
    tab_r = pl.BlockSpec((MLA_QK_PAD, tm), lambda i: (0, i % nseq))
    cols = lambda n: pl.BlockSpec((MLA_HEADS, n, tm), lambda i: (0, 0, i))
    return pl.pallas_call(
        _mla_qkv_kernel,
        grid=(t // tm,),
        in_specs=[pl.BlockSpec((tm, MLA_IN_PAD), lambda i: (i, 0)), tab, tab, tab_r, tab_r,
                  _full(q_norm), _full(w_uq), _full(kv_norm), _full(w_uk), _full(w_uv)],
        out_specs=[cols(MLA_QK_PAD), pl.BlockSpec((MLA_HEADS, tm, MLA_QK_PAD), lambda i: (0, i, 0)),
                   cols(MLA_V)],
        out_shape=[jax.ShapeDtypeStruct((MLA_HEADS, MLA_QK_PAD, t), BF16),
                   jax.ShapeDtypeStruct((MLA_HEADS, t, MLA_QK_PAD), BF16),
                   jax.ShapeDtypeStruct((MLA_HEADS, MLA_V, t), BF16)],
        compiler_params=_params("parallel"),
        name="mla_qkv",
    )(z, *rope, q_norm, w_uq, kv_norm, w_uk, w_uv)


def _mla_attn_kernel(nk, q_ref, k_ref, v_ref, o_ref, m_s, l_s, acc_s, bm_s, bl_s, pv_s):
    j = pl.program_id(2)

    def scores(h):
        return jnp.dot(k_ref[h], q_ref[h], preferred_element_type=F32)

    @pl.when(j == 0)
    def _():
        m_s[...] = jnp.full(m_s.shape, -jnp.inf, F32)
        l_s[...] = jnp.zeros_like(l_s)
        acc_s[...] = jnp.zeros_like(acc_s)
        for h in range(MLA_HEADS):
            bm_s[h:h + 1, :] = jnp.max(scores(h), axis=0, keepdims=True)

    @pl.when(j > 0)
    def _():
        s_next = scores(0)
        for h in range(MLA_HEADS):
            s = s_next
            if h + 1 < MLA_HEADS:
                s_next = scores(h + 1)
            bm_s[h:h + 1, :] = jnp.max(s, axis=0, keepdims=True)
            p = jnp.exp2(s - m_s[h:h + 1, :])
            bl_s[h:h + 1, :] = jnp.sum(p, axis=0, keepdims=True)
            pv_s[h] = jnp.dot(v_ref[h], p.astype(BF16), preferred_element_type=F32)

    stale_shift_ok = jnp.max(bm_s[...] - m_s[...]) < MLA_SHIFT_SLACK

    @pl.when(stale_shift_ok)
    def _():
        for h in range(MLA_HEADS):
            m_prev = m_s[h:h + 1, :]
            m_new = jnp.maximum(m_prev, bm_s[h:h + 1, :])
            alpha = jnp.exp2(m_prev - m_new)
            acc_s[h] = (acc_s[h] + pv_s[h]) * alpha
            l_s[h:h + 1, :] = (l_s[h:h + 1, :] + bl_s[h:h + 1, :]) * alpha
            m_s[h:h + 1, :] = m_new

    @pl.when(jnp.logical_not(stale_shift_ok))
    def _():
        for h in range(MLA_HEADS):
            s = scores(h)
            m_prev = m_s[h:h + 1, :]
            m_new = jnp.maximum(m_prev, bm_s[h:h + 1, :])
            alpha = jnp.exp2(m_prev - m_new)
            p = jnp.exp2(s - m_new)
            l_s[h:h + 1, :] = alpha * l_s[h:h + 1, :] + jnp.sum(p, axis=0, keepdims=True)
            acc_s[h] = alpha * acc_s[h] + jnp.dot(v_ref[h], p.astype(BF16), preferred_element_type=F32)
            m_s[h:h + 1, :] = m_new

    @pl.when(j == nk - 1)
    def _():
        heads = [acc_s[h] / l_s[h:h + 1, :] for h in range(MLA_HEADS)]
        o_ref[...] = jnp.concatenate(heads, axis=0).T


def _mla_attn(q, k, v, bsz, seq, tq, tk):
    nq = seq // tq
    nk = seq // tk
    t = bsz * seq
    return pl.pallas_call(
        functools.partial(_mla_attn_kernel, nk),
        grid=(bsz, nq, nk),
        in_specs=[pl.BlockSpec((MLA_HEADS, MLA_QK_PAD, tq), lambda b, i, j: (0, 0, b * nq + i)),
                  pl.BlockSpec((MLA_HEADS, tk, MLA_QK_PAD), lambda b, i, j: (0, b * nk + j, 0)),
                  pl.BlockSpec((MLA_HEADS, MLA_V, tk), lambda b, i, j: (0, 0, b * nk + j))],
        out_specs=pl.BlockSpec((tq, MLA_W), lambda b, i, j: (b * nq + i, 0)),
        out_shape=jax.ShapeDtypeStruct((t, MLA_W), F32),
        scratch_shapes=[pltpu.VMEM((MLA_HEADS, tq), F32), pltpu.VMEM((MLA_HEADS, tq), F32),
                        pltpu.VMEM((MLA_HEADS, MLA_V, tq), F32),
                        pltpu.VMEM((MLA_HEADS, tq), F32), pltpu.VMEM((MLA_HEADS, tq), F32),
                        pltpu.VMEM((MLA_HEADS, MLA_V, tq), F32)],
        compiler_params=_params("parallel", "parallel", "arbitrary"),
        name="mla_attn",
    )(q, k, v)


def _s5_param_kernel(lre_ref, lim_ref, ldt_ref, btre_ref, btim_ref, cre_ref, cim_ref, ek_ref, ew_ref, ee_ref,
                     kt_ref, wre_ref, wim_ref, ere_ref, eim_ref, are_ref, aim_ref):
    lam_re = lre_ref[...]
    lam_im = lim_ref[...]
    dt = jnp.exp(ldt_ref[...])

    def rows_of(x):
        return jnp.broadcast_to(x[:, None, :], (S5_CHUNK, S5_GROUP, S5_STATE)).reshape(S5_CW, S5_STATE)

    def power(e):
        mag = jnp.exp(lam_re * dt * e)
        ang = lam_im * dt * e
        return mag * jnp.cos(ang), mag * jnp.sin(ang)

    a_re, a_im = power(jnp.ones_like(lam_re))
    den = lam_re * lam_re + lam_im * lam_im
    nr = a_re - 1.0
    coef_re = (nr * lam_re + a_im * lam_im) / den
    coef_im = (a_im * lam_re - nr * lam_im) / den
    bt_re = btre_ref[...]
    bt_im = btim_ref[...]
    bb_re = coef_re * bt_re - coef_im * bt_im
    bb_im = coef_re * bt_im + coef_im * bt_re
    tile = lambda x: jnp.concatenate([x] * S5_CHUNK, axis=0)
    c_re = tile(cre_ref[...])
    c_im = tile(cim_ref[...])
    pk_re, pk_im = map(rows_of, power(ek_ref[...]))
    ck_re = c_re * pk_re - c_im * pk_im
    ck_im = c_re * pk_im + c_im * pk_re
    nt = (((1,), (1,)), ((), ()))
    kt_ref[...] = (lax.dot_general(bb_re, ck_re, nt, precision=HIGHEST, preferred_element_type=F32)
                   - lax.dot_general(bb_im, ck_im, nt, precision=HIGHEST, preferred_element_type=F32))
    pw_re, pw_im = map(rows_of, power(ew_ref[...]))
    bbt_re = tile(bb_re)
    bbt_im = tile(bb_im)
    wre_ref[...] = pw_re * bbt_re - pw_im * bbt_im
    wim_ref[...] = pw_re * bbt_im + pw_im * bbt_re
    pe_re, pe_im = map(rows_of, power(ee_ref[...]))
    ere_ref[...] = c_re * pe_re - c_im * pe_im
    eim_ref[...] = -(c_re * pe_im + c_im * pe_re)
    ac_re, ac_im = power(jnp.full_like(lam_re, float(S5_CHUNK)))
    are_ref[...] = ac_re
    aim_ref[...] = ac_im


def _s5_params(lam_re, lam_im, log_dt, b_re, b_im, c_re, c_im):
    n = 2 * S5_GROUPS
    p, c, cw = S5_STATE, S5_GROUP, S5_CW
    vec = lambda a: a.reshape(n, 1, p)
    ldt = jnp.broadcast_to(log_dt.reshape(n, 1, 1), (n, 1, p))
    bt = lambda a: a.transpose(0, 1, 3, 2).reshape(n, c, p)
    cc = lambda a: a.reshape(n, c, p)
    steps = np.arange(S5_CHUNK, dtype=np.float32)
    rows = lambda e: jnp.asarray(np.broadcast_to(e[:, :, None], (2, S5_CHUNK, p)).copy())
    e_k = rows(np.stack([steps, steps]))
    e_w = rows(np.stack([S5_CHUNK - 1 - steps, steps]))
    e_e = rows(np.stack([steps + 1, S5_CHUNK - steps]))
    per = lambda *shape: pl.BlockSpec((None,) + shape, lambda i: (i,) + (0,) * len(shape))
    per_dir = pl.BlockSpec((None, S5_CHUNK, p), lambda i: (i // S5_GROUPS, 0, 0))
    out = lambda *shape: jax.ShapeDtypeStruct((n,) + shape, F32)
    return pl.pallas_call(
        _s5_param_kernel,
        grid=(n,),
        in_specs=[per(1, p)] * 3 + [per(c, p)] * 4 + [per_dir] * 3,
        out_specs=[per(c, cw)] + [per(cw, p)] * 4 + [per(1, p)] * 2,
        out_shape=[out(c, cw)] + [out(cw, p)] * 4 + [out(1, p)] * 2,
        compiler_params=_params("parallel"),
        name="s5_params",
    )(vec(lam_re), vec(lam_im), ldt, bt(b_re), bt(b_im), cc(c_re), cc(c_im), e_k, e_w, e_e)


def _s5_toeplitz(kt):
    g, c, n = S5_GROUPS, S5_GROUP, S5_CHUNK
    k = kt.reshape(2, g, c, n, c)
    s_in = np.arange(n)[:, None]
    s_out = np.arange(n)[None, :]
    lag_f = np.clip(s_out - s_in, 0, n - 1)
    lag_b = np.clip(s_in - s_out, 0, n - 1)
    m_f = jnp.asarray((s_out >= s_in).astype(np.float32))[None, None, :, :, None]
    m_b = jnp.asarray((s_in >= s_out).astype(np.float32))[None, None, :, :, None]
    m = k[0][:, :, lag_f, :] * m_f + k[1][:, :, lag_b, :] * m_b
    return m.transpose(0, 2, 1, 3, 4).reshape(g, n * c, n * c)


def _s5_operators(lam_re, lam_im, log_dt, b_re, b_im, c_re, c_im):
    kt, w_re, w_im, e_re, e_im, a_re, a_im = _s5_params(lam_re, lam_im, log_dt, b_re, b_im, c_re, c_im)
    g = S5_GROUPS
    parts = lambda re, im: jnp.stack([re[:g], im[:g], re[g:], im[g:]], axis=1)
    return (_s5_toeplitz(kt).astype(BF16), parts(w_re, w_im).astype(BF16), parts(e_re, e_im).astype(BF16),
            parts(a_re[:, 0], a_im[:, 0]))


def _s5_chunk_kernel(nk, rows, u_ref, m_ref, w_ref, e_ref, a_ref, y_ref, wfr_s, wfi_s, wbr_s, wbi_s):
    u = u_ref[...].astype(BF16)
    y_ref[...] = jnp.dot(u, m_ref[...], preferred_element_type=F32)
    for i, dst in enumerate((wfr_s, wfi_s, wbr_s, wbi_s)):
        dst[...] = jnp.dot(u, w_ref[i], preferred_element_type=F32)
    a = [jnp.broadcast_to(a_ref[i:i + 1, :], (rows, S5_STATE)) for i in range(4)]

    def step(k, carry):
        fr, fi, br, bi = carry
        sf = pl.ds(k, rows, stride=nk)
        sb = pl.ds(nk - 1 - k, rows, stride=nk)
        wfr, wfi, wbr, wbi = wfr_s[sf, :], wfi_s[sf, :], wbr_s[sb, :], wbi_s[sb, :]
        wfr_s[sf, :] = fr
        wfi_s[sf, :] = fi
        wbr_s[sb, :] = br
        wbi_s[sb, :] = bi
        return (a[0] * fr - a[1] * fi + wfr, a[0] * fi + a[1] * fr + wfi,
                a[2] * br - a[3] * bi + wbr, a[2] * bi + a[3] * br + wbi)

    zero = jnp.zeros((rows, S5_STATE), F32)
    lax.fori_loop(0, nk, step, (zero, zero, zero, zero))
    carry_in = 0.0
    for i, src in enumerate((wfr_s, wfi_s, wbr_s, wbi_s)):
        carry_in = carry_in + _mm_nt(src[...], e_ref[i])
    y_ref[...] += carry_in


def _s5_scan(z_groups, bsz, seq, operators):
    m_mat, w_mat, e_mat, a_vec = operators
    nk = seq // S5_CHUNK
    r = bsz * nk
    grp = lambda *shape: pl.BlockSpec((None,) + shape, lambda i: (i,) + (0,) * len(shape))
    return pl.pallas_call(
        functools.partial(_s5_chunk_kernel, nk, bsz),
        grid=(S5_GROUPS,),
        in_specs=[grp(r, S5_CW), grp(S5_CW, S5_CW), grp(4, S5_CW, S5_STATE), grp(4, S5_CW, S5_STATE),
                  grp(4, S5_STATE)],
        out_specs=grp(r, S5_CW),
        out_shape=jax.ShapeDtypeStruct((S5_GROUPS, r, S5_CW), F32),
        scratch_shapes=[pltpu.VMEM((r, S5_STATE), F32)] * 4,
        compiler_params=_params("parallel"),
        name="s5_scan",
    )(z_groups, m_mat, w_mat, e_mat, a_vec)


def _s5_gated(y_ref, u_ref, d_ref, w_ref, b_ref, rows_s):
    chunks = y_ref.shape[1]
    for g in range(S5_GROUPS):
        y_g = y_ref[g] + d_ref[g:g + 1, :] * u_ref[g]
        for half in range(S5_CW // LANES):
            rows_s[half, pl.ds(g, chunks, stride=S5_GROUPS), :] = y_g[:, half * LANES:(half + 1) * LANES]
    y = jnp.concatenate([rows_s[half] for half in range(S5_CW // LANES)], axis=1)
    y = jax.nn.gelu(_to_chunk_rows(y), approximate=True)
    return y * _sigmoid(_mm(y, w_ref[...]) + b_ref[...])


def _merge_kernel(x_ref, of_ref, ob_ref, bonus_ref, rgate_ref, mla_ref, s5y_ref, s5u_ref, g_ref,
                  gng_ref, gnb_ref, wrw_ref, wmla_ref, s5d_ref, s5gw_ref, s5gb_ref, ws5_ref, wout_ref,
                  lng_ref, lnb_ref, o_ref, rows_s):
    w = RW_W
    hr = lax.broadcasted_iota(jnp.int32, (w, w), 0) // RW_HEAD
    hc = lax.broadcasted_iota(jnp.int32, (w, w), 1) // RW_HEAD
    head_mean = (hr == hc).astype(F32) * (1.0 / RW_HEAD)
    o = of_ref[...] + ob_ref[...]
    mean = _mm_split(o, head_mean, 2, 1)
    oc = o - mean
    var = _mm_split(oc * oc, head_mean, 2, 1)
    o = oc * lax.rsqrt(var + RW_GN_EPS) * gng_ref[...] + gnb_ref[...]
    y_rw = _mm((o + bonus_ref[...]) * rgate_ref[...], wrw_ref[...])
    y_mla = _mm(mla_ref[...], wmla_ref[...])
    y_s5 = _mm(_s5_gated(s5y_ref, s5u_ref, s5d_ref, s5gw_ref, s5gb_ref, rows_s), ws5_ref[...])
    d = D_MODEL
    merged = g_ref[:, 0:d] * y_rw + g_ref[:, d:2 * d] * y_mla + g_ref[:, 2 * d:3 * d] * y_s5
    o_ref[...] = _layer_norm(DN_ALPHA * x_ref[...] + _mm(merged, wout_ref[...]), lng_ref[...], lnb_ref[...])


def _merge(x, o_f, o_b, bonus, rgate, o_mla, s5_y, s5_u, gates, gn_g, gn_b, w_rw, w_mla, s5_d, s5_glu_w, s5_glu_b,
           w_s5, w_out, ln_g, ln_b, tm):
    t = x.shape[0]
    row = lambda n: pl.BlockSpec((tm, n), lambda i: (i, 0))
    grp = pl.BlockSpec((S5_GROUPS, tm // S5_CHUNK, S5_CW), lambda i: (0, i, 0))
    weights = [gn_g, gn_b, w_rw, w_mla, s5_d, s5_glu_w, s5_glu_b, w_s5, w_out, ln_g, ln_b]
    return pl.pallas_call(
        _merge_kernel,
        grid=(t // tm,),
        in_specs=[row(D_MODEL), row(RW_W), row(RW_W), row(RW_W), row(RW_W), row(MLA_W), grp, grp,
                  row(N_BRANCH * D_MODEL)] + [_full(wt) for wt in weights],
        out_specs=row(D_MODEL),
        out_shape=jax.ShapeDtypeStruct((t, D_MODEL), F32),
        scratch_shapes=[pltpu.VMEM((S5_CW // LANES, tm, LANES), F32)],
        compiler_params=_params("parallel"),
        name="merge_ln1",
    )(x, o_f, o_b, bonus, rgate, o_mla, s5_y, s5_u, gates, *weights)


def _mlp_kernel(nf, x_ref, w1_ref, w2_ref, lng_ref, lnb_ref, o_ref, acc_s):
    j = pl.program_id(1)

    @pl.when(j == 0)
    def _():
        acc_s[...] = jnp.zeros_like(acc_s)

    h = jnp.maximum(_mm(x_ref[...], w1_ref[...]), 0.0)
    acc_s[...] += _mm(h * h, w2_ref[...])

    @pl.when(j == nf - 1)
    def _():
        o_ref[...] = _layer_norm(DN_ALPHA * x_ref[...] + acc_s[...], lng_ref[...], lnb_ref[...])


def _mlp(x, w1, w2, ln_g, ln_b, tm, tf):
    t = x.shape[0]
    nf = D_FF // tf
    return pl.pallas_call(
        functools.partial(_mlp_kernel, nf),
        grid=(t // tm, nf),
        in_specs=[pl.BlockSpec((tm, D_MODEL), lambda i, j: (i, 0)),
                  pl.BlockSpec((D_MODEL, tf), lambda i, j: (0, j)),
                  pl.BlockSpec((tf, D_MODEL), lambda i, j: (j, 0)),
                  _full(ln_g), _full(ln_b)],
        out_specs=pl.BlockSpec((tm, D_MODEL), lambda i, j: (i, 0)),
        out_shape=jax.ShapeDtypeStruct((t, D_MODEL), F32),
        scratch_shapes=[pltpu.VMEM((tm, D_MODEL), F32)],
        compiler_params=_params("parallel", "arbitrary"),
        name="mlp_ln2",
    )(x, w1, w2, ln_g, ln_b)


def _tile(n, pref):
    t = min(n, pref)
    assert n % t == 0, (n, pref)
    return t


def _prep_layer(w_in, rw_mu, rw_w0, rw_w2, rw_a0, rw_a2, rw_g2, rw_k_k, rw_k_a, rw_r_k, rw_gn_g, rw_gn_b, rw_proj,
                mla_q_norm, mla_w_uq, mla_kv_norm, mla_w_ukv, mla_proj,
                s5_lam_re, s5_lam_im, s5_log_dt, s5_b_re, s5_b_im, s5_c_re, s5_c_im, s5_d, s5_glu_w, s5_glu_b,
                s5_proj, w_out, ln1_g, ln1_b, mlp_w1, mlp_w2, ln2_g, ln2_b):
    row = lambda a: a.reshape(1, -1)
    p = {}
    p["w_rw"] = w_in[:, :OFF_MLA].astype(BF16)
    w_mla = w_in[:, OFF_MLA:OFF_S5]
    zeros = lambda n: jnp.zeros((D_MODEL, n), F32)
    p["w_mla"] = jnp.concatenate([w_mla[:, :MLA_Q_LORA + MLA_KV_LORA], zeros(MLA_NOPE),
                                  w_mla[:, MLA_Q_LORA + MLA_KV_LORA:],
                                  zeros(MLA_QK_PAD - MLA_NOPE - MLA_ROPE)], axis=1).astype(BF16)
    p["w_s5"] = w_in[:, OFF_S5:OFF_GATE].astype(BF16)
    p["w_gate"] = w_in[:, OFF_GATE:].astype(BF16)
    p["rw"] = [row(rw_mu)]
    p["rw_dir"] = [rw_w0[:, None, :], rw_w2, rw_a0[:, None, :], rw_a2]
    p["rw_shared"] = [rw_g2, row(rw_k_k), row(rw_k_a), row(rw_r_k)]
    p["rw_gn"] = [row(rw_gn_g), row(rw_gn_b)]
    p["rw_proj"] = rw_proj.astype(BF16)
    uq = mla_w_uq.reshape(MLA_Q_LORA, MLA_HEADS, MLA_NOPE + MLA_ROPE)
    uq = jnp.pad(uq, ((0, 0), (0, 0), (0, MLA_QK_PAD - MLA_NOPE - MLA_ROPE)))
    p["w_uq"] = uq.transpose(1, 2, 0).astype(BF16)
    ukv = mla_w_ukv.reshape(MLA_KV_LORA, MLA_HEADS, MLA_NOPE + MLA_V)
    uk = jnp.pad(ukv[:, :, :MLA_NOPE], ((0, 0), (0, 0), (0, MLA_QK_PAD - MLA_NOPE)))
    p["w_uk"] = uk.transpose(1, 0, 2).astype(BF16)
    p["w_uv"] = ukv[:, :, MLA_NOPE:].transpose(1, 2, 0).astype(BF16)
    p["q_norm"] = row(mla_q_norm)
    p["kv_norm"] = row(mla_kv_norm)
    p["mla_proj"] = mla_proj.astype(BF16)
    p["s5_ops"] = _s5_operators(s5_lam_re, s5_lam_im, s5_log_dt, s5_b_re, s5_b_im, s5_c_re, s5_c_im)
    d_rows = jnp.broadcast_to(s5_d.reshape(S5_GROUPS, 1, S5_GROUP), (S5_GROUPS, S5_CHUNK, S5_GROUP))
    p["s5_post"] = [d_rows.reshape(S5_GROUPS, S5_CW), s5_glu_w.astype(BF16), row(s5_glu_b)]
    p["s5_proj"] = s5_proj.astype(BF16)
    p["w_out"] = w_out.astype(BF16)
    p["ln1"] = [row(ln1_g), row(ln1_b)]
    p["w1"] = mlp_w1.astype(BF16)
    p["w2"] = mlp_w2.astype(BF16)
    p["ln2"] = [row(ln2_g), row(ln2_b)]
    return p


def _layer(x, p, bsz, seq, rope):
    t = bsz * seq
    tm = _tile(t, 512)
    z_rw, z_mla, z_s5, gates = _inproj(x, p["w_rw"], p["w_mla"], p["w_s5"], p["w_gate"], _tile(t, 256))

    tb = _tile(seq, 512)
    z3 = z_rw.reshape(bsz, seq, RW_IN)
    o_f, bonus, rgate, o_b = _rwkv_scan(z3, *p["rw"], *p["rw_dir"], *p["rw_shared"], tb=tb)
    flat = lambda a: a.reshape(t, RW_W)

    q, k, v = _mla_qkv(z_mla, rope, p["q_norm"], p["w_uq"], p["kv_norm"], p["w_uk"], p["w_uv"],
                       seq, _tile(seq, 512))
    o_mla = _mla_attn(q, k, v, bsz, seq, _tile(seq, 1024), _tile(seq, 1024))

    s5_y = _s5_scan(z_s5, bsz, seq, p["s5_ops"])

    x1 = _merge(x, flat(o_f), flat(o_b), flat(bonus), flat(rgate), o_mla, s5_y, z_s5, gates,
                *p["rw_gn"], p["rw_proj"], p["mla_proj"], *p["s5_post"], p["s5_proj"], p["w_out"], *p["ln1"], tm)
    return _mlp(x1, p["w1"], p["w2"], *p["ln2"], _tile(t, 1024), 1024)


def _trunk(x, layers):
    bsz, seq, _ = x.shape
    assert seq % RW_CHUNK == 0 and seq % S5_CHUNK == 0
    rope = _rope_tables(seq)
    h = x.reshape(bsz * seq, D_MODEL)
    for p in layers:
        h = _layer(h, p, bsz, seq, rope)
    return h.reshape(bsz, seq, D_MODEL)


def kernel(x_prompt, x_sample, w_in, rw_mu, rw_w0, rw_w2, rw_a0, rw_a2, rw_g2, rw_k_k, rw_k_a, rw_r_k, rw_gn_g, rw_gn_b, rw_proj, mla_q_norm, mla_w_uq, mla_kv_norm, mla_w_ukv, mla_proj, s5_lam_re, s5_lam_im, s5_log_dt, s5_b_re, s5_b_im, s5_c_re, s5_c_im, s5_d, s5_glu_w, s5_glu_b, s5_proj, w_out, ln1_g, ln1_b, mlp_w1, mlp_w2, ln2_g, ln2_b):
    weights = (w_in, rw_mu, rw_w0, rw_w2, rw_a0, rw_a2, rw_g2, rw_k_k, rw_k_a, rw_r_k, rw_gn_g, rw_gn_b, rw_proj,
               mla_q_norm, mla_w_uq, mla_kv_norm, mla_w_ukv, mla_proj,
               s5_lam_re, s5_lam_im, s5_log_dt, s5_b_re, s5_b_im, s5_c_re, s5_c_im, s5_d, s5_glu_w, s5_glu_b,
               s5_proj, w_out, ln1_g, ln1_b, mlp_w1, mlp_w2, ln2_g, ln2_b)
    layers = [_prep_layer(*[wt[l] for wt in weights]) for l in range(w_in.shape[0])]
    return _trunk(x_prompt, layers), _trunk(x_sample, layers)
```

```python
import functools
import math

import numpy as np
import jax
import jax.numpy as jnp
from jax import lax
from jax.experimental import pallas as pl
from jax.experimental.pallas import tpu as pltpu

F32 = jnp.float32
BF16 = jnp.bfloat16
HIGHEST = lax.Precision.HIGHEST

D_MODEL = 1024
DEPTH = 2
RW_HEADS = 4
RW_HEAD = 64
RW_W = RW_HEADS * RW_HEAD
RW_LORA_W = 32
RW_LORA_A = 32
RW_LORA_G = 64
RW_GN_EPS = 64e-5
RW_IN = 3 * RW_W + RW_LORA_W + RW_LORA_A + RW_LORA_G
MLA_HEADS = 8
MLA_NOPE = 64
MLA_ROPE = 32
MLA_V = 64
MLA_Q_LORA = 256
MLA_KV_LORA = 128
MLA_W = MLA_HEADS * MLA_V
MLA_QK_PAD = 128
MLA_IN_PAD = MLA_Q_LORA + MLA_KV_LORA + MLA_QK_PAD
MLA_SCALE = (MLA_NOPE + MLA_ROPE) ** -0.5
LOG2_E = math.log2(math.e)
MLA_SHIFT_SLACK = 64.0
ROPE_THETA = 10000.0
RMS_EPS = 1e-6
S5_W = 256
S5_GROUP = 16
S5_GROUPS = S5_W // S5_GROUP
S5_STATE = 64
S5_CHUNK = 16
S5_CW = S5_CHUNK * S5_GROUP
LANES = 128
D_FF = 4 * D_MODEL
LN_EPS = 1e-5
N_BRANCH = 3
DN_ALPHA = (2 * DEPTH) ** 0.25
OFF_MLA = RW_IN
OFF_S5 = OFF_MLA + MLA_Q_LORA + MLA_KV_LORA + MLA_ROPE
OFF_GATE = OFF_S5 + S5_W
RW_CHUNK = 64
VMEM_LIMIT = 56 * 1024 * 1024


def _params(*sem):
    return pltpu.CompilerParams(dimension_semantics=sem, vmem_limit_bytes=VMEM_LIMIT)


def _mm(a, b):
    return jnp.dot(a.astype(BF16), b.astype(BF16), preferred_element_type=F32)


def _mm_nt(a, b):
    return lax.dot_general(a.astype(BF16), b.astype(BF16), (((1,), (1,)), ((), ())),
                           preferred_element_type=F32)


def _mm_tn(a, b):
    return lax.dot_general(a.astype(BF16), b.astype(BF16), (((0,), (0,)), ((), ())),
                           preferred_element_type=F32)


def _mm_f32(a, b):
    return jnp.dot(a, b, preferred_element_type=F32, precision=HIGHEST)


def _bf16_terms(x, n):
    terms = []
    for _ in range(n):
        t = x.astype(BF16)
        terms.append(t)
        x = x - t.astype(F32)
    return terms


def _mm_split(a, b, a_terms, b_terms):
    at = _bf16_terms(a, a_terms)
    bt = _bf16_terms(b, b_terms)
    out = None
    for i, x in enumerate(at):
        for j, y in enumerate(bt):
            if i + j < max(a_terms, b_terms):
                d = jnp.dot(x, y, preferred_element_type=F32)
                out = d if out is None else out + d
    return out


def _sigmoid(x):
    return 1.0 / (1.0 + jnp.exp(-x))


def _full(a):
    nd = a.ndim
    return pl.BlockSpec(a.shape, lambda *_: (0,) * nd)


def _layer_norm(x, g, b):
    mu = jnp.mean(x, -1, keepdims=True)
    xc = x - mu
    var = jnp.mean(xc * xc, -1, keepdims=True)
    return xc * lax.rsqrt(var + LN_EPS) * g + b


def _to_chunk_rows(x):
    n = S5_CHUNK
    assert n == S5_GROUPS and x.shape[1] == S5_CW and x.shape[0] % n == 0
    x = x.reshape(x.shape[0] // n, n, S5_CW)
    row = lax.broadcasted_iota(jnp.int32, (1, n, S5_CW), 1)
    pkt = lax.broadcasted_iota(jnp.int32, (1, n, S5_CW), 2) // S5_GROUP
    d = 1
    while d < n:
        row_bit = (row // d) % 2
        pkt_bit = (pkt // d) % 2
        up = pltpu.roll(pltpu.roll(x, n - d, axis=1), S5_GROUP * d, axis=2)
        down = pltpu.roll(pltpu.roll(x, d, axis=1), S5_CW - S5_GROUP * d, axis=2)
        x = jnp.where(row_bit == pkt_bit, x, jnp.where(row_bit == 0, up, down))
        d *= 2
    return x.reshape(x.shape[0] * n, S5_CW)


def _inproj_kernel(x_ref, wrw_ref, wmla_ref, ws5_ref, wg_ref, zrw_ref, zmla_ref, zs5_ref, g_ref, rows_s):
    xb = x_ref[...].astype(BF16)
    zrw_ref[...] = jnp.dot(xb, wrw_ref[...], preferred_element_type=F32)
    zmla_ref[...] = jnp.dot(xb, wmla_ref[...], preferred_element_type=F32)
    z_rows = _to_chunk_rows(jnp.dot(xb, ws5_ref[...], preferred_element_type=F32))
    chunks = z_rows.shape[0] // S5_GROUPS
    for half in range(S5_CW // LANES):
        lanes = slice(half * LANES, (half + 1) * LANES)
        rows_s[half] = z_rows[:, lanes]
        for g in range(S5_GROUPS):
            zs5_ref[g, :, lanes] = rows_s[half, pl.ds(g, chunks, stride=S5_GROUPS), :]
    g_ref[...] = _sigmoid(jnp.dot(xb, wg_ref[...], preferred_element_type=F32)).astype(g_ref.dtype)


def _inproj(x, w_rw, w_mla, w_s5, w_g, tm):
    t = x.shape[0]
    row = lambda n: pl.BlockSpec((tm, n), lambda i: (i, 0))
    return pl.pallas_call(
        _inproj_kernel,
        grid=(t // tm,),
        in_specs=[row(D_MODEL), _full(w_rw), _full(w_mla), _full(w_s5), _full(w_g)],
        out_specs=[row(RW_IN), row(MLA_IN_PAD),
                   pl.BlockSpec((S5_GROUPS, tm // S5_CHUNK, S5_CW), lambda i: (0, i, 0)), row(N_BRANCH * D_MODEL)],
        out_shape=[jax.ShapeDtypeStruct((t, RW_IN), F32), jax.ShapeDtypeStruct((t, MLA_IN_PAD), F32),
                   jax.ShapeDtypeStruct((S5_GROUPS, t // S5_CHUNK, S5_CW), F32),
                   jax.ShapeDtypeStruct((t, N_BRANCH * D_MODEL), BF16)],
        scratch_shapes=[pltpu.VMEM((S5_CW // LANES, tm, LANES), F32)],
        compiler_params=_params("parallel"),
        name="inproj",
    )(x, w_rw, w_mla, w_s5, w_g)


def _rwkv_kernel(nblk, tb, nb,
                 zf_ref, zfp_ref, zfn_ref, zb_ref, zbp_ref, zbn_ref,
                 mu_ref, w0_ref, w2_ref, a0_ref, a2_ref, g2_ref, kk_ref, ka_ref, rk_ref,
                 of_ref, bonus_ref, gate_ref, ob_ref, r_s, v_s, kn_s, lw_s, b_s, kd_s, state):
    step = pl.program_id(1)
    c = RW_CHUNK
    w = RW_W
    hr = lax.broadcasted_iota(jnp.int32, (w, w), 0) // RW_HEAD
    hc = lax.broadcasted_iota(jnp.int32, (w, w), 1) // RW_HEAD
    same_head = hr == hc
    head_ones = same_head.astype(F32)

    def prepare(d, bi, z_ref, zp_ref, zn_ref, blk):
        z = z_ref[bi]
        prev_row = jnp.where(blk == 0, 0.0, zp_ref[bi, 7:8, :])
        next_row = jnp.where(blk == nblk - 1, 0.0, zn_ref[bi, 0:1, :])
        tile_row = lax.broadcasted_iota(jnp.int32, (8, RW_IN), 0)
        z_prev = pltpu.roll(z, 1, axis=0)
        z_prev = jnp.concatenate([jnp.where(tile_row == 0, prev_row, z_prev[0:8]), z_prev[8:]], axis=0)
        z_next = pltpu.roll(z, tb - 1, axis=0)
        z_next = jnp.concatenate([z_next[:tb - 8], jnp.where(tile_row == 7, next_row, z_next[tb - 8:])], axis=0)
        mu = mu_ref[...]
        z = (1.0 - mu) * z + (0.5 * mu) * (z_prev + z_next)
        r = z[:, 0:w]
        k = z[:, w:2 * w]
        v = z[:, 2 * w:3 * w]
        xw = z[:, 3 * w:3 * w + RW_LORA_W]
        xa = z[:, 3 * w + RW_LORA_W:3 * w + RW_LORA_W + RW_LORA_A]
        xg = z[:, 3 * w + RW_LORA_W + RW_LORA_A:]
        kk = k * kk_ref[...]
        kk_ss = _mm_split(kk * kk, head_ones, 2, 1)
        kk = kk * lax.rsqrt(jnp.maximum(kk_ss, 1e-12))
        y = w0_ref[d] + _mm_split(jnp.tanh(xw), w2_ref[d], 2, 2)
        lw = -math.exp(-0.5) * _sigmoid(y)
        a = _sigmoid(a0_ref[d] + _mm(xa, a2_ref[d]))
        r_s[d, bi] = r
        v_s[d, bi] = v
        kn_s[d, bi] = kk
        lw_s[d, bi] = lw
        b_s[d, bi] = kk * a
        kd_s[d, bi] = k * (1.0 + (a - 1.0) * ka_ref[...])
        if d == 0:
            rk = _mm_split(r * k * rk_ref[...], head_ones, 2, 1)
            bonus_ref[bi] = rk * v
            gate_ref[bi] = _mm(_sigmoid(xg), g2_ref[...])

    for bi in range(nb):
        prepare(0, bi, zf_ref, zfp_ref, zfn_ref, step)
        prepare(1, bi, zb_ref, zbp_ref, zbn_ref, nblk - 1 - step)

    @pl.when(step == 0)
    def _():
        state[...] = jnp.zeros_like(state)

    ti = lax.broadcasted_iota(jnp.int32, (c, c), 0)
    si = lax.broadcasted_iota(jnp.int32, (c, c), 1)
    tw = lax.broadcasted_iota(jnp.int32, (c, w), 0)
    sw = lax.broadcasted_iota(jnp.int32, (c, w), 1) % RW_HEAD
    eye = (tw == sw).astype(F32)

    def stack(x):
        return jnp.where(same_head, jnp.concatenate([x] * RW_HEADS, axis=0), 0.0).astype(BF16)

    nchunk = tb // c

    def chunk(reverse, bi, cpos):
        d = 1 if reverse else 0
        o_ref = ob_ref if reverse else of_ref
        cum_mat = ((si >= ti) if reverse else (si <= ti)).astype(F32)
        strict = (tw < sw) if reverse else (tw > sw)
        incl = (tw <= sw) if reverse else (tw >= sw)
        sl = pl.ds(pl.multiple_of(cpos * c, c), c)
        lwc = lw_s[d, bi, sl, :]
        l_in = _mm_split(cum_mat, lwc, 1, 3)
        l_tot = jnp.sum(lwc, axis=0, keepdims=True)
        e_in = jnp.exp(l_in)
        e_neg = jnp.exp(-l_in)
        e_tot = jnp.exp(l_tot)
        kap = kn_s[d, bi, sl, :] * jnp.exp(l_in - lwc)
        bt = b_s[d, bi, sl, :] * e_neg
        kt = kd_s[d, bi, sl, :] * e_neg
        rt = (r_s[d, bi, sl, :] * e_in).astype(BF16)
        v = v_s[d, bi, sl, :]
        kap_w, bt_w, kt_w, v_w = stack(kap), stack(bt), stack(kt), stack(v)
        kap = kap.astype(BF16)
        end_w = jnp.concatenate([bt * e_tot, kt * e_tot], axis=0).astype(BF16)
        s_bd = state[d, bi]
        s_w = s_bd.astype(BF16)
        yield
        kap_rt = jnp.concatenate([kap, rt], axis=0)
        am_b = _mm_nt(kap_rt, bt_w)
        a_b = jnp.where(strict, am_b[0:c], 0.0)
        m_b = jnp.where(incl, am_b[c:2 * c], 0.0)
        yield
        am_k = _mm_nt(kap_rt, kt_w)
        a_k = jnp.where(strict, am_k[0:c], 0.0)
        m_k = jnp.where(incl, am_k[c:2 * c], 0.0)
        yield
        assert c == 2 ** int(math.log2(c)) and int(math.log2(c)) % 2 == 0
        powers = [-a_b]
        pairs = []
        akv = _mm(a_k, v_w)
        for level in range(1, int(math.log2(c))):
            sq_w = stack(powers[-1])
            powers.append(_mm(powers[-1], sq_w))
            if level % 2 == 0:
                lo = eye + powers[level - 2]
                pairs.append(lo + _mm(lo, sq_w))
            yield
        lo = eye + powers[-2]
        pairs.append(lo + _mm(lo, stack(powers[-1])))
        tinv = pairs[0]
        for pr in pairs[1:-1]:
            tinv = _mm(tinv, stack(pr))
        yield
        tinv = _mm(tinv, stack(pairs[-1]))
        yield
        pm = _mm(tinv, kap_w)
        qm = _mm(tinv, stack(akv))
        yield
        u = -(_mm_nt(pm, s_w) + qm)
        yield
        o = _mm_nt(rt, s_w) + _mm(m_b, stack(u)) + _mm(m_k, v_w)
        upd = _mm_tn(jnp.concatenate([u, v], axis=0), end_w)
        yield
        o_ref[bi, sl, :] = o
        state[d, bi] = s_bd * e_tot + jnp.where(same_head, upd, 0.0)

    def all_scans(ci, carry):
        scans = [chunk(False, bi, ci) for bi in range(nb)] + [chunk(True, bi, nchunk - 1 - ci) for bi in range(nb)]
        while scans:
            scans = [g for g in scans if next(g, StopIteration) is not StopIteration]
        return carry

    lax.fori_loop(0, nchunk, all_scans, 0)


def _rwkv_scan(z, mu, w0, w2, a0, a2, g2, k_k, k_a, r_k, tb):
    bsz, seq, _ = z.shape
    nblk = seq // tb
    nb = 2 if bsz % 2 == 0 else 1
    z8 = z.reshape(bsz, seq // 8, 8, RW_IN)
    t8 = tb // 8

    def specs(pos):
        z_spec = pl.BlockSpec((nb, tb, RW_IN), lambda b, i: (b, pos(i), 0))
        zp_spec = pl.BlockSpec((nb, None, 8, RW_IN), lambda b, i: (b, jnp.maximum(pos(i) * t8 - 1, 0), 0, 0))
        zn_spec = pl.BlockSpec((nb, None, 8, RW_IN),
                               lambda b, i: (b, jnp.minimum((pos(i) + 1) * t8, seq // 8 - 1), 0, 0))
        o_spec = pl.BlockSpec((nb, tb, RW_W), lambda b, i: (b, pos(i), 0))
        return [z_spec, zp_spec, zn_spec], o_spec

    in_f, o_f = specs(lambda i: i)
    in_b, o_b = specs(lambda i: nblk - 1 - i)
    o_shape = jax.ShapeDtypeStruct((bsz, seq, RW_W), F32)
    weights = [mu, w0, w2, a0, a2, g2, k_k, k_a, r_k]
    return pl.pallas_call(
        functools.partial(_rwkv_kernel, nblk, tb, nb),
        grid=(bsz // nb, nblk),
        in_specs=in_f + in_b + [_full(wt) for wt in weights],
        out_specs=[o_f, o_f, o_f, o_b],
        out_shape=[o_shape] * 4,
        scratch_shapes=[pltpu.VMEM((2, nb, tb, RW_W), F32)] * 6 + [pltpu.VMEM((2, nb, RW_W, RW_W), F32)],
        compiler_params=_params("parallel", "arbitrary"),
        name="rwkv_scan",
    )(z, z8, z8, z, z8, z8, *weights)


def _rope_tables(seq):
    half = MLA_ROPE // 2
    inv = (ROPE_THETA ** (-np.arange(half, dtype=np.float32) / half)).astype(np.float32)
    ang = np.arange(seq, dtype=np.float32)[:, None] * inv[None, :]
    cos = np.cos(ang).astype(np.float32)
    sin = np.sin(ang).astype(np.float32)
    ct = np.zeros((seq, MLA_QK_PAD), np.float32)
    st = np.zeros((seq, MLA_QK_PAD), np.float32)
    ct[:, :MLA_NOPE] = 1.0
    ct[:, MLA_NOPE:MLA_NOPE + half] = cos
    ct[:, MLA_NOPE + half:MLA_NOPE + 2 * half] = cos
    st[:, MLA_NOPE:MLA_NOPE + half] = -sin
    st[:, MLA_NOPE + half:MLA_NOPE + 2 * half] = sin
    return jnp.asarray(ct), jnp.asarray(st), jnp.asarray(ct.T), jnp.asarray(st.T)


def _rope(x, cos_t, sin_t):
    half = MLA_ROPE // 2
    lane = lax.broadcasted_iota(jnp.int32, x.shape, 1)
    swapped = jnp.where(lane < MLA_NOPE + half,
                        pltpu.roll(x, MLA_QK_PAD - half, axis=1), pltpu.roll(x, half, axis=1))
    return x * cos_t + swapped * sin_t


def _rope_rows(x, cos_t, sin_t):
    half = MLA_ROPE // 2
    a, b = MLA_NOPE, MLA_NOPE + half
    swapped = jnp.concatenate([x[0:a], x[b:b + half], x[a:b], x[b + half:]], axis=0)
    return x * cos_t + swapped * sin_t


def _mla_qkv_kernel(z_ref, cos_ref, sin_ref, cosr_ref, sinr_ref, qn_ref, wuq_ref, kvn_ref, wuk_ref, wuv_ref,
                    q_ref, k_ref, v_ref):
    z = z_ref[...]
    cos_t = cos_ref[...]
    sin_t = sin_ref[...]
    cos_r = cosr_ref[...]
    sin_r = sinr_ref[...]
    c_q = z[:, :MLA_Q_LORA]
    c_kv = z[:, MLA_Q_LORA:MLA_Q_LORA + MLA_KV_LORA]
    k_rope = _rope(z[:, MLA_Q_LORA + MLA_KV_LORA:], cos_t, sin_t)
    c_q = (c_q * lax.rsqrt(jnp.mean(c_q * c_q, -1, keepdims=True) + RMS_EPS) * qn_ref[...]).astype(BF16)
    c_kv = (c_kv * lax.rsqrt(jnp.mean(c_kv * c_kv, -1, keepdims=True) + RMS_EPS) * kvn_ref[...]).astype(BF16)
    nt = (((1,), (1,)), ((), ()))
    q_all = lax.dot_general(wuq_ref[...].reshape(MLA_HEADS * MLA_QK_PAD, MLA_Q_LORA), c_q, nt,
                            preferred_element_type=F32)
    v_all = lax.dot_general(wuv_ref[...].reshape(MLA_HEADS * MLA_V, MLA_KV_LORA), c_kv, nt,
                            preferred_element_type=F32)
    for h in range(MLA_HEADS):
        q_t = q_all[h * MLA_QK_PAD:(h + 1) * MLA_QK_PAD]
        q_ref[h] = (_rope_rows(q_t, cos_r, sin_r) * (MLA_SCALE * LOG2_E)).astype(BF16)
        kh = jnp.dot(c_kv, wuk_ref[h], preferred_element_type=F32)
        k_ref[h] = (kh + k_rope).astype(BF16)
        v_ref[h] = v_all[h * MLA_V:(h + 1) * MLA_V].astype(BF16)


def _mla_qkv(z, rope, q_norm, w_uq, kv_norm, w_uk, w_uv, seq, tm):
    t = z.shape[0]
    nseq = seq // tm
    tab = pl.BlockSpec((tm, MLA_QK_PAD), lambda i: (i % nseq, 0))
    tab_r = pl.BlockSpec((MLA_QK_PAD, tm), lambda i: (0, i % nseq))
    cols = lambda n: pl.BlockSpec((MLA_HEADS, n, tm), lambda i: (0, 0, i))
    return pl.pallas_call(
        _mla_qkv_kernel,
        grid=(t // tm,),
        in_specs=[pl.BlockSpec((tm, MLA_IN_PAD), lambda i: (i, 0)), tab, tab, tab_r, tab_r,
                  _full(q_norm), _full(w_uq), _full(kv_norm), _full(w_uk), _full(w_uv)],
        out_specs=[cols(MLA_QK_PAD), pl.BlockSpec((MLA_HEADS, tm, MLA_QK_PAD), lambda i: (0, i, 0)),
                   cols(MLA_V)],
        out_shape=[jax.ShapeDtypeStruct((MLA_HEADS, MLA_QK_PAD, t), BF16),
                   jax.ShapeDtypeStruct((MLA_HEADS, t, MLA_QK_PAD), BF16),
                   jax.ShapeDtypeStruct((MLA_HEADS, MLA_V, t), BF16)],
        compiler_params=_params("parallel"),
        name="mla_qkv",
    )(z, *rope, q_norm, w_uq, kv_norm, w_uk, w_uv)


def _mla_attn_kernel(nk, q_ref, k_ref, v_ref, o_ref, m_s, l_s, acc_s, bm_s, bl_s, pv_s):
    j = pl.program_id(2)

    def scores(h):
        return jnp.dot(k_ref[h], q_ref[h], preferred_element_type=F32)

    @pl.when(j == 0)
    def _():
        m_s[...] = jnp.full(m_s.shape, -jnp.inf, F32)
        l_s[...] = jnp.zeros_like(l_s)
        acc_s[...] = jnp.zeros_like(acc_s)
        for h in range(MLA_HEADS):
            bm_s[h:h + 1, :] = jnp.max(scores(h), axis=0, keepdims=True)

    @pl.when(j > 0)
    def _():
        s_next = scores(0)
        for h in range(MLA_HEADS):
            s = s_next
            if h + 1 < MLA_HEADS:
                s_next = scores(h + 1)
            bm_s[h:h + 1, :] = jnp.max(s, axis=0, keepdims=True)
            p = jnp.exp2(s - m_s[h:h + 1, :])
            bl_s[h:h + 1, :] = jnp.sum(p, axis=0, keepdims=True)
            pv_s[h] = jnp.dot(v_ref[h], p.astype(BF16), preferred_element_type=F32)

    stale_shift_ok = jnp.max(bm_s[...] - m_s[...]) < MLA_SHIFT_SLACK

    @pl.when(stale_shift_ok)
    def _():
        for h in range(MLA_HEADS):
            m_prev = m_s[h:h + 1, :]
            m_new = jnp.maximum(m_prev, bm_s[h:h + 1, :])
            alpha = jnp.exp2(m_prev - m_new)
            acc_s[h] = (acc_s[h] + pv_s[h]) * alpha
            l_s[h:h + 1, :] = (l_s[h:h + 1, :] + bl_s[h:h + 1, :]) * alpha
            m_s[h:h + 1, :] = m_new

    @pl.when(jnp.logical_not(stale_shift_ok))
    def _():
        for h in range(MLA_HEADS):
            s = scores(h)
            m_prev = m_s[h:h + 1, :]
            m_new = jnp.maximum(m_prev, bm_s[h:h + 1, :])
            alpha = jnp.exp2(m_prev - m_new)
            p = jnp.exp2(s - m_new)
            l_s[h:h + 1, :] = alpha * l_s[h:h + 1, :] + jnp.sum(p, axis=0, keepdims=True)
            acc_s[h] = alpha * acc_s[h] + jnp.dot(v_ref[h], p.astype(BF16), preferred_element_type=F32)
            m_s[h:h + 1, :] = m_new

    @pl.when(j == nk - 1)
    def _():
        heads = [acc_s[h] / l_s[h:h + 1, :] for h in range(MLA_HEADS)]
        o_ref[...] = jnp.concatenate(heads, axis=0).T


def _mla_attn(q, k, v, bsz, seq, tq, tk):
    nq = seq // tq
    nk = seq // tk
    t = bsz * seq
    return pl.pallas_call(
        functools.partial(_mla_attn_kernel, nk),
        grid=(bsz, nq, nk),
        in_specs=[pl.BlockSpec((MLA_HEADS, MLA_QK_PAD, tq), lambda b, i, j: (0, 0, b * nq + i)),
                  pl.BlockSpec((MLA_HEADS, tk, MLA_QK_PAD), lambda b, i, j: (0, b * nk + j, 0)),
                  pl.BlockSpec((MLA_HEADS, MLA_V, tk), lambda b, i, j: (0, 0, b * nk + j))],
        out_specs=pl.BlockSpec((tq, MLA_W), lambda b, i, j: (b * nq + i, 0)),
        out_shape=jax.ShapeDtypeStruct((t, MLA_W), F32),
        scratch_shapes=[pltpu.VMEM((MLA_HEADS, tq), F32), pltpu.VMEM((MLA_HEADS, tq), F32),
                        pltpu.VMEM((MLA_HEADS, MLA_V, tq), F32),
                        pltpu.VMEM((MLA_HEADS, tq), F32), pltpu.VMEM((MLA_HEADS, tq), F32),
                        pltpu.VMEM((MLA_HEADS, MLA_V, tq), F32)],
        compiler_params=_params("parallel", "parallel", "arbitrary"),
        name="mla_attn",
    )(q, k, v)


def _s5_param_kernel(lre_ref, lim_ref, ldt_ref, btre_ref, btim_ref, cre_ref, cim_ref, ek_ref, ew_ref, ee_ref,
                     kt_ref, wre_ref, wim_ref, ere_ref, eim_ref, are_ref, aim_ref):
    lam_re = lre_ref[...]
    lam_im = lim_ref[...]
    dt = jnp.exp(ldt_ref[...])

    def rows_of(x):
        return jnp.broadcast_to(x[:, None, :], (S5_CHUNK, S5_GROUP, S5_STATE)).reshape(S5_CW, S5_STATE)

    def power(e):
        mag = jnp.exp(lam_re * dt * e)
        ang = lam_im * dt * e
        return mag * jnp.cos(ang), mag * jnp.sin(ang)

    a_re, a_im = power(jnp.ones_like(lam_re))
    den = lam_re * lam_re + lam_im * lam_im
    nr = a_re - 1.0
    coef_re = (nr * lam_re + a_im * lam_im) / den
    coef_im = (a_im * lam_re - nr * lam_im) / den
    bt_re = btre_ref[...]
    bt_im = btim_ref[...]
    bb_re = coef_re * bt_re - coef_im * bt_im
    bb_im = coef_re * bt_im + coef_im * bt_re
    tile = lambda x: jnp.concatenate([x] * S5_CHUNK, axis=0)
    c_re = tile(cre_ref[...])
    c_im = tile(cim_ref[...])
    pk_re, pk_im = map(rows_of, power(ek_ref[...]))
    ck_re = c_re * pk_re - c_im * pk_im
    ck_im = c_re * pk_im + c_im * pk_re
    nt = (((1,), (1,)), ((), ()))
    kt_ref[...] = (lax.dot_general(bb_re, ck_re, nt, precision=HIGHEST, preferred_element_type=F32)
                   - lax.dot_general(bb_im, ck_im, nt, precision=HIGHEST, preferred_element_type=F32))
    pw_re, pw_im = map(rows_of, power(ew_ref[...]))
    bbt_re = tile(bb_re)
    bbt_im = tile(bb_im)
    wre_ref[...] = pw_re * bbt_re - pw_im * bbt_im
    wim_ref[...] = pw_re * bbt_im + pw_im * bbt_re
    pe_re, pe_im = map(rows_of, power(ee_ref[...]))
    ere_ref[...] = c_re * pe_re - c_im * pe_im
    eim_ref[...] = -(c_re * pe_im + c_im * pe_re)
    ac_re, ac_im = power(jnp.full_like(lam_re, float(S5_CHUNK)))
    are_ref[...] = ac_re
    aim_ref[...] = ac_im


def _s5_params(lam_re, lam_im, log_dt, b_re, b_im, c_re, c_im):
    n = 2 * S5_GROUPS
    p, c, cw = S5_STATE, S5_GROUP, S5_CW
    vec = lambda a: a.reshape(n, 1, p)
    ldt = jnp.broadcast_to(log_dt.reshape(n, 1, 1), (n, 1, p))
    bt = lambda a: a.transpose(0, 1, 3, 2).reshape(n, c, p)
    cc = lambda a: a.reshape(n, c, p)
    steps = np.arange(S5_CHUNK, dtype=np.float32)
    rows = lambda e: jnp.asarray(np.broadcast_to(e[:, :, None], (2, S5_CHUNK, p)).copy())
    e_k = rows(np.stack([steps, steps]))
    e_w = rows(np.stack([S5_CHUNK - 1 - steps, steps]))
    e_e = rows(np.stack([steps + 1, S5_CHUNK - steps]))
    per = lambda *shape: pl.BlockSpec((None,) + shape, lambda i: (i,) + (0,) * len(shape))
    per_dir = pl.BlockSpec((None, S5_CHUNK, p), lambda i: (i // S5_GROUPS, 0, 0))
    out = lambda *shape: jax.ShapeDtypeStruct((n,) + shape, F32)
    return pl.pallas_call(
        _s5_param_kernel,
        grid=(n,),
        in_specs=[per(1, p)] * 3 + [per(c, p)] * 4 + [per_dir] * 3,
        out_specs=[per(c, cw)] + [per(cw, p)] * 4 + [per(1, p)] * 2,
        out_shape=[out(c, cw)] + [out(cw, p)] * 4 + [out(1, p)] * 2,
        compiler_params=_params("parallel"),
        name="s5_params",
    )(vec(lam_re), vec(lam_im), ldt, bt(b_re), bt(b_im), cc(c_re), cc(c_im), e_k, e_w, e_e)


def _s5_toeplitz(kt):
    g, c, n = S5_GROUPS, S5_GROUP, S5_CHUNK
    k = kt.reshape(2, g, c, n, c)
    s_in = np.arange(n)[:, None]
    s_out = np.arange(n)[None, :]
    lag_f = np.clip(s_out - s_in, 0, n - 1)
    lag_b = np.clip(s_in - s_out, 0, n - 1)
    m_f = jnp.asarray((s_out >= s_in).astype(np.float32))[None, None, :, :, None]
    m_b = jnp.asarray((s_in >= s_out).astype(np.float32))[None, None, :, :, None]
    m = k[0][:, :, lag_f, :] * m_f + k[1][:, :, lag_b, :] * m_b
    return m.transpose(0, 2, 1, 3, 4).reshape(g, n * c, n * c)


def _s5_operators(lam_re, lam_im, log_dt, b_re, b_im, c_re, c_im):
    kt, w_re, w_im, e_re, e_im, a_re, a_im = _s5_params(lam_re, lam_im, log_dt, b_re, b_im, c_re, c_im)
    g = S5_GROUPS
    parts = lambda re, im: jnp.stack([re[:g], im[:g], re[g:], im[g:]], axis=1)
    return (_s5_toeplitz(kt).astype(BF16), parts(w_re, w_im).astype(BF16), parts(e_re, e_im).astype(BF16),
            parts(a_re[:, 0], a_im[:, 0]))


def _s5_chunk_kernel(nk, rows, u_ref, m_ref, w_ref, e_ref, a_ref, y_ref, wfr_s, wfi_s, wbr_s, wbi_s):
    u = u_ref[...].astype(BF16)
    y_ref[...] = jnp.dot(u, m_ref[...], preferred_element_type=F32)
    for i, dst in enumerate((wfr_s, wfi_s, wbr_s, wbi_s)):
        dst[...] = jnp.dot(u, w_ref[i], preferred_element_type=F32)
    a = [jnp.broadcast_to(a_ref[i:i + 1, :], (rows, S5_STATE)) for i in range(4)]

    def step(k, carry):
        fr, fi, br, bi = carry
        sf = pl.ds(k, rows, stride=nk)
        sb = pl.ds(nk - 1 - k, rows, stride=nk)
        wfr, wfi, wbr, wbi = wfr_s[sf, :], wfi_s[sf, :], wbr_s[sb, :], wbi_s[sb, :]
        wfr_s[sf, :] = fr
        wfi_s[sf, :] = fi
        wbr_s[sb, :] = br
        wbi_s[sb, :] = bi
        return (a[0] * fr - a[1] * fi + wfr, a[0] * fi + a[1] * fr + wfi,
                a[2] * br - a[3] * bi + wbr, a[2] * bi + a[3] * br + wbi)

    zero = jnp.zeros((rows, S5_STATE), F32)
    lax.fori_loop(0, nk, step, (zero, zero, zero, zero))
    carry_in = 0.0
    for i, src in enumerate((wfr_s, wfi_s, wbr_s, wbi_s)):
        carry_in = carry_in + _mm_nt(src[...], e_ref[i])
    y_ref[...] += carry_in


def _s5_scan(z_groups, bsz, seq, operators):
    m_mat, w_mat, e_mat, a_vec = operators
    nk = seq // S5_CHUNK
    r = bsz * nk
    grp = lambda *shape: pl.BlockSpec((None,) + shape, lambda i: (i,) + (0,) * len(shape))
    return pl.pallas_call(
        functools.partial(_s5_chunk_kernel, nk, bsz),
        grid=(S5_GROUPS,),
        in_specs=[grp(r, S5_CW), grp(S5_CW, S5_CW), grp(4, S5_CW, S5_STATE), grp(4, S5_CW, S5_STATE),
                  grp(4, S5_STATE)],
        out_specs=grp(r, S5_CW),
        out_shape=jax.ShapeDtypeStruct((S5_GROUPS, r, S5_CW), F32),
        scratch_shapes=[pltpu.VMEM((r, S5_STATE), F32)] * 4,
        compiler_params=_params("parallel"),
        name="s5_scan",
    )(z_groups, m_mat, w_mat, e_mat, a_vec)


def _s5_gated(y_ref, u_ref, d_ref, w_ref, b_ref, rows_s):
    chunks = y_ref.shape[1]
    for g in range(S5_GROUPS):
        y_g = y_ref[g] + d_ref[g:g + 1, :] * u_ref[g]
        for half in range(S5_CW // LANES):
            rows_s[half, pl.ds(g, chunks, stride=S5_GROUPS), :] = y_g[:, half * LANES:(half + 1) * LANES]
    y = jnp.concatenate([rows_s[half] for half in range(S5_CW // LANES)], axis=1)
    y = jax.nn.gelu(_to_chunk_rows(y), approximate=True)
    return y * _sigmoid(_mm(y, w_ref[...]) + b_ref[...])


def _merge_kernel(x_ref, of_ref, ob_ref, bonus_ref, rgate_ref, mla_ref, s5y_ref, s5u_ref, g_ref,
                  gng_ref, gnb_ref, wrw_ref, wmla_ref, s5d_ref, s5gw_ref, s5gb_ref, ws5_ref, wout_ref,
                  lng_ref, lnb_ref, o_ref, rows_s):
    w = RW_W
    hr = lax.broadcasted_iota(jnp.int32, (w, w), 0) // RW_HEAD
    hc = lax.broadcasted_iota(jnp.int32, (w, w), 1) // RW_HEAD
    head_mean = (hr == hc).astype(F32) * (1.0 / RW_HEAD)
    o = of_ref[...] + ob_ref[...]
    mean = _mm_split(o, head_mean, 2, 1)
    oc = o - mean
    var = _mm_split(oc * oc, head_mean, 2, 1)
    o = oc * lax.rsqrt(var + RW_GN_EPS) * gng_ref[...] + gnb_ref[...]
    y_rw = _mm((o + bonus_ref[...]) * rgate_ref[...], wrw_ref[...])
    y_mla = _mm(mla_ref[...], wmla_ref[...])
    y_s5 = _mm(_s5_gated(s5y_ref, s5u_ref, s5d_ref, s5gw_ref, s5gb_ref, rows_s), ws5_ref[...])
    d = D_MODEL
    merged = g_ref[:, 0:d] * y_rw + g_ref[:, d:2 * d] * y_mla + g_ref[:, 2 * d:3 * d] * y_s5
    o_ref[...] = _layer_norm(DN_ALPHA * x_ref[...] + _mm(merged, wout_ref[...]), lng_ref[...], lnb_ref[...])


def _merge(x, o_f, o_b, bonus, rgate, o_mla, s5_y, s5_u, gates, gn_g, gn_b, w_rw, w_mla, s5_d, s5_glu_w, s5_glu_b,
           w_s5, w_out, ln_g, ln_b, tm):
    t = x.shape[0]
    row = lambda n: pl.BlockSpec((tm, n), lambda i: (i, 0))
    grp = pl.BlockSpec((S5_GROUPS, tm // S5_CHUNK, S5_CW), lambda i: (0, i, 0))
    weights = [gn_g, gn_b, w_rw, w_mla, s5_d, s5_glu_w, s5_glu_b, w_s5, w_out, ln_g, ln_b]
    return pl.pallas_call(
        _merge_kernel,
        grid=(t // tm,),
        in_specs=[row(D_MODEL), row(RW_W), row(RW_W), row(RW_W), row(RW_W), row(MLA_W), grp, grp,
                  row(N_BRANCH * D_MODEL)] + [_full(wt) for wt in weights],
        out_specs=row(D_MODEL),
        out_shape=jax.ShapeDtypeStruct((t, D_MODEL), F32),
        scratch_shapes=[pltpu.VMEM((S5_CW // LANES, tm, LANES), F32)],
        compiler_params=_params("parallel"),
        name="merge_ln1",
    )(x, o_f, o_b, bonus, rgate, o_mla, s5_y, s5_u, gates, *weights)


def _mlp_kernel(nf, x_ref, w1_ref, w2_ref, lng_ref, lnb_ref, o_ref, acc_s):
    j = pl.program_id(1)

    @pl.when(j == 0)
    def _():
        acc_s[...] = jnp.zeros_like(acc_s)

    h = jnp.maximum(_mm(x_ref[...], w1_ref[...]), 0.0)
    acc_s[...] += _mm(h * h, w2_ref[...])

    @pl.when(j == nf - 1)
    def _():
        o_ref[...] = _layer_norm(DN_ALPHA * x_ref[...] + acc_s[...], lng_ref[...], lnb_ref[...])


def _mlp(x, w1, w2, ln_g, ln_b, tm, tf):
    t = x.shape[0]
    nf = D_FF // tf
    return pl.pallas_call(
        functools.partial(_mlp_kernel, nf),
        grid=(t // tm, nf),
        in_specs=[pl.BlockSpec((tm, D_MODEL), lambda i, j: (i, 0)),
                  pl.BlockSpec((D_MODEL, tf), lambda i, j: (0, j)),
                  pl.BlockSpec((tf, D_MODEL), lambda i, j: (j, 0)),
                  _full(ln_g), _full(ln_b)],
        out_specs=pl.BlockSpec((tm, D_MODEL), lambda i, j: (i, 0)),
        out_shape=jax.ShapeDtypeStruct((t, D_MODEL), F32),
        scratch_shapes=[pltpu.VMEM((tm, D_MODEL), F32)],
        compiler_params=_params("parallel", "arbitrary"),
        name="mlp_ln2",
    )(x, w1, w2, ln_g, ln_b)


def _tile(n, pref):
    t = min(n, pref)
    assert n % t == 0, (n, pref)
    return t


def _prep_layer(w_in, rw_mu, rw_w0, rw_w2, rw_a0, rw_a2, rw_g2, rw_k_k, rw_k_a, rw_r_k, rw_gn_g, rw_gn_b, rw_proj,
                mla_q_norm, mla_w_uq, mla_kv_norm, mla_w_ukv, mla_proj,
                s5_lam_re, s5_lam_im, s5_log_dt, s5_b_re, s5_b_im, s5_c_re, s5_c_im, s5_d, s5_glu_w, s5_glu_b,
                s5_proj, w_out, ln1_g, ln1_b, mlp_w1, mlp_w2, ln2_g, ln2_b):
    row = lambda a: a.reshape(1, -1)
    p = {}
    p["w_rw"] = w_in[:, :OFF_MLA].astype(BF16)
    w_mla = w_in[:, OFF_MLA:OFF_S5]
    zeros = lambda n: jnp.zeros((D_MODEL, n), F32)
    p["w_mla"] = jnp.concatenate([w_mla[:, :MLA_Q_LORA + MLA_KV_LORA], zeros(MLA_NOPE),
                                  w_mla[:, MLA_Q_LORA + MLA_KV_LORA:],
                                  zeros(MLA_QK_PAD - MLA_NOPE - MLA_ROPE)], axis=1).astype(BF16)
    p["w_s5"] = w_in[:, OFF_S5:OFF_GATE].astype(BF16)
    p["w_gate"] = w_in[:, OFF_GATE:].astype(BF16)
    p["rw"] = [row(rw_mu)]
    p["rw_dir"] = [rw_w0[:, None, :], rw_w2, rw_a0[:, None, :], rw_a2]
    p["rw_shared"] = [rw_g2, row(rw_k_k), row(rw_k_a), row(rw_r_k)]
    p["rw_gn"] = [row(rw_gn_g), row(rw_gn_b)]
    p["rw_proj"] = rw_proj.astype(BF16)
    uq = mla_w_uq.reshape(MLA_Q_LORA, MLA_HEADS, MLA_NOPE + MLA_ROPE)
    uq = jnp.pad(uq, ((0, 0), (0, 0), (0, MLA_QK_PAD - MLA_NOPE - MLA_ROPE)))
    p["w_uq"] = uq.transpose(1, 2, 0).astype(BF16)
    ukv = mla_w_ukv.reshape(MLA_KV_LORA, MLA_HEADS, MLA_NOPE + MLA_V)
    uk = jnp.pad(ukv[:, :, :MLA_NOPE], ((0, 0), (0, 0), (0, MLA_QK_PAD - MLA_NOPE)))
    p["w_uk"] = uk.transpose(1, 0, 2).astype(BF16)
    p["w_uv"] = ukv[:, :, MLA_NOPE:].transpose(1, 2, 0).astype(BF16)
    p["q_norm"] = row(mla_q_norm)
    p["kv_norm"] = row(mla_kv_norm)
    p["mla_proj"] = mla_proj.astype(BF16)
    p["s5_ops"] = _s5_operators(s5_lam_re, s5_lam_im, s5_log_dt, s5_b_re, s5_b_im, s5_c_re, s5_c_im)
    d_rows = jnp.broadcast_to(s5_d.reshape(S5_GROUPS, 1, S5_GROUP), (S5_GROUPS, S5_CHUNK, S5_GROUP))
    p["s5_post"] = [d_rows.reshape(S5_GROUPS, S5_CW), s5_glu_w.astype(BF16), row(s5_glu_b)]
    p["s5_proj"] = s5_proj.astype(BF16)
    p["w_out"] = w_out.astype(BF16)
    p["ln1"] = [row(ln1_g), row(ln1_b)]
    p["w1"] = mlp_w1.astype(BF16)
    p["w2"] = mlp_w2.astype(BF16)
    p["ln2"] = [row(ln2_g), row(ln2_b)]
    return p


def _layer(x, p, bsz, seq, rope):
    t = bsz * seq
    tm = _tile(t, 512)
    z_rw, z_mla, z_s5, gates = _inproj(x, p["w_rw"], p["w_mla"], p["w_s5"], p["w_gate"], _tile(t, 256))

    tb = _tile(seq, 512)
    z3 = z_rw.reshape(bsz, seq, RW_IN)
    o_f, bonus, rgate, o_b = _rwkv_scan(z3, *p["rw"], *p["rw_dir"], *p["rw_shared"], tb=tb)
    flat = lambda a: a.reshape(t, RW_W)

    q, k, v = _mla_qkv(z_mla, rope, p["q_norm"], p["w_uq"], p["kv_norm"], p["w_uk"], p["w_uv"],
                       seq, _tile(seq, 512))
    o_mla = _mla_attn(q, k, v, bsz, seq, _tile(seq, 1024), _tile(seq, 1024))

    s5_y = _s5_scan(z_s5, bsz, seq, p["s5_ops"])

    x1 = _merge(x, flat(o_f), flat(o_b), flat(bonus), flat(rgate), o_mla, s5_y, z_s5, gates,
                *p["rw_gn"], p["rw_proj"], p["mla_proj"], *p["s5_post"], p["s5_proj"], p["w_out"], *p["ln1"], tm)
    return _mlp(x1, p["w1"], p["w2"], *p["ln2"], _tile(t, 1024), 1024)


def _trunk(x, layers):
    bsz, seq, _ = x.shape
    assert seq % RW_CHUNK == 0 and seq % S5_CHUNK == 0
    rope = _rope_tables(seq)
    h = x.reshape(bsz * seq, D_MODEL)
    for p in layers:
        h = _layer(h, p, bsz, seq, rope)
    return h.reshape(bsz, seq, D_MODEL)


def kernel(x_prompt, x_sample, w_in, rw_mu, rw_w0, rw_w2, rw_a0, rw_a2, rw_g2, rw_k_k, rw_k_a, rw_r_k, rw_gn_g, rw_gn_b, rw_proj, mla_q_norm, mla_w_uq, mla_kv_norm, mla_w_ukv, mla_proj, s5_lam_re, s5_lam_im, s5_log_dt, s5_b_re, s5_b_im, s5_c_re, s5_c_im, s5_d, s5_glu_w, s5_glu_b, s5_proj, w_out, ln1_g, ln1_b, mlp_w1, mlp_w2, ln2_g, ln2_b):
    weights = (w_in, rw_mu, rw_w0, rw_w2, rw_a0, rw_a2, rw_g2, rw_k_k, rw_k_a, rw_r_k, rw_gn_g, rw_gn_b, rw_proj,
               mla_q_norm, mla_w_uq, mla_kv_norm, mla_w_ukv, mla_proj,
               s5_lam_re, s5_lam_im, s5_log_dt, s5_b_re, s5_b_im, s5_c_re, s5_c_im, s5_d, s5_glu_w, s5_glu_b,
               s5_proj, w_out, ln1_g, ln1_b, mlp_w1, mlp_w2, ln2_g, ln2_b)
    layers = [_prep_layer(*[wt[l] for wt in weights]) for l in range(w_in.shape[0])]
    return _trunk(x_prompt, layers), _trunk(x_sample, layers)
```
